```python
import math
import jax, jax.numpy as jnp
from jax import lax
import numpy as np

D_MODEL = 2048
BATCH = 16
SEQ = 256
DEPTH = 4
DEC_BATCH = 2
DEC_SEQ = 1024
PAST_LEN = 256

GRID_W = 64
N_EVEN = (DEPTH + 1) // 2
N_ODD = DEPTH // 2
EPS = 1e-6
H_A = 8
DK_A = 128
DV_A = 128
CONV_K = 5
DELTA_CHUNK = 64
H_B = 8
DQK_B = 64
DV_B = 2 * DQK_B
Q_BLOCK = 128
ROPE_BASE = 10000.0
H_C = 4
DK_C = D_MODEL // 2 // H_C
DV_C = D_MODEL // H_C
GATE_RANK = 16
GLA_NORMALIZER = 16.0
GLA_CHUNK = 16
WIDTHS_EVEN = (H_A * DK_A, H_A * DK_A, H_A * DV_A, H_A * DV_A, 2 * H_A, 2 * H_A,
               H_B * 2 * DQK_B, H_B * 2 * DQK_B, H_B * DV_B, H_B * DV_B)
D_IN_EVEN = sum(WIDTHS_EVEN)
MIX_EVEN = H_A * DV_A + H_B * DV_B
WIDTHS_ODD = (H_C * DK_C, H_C * DK_C, H_C * DV_C, H_C * DV_C, 2 * GATE_RANK)
D_IN_ODD = sum(WIDTHS_ODD)
MIX_ODD = H_C * DV_C

kernel_name = 'hybrid_dit_prefix_step'


def _split(x, widths):
    out, start = [], 0
    for w in widths:
        out.append(x[..., start:start + w])
        start += w
    return out


def rms_norm(x, g):
    xf = x.astype(jnp.float32)
    y = xf * lax.rsqrt(jnp.mean(xf * xf, axis=-1, keepdims=True) + EPS)
    return (y * g.astype(jnp.float32)).astype(x.dtype)


def l2_normalize(x):
    xf = x.astype(jnp.float32)
    return xf * lax.rsqrt(jnp.sum(xf * xf, axis=-1, keepdims=True) + EPS)


def adaln(cvec, w, b):
    m = jax.nn.silu(cvec) @ w + b
    return jnp.split(m[:, None, :], 3, axis=-1)


def short_conv(x, w):
    return lax.conv_general_dilated(
        x, w[:, None, :].astype(x.dtype), window_strides=(1,),
        padding=[(CONV_K // 2, CONV_K // 2)],
        dimension_numbers=('NWC', 'WIO', 'NWC'), feature_group_count=x.shape[-1])


def _chunk(a, c):
    b, t = a.shape[:2]
    return jnp.swapaxes(a.reshape((b, t // c, c) + a.shape[2:]), 2, 3)


def _unchunk(o):
    n, b, h, c, d = o.shape
    return jnp.transpose(o, (1, 0, 3, 2, 4)).reshape(b, n * c, h, d)


def gated_delta_rule(q, k, v, beta, g, s0):
    f32 = jnp.float32
    dv = v.shape[-1]
    q, k, v, beta, g = (_chunk(a.astype(f32), DELTA_CHUNK) for a in (q, k, v, beta, g))
    q = q * DK_A ** -0.5
    G = jnp.cumsum(g, axis=-1)
    tri = jnp.tril(jnp.ones((DELTA_CHUNK, DELTA_CHUNK), bool))
    decay = jnp.exp(jnp.where(tri, G[..., :, None] - G[..., None, :], -jnp.inf))
    kb = k * beta[..., None]
    a_mat = jnp.einsum('bnhid,bnhjd->bnhij', kb, k) * decay
    rhs = jnp.concatenate([v * beta[..., None], kb * jnp.exp(G)[..., None]], axis=-1)
    sol = lax.linalg.triangular_solve(a_mat, rhs, left_side=True, lower=True, unit_diagonal=True)
    w_v, w_k = sol[..., :dv], sol[..., dv:]
    qk = jnp.einsum('bnhid,bnhjd->bnhij', q, k) * decay
    q_dec = q * jnp.exp(G)[..., None]
    k_dec = k * jnp.exp(G[..., -1:] - G)[..., None]
    g_last = jnp.exp(G[..., -1])
    xs = tuple(jnp.moveaxis(a, 1, 0) for a in (w_v, w_k, qk, q_dec, k_dec, g_last))

    def step(S, inp):
        wv, wk, qk_c, qd, kd, gl = inp
        u = wv - jnp.einsum('bhcd,bhde->bhce', wk, S)
        o = jnp.einsum('bhcd,bhde->bhce', qd, S) + jnp.einsum('bhij,bhje->bhie', qk_c, u)
        S = S * gl[..., None, None] + jnp.einsum('bhcd,bhce->bhde', kd, u)
        return S, o

    s_fin, o = lax.scan(step, s0.astype(f32), xs)
    return _unchunk(o), s_fin


def gla_chunked(q, k, v, log_a, s0):
    f32 = jnp.float32
    q, k, v, la = (_chunk(a.astype(f32), GLA_CHUNK) for a in (q, k, v, log_a))
    q = q * DK_C ** -0.5
    Bc = jnp.cumsum(la, axis=-2)
    tri = jnp.tril(jnp.ones((GLA_CHUNK, GLA_CHUNK), bool))
    rel = jnp.exp(jnp.where(tri[..., None], Bc[..., :, None, :] - Bc[..., None, :, :], -jnp.inf))
    attn = jnp.einsum('bnhid,bnhjd,bnhijd->bnhij', q, k, rel)
    o_intra = jnp.einsum('bnhij,bnhje->bnhie', attn, v)
    q_dec = q * jnp.exp(Bc)
    k_dec = k * jnp.exp(Bc[..., -1:, :] - Bc)
    g_last = jnp.exp(Bc[..., -1, :])
    xs = tuple(jnp.moveaxis(a, 1, 0) for a in (q_dec, k_dec, v, o_intra, g_last))

    def step(S, inp):
        qd, kd, vc, oi, gl = inp
        o = jnp.einsum('bhcd,bhde->bhce', qd, S) + oi
        S = S * gl[..., None] + jnp.einsum('bhcd,bhce->bhde', kd, vc)
        return S, o

    s_fin, o = lax.scan(step, s0.astype(f32), xs)
    return _unchunk(o), s_fin


def axial_rope_tables(n_tokens, dim):
    rows = n_tokens // GRID_W
    row = jnp.repeat(jnp.arange(rows), GRID_W).astype(jnp.float32)
    col = jnp.tile(jnp.arange(GRID_W), rows).astype(jnp.float32)
    quarter = dim // 4
    freqs = ROPE_BASE ** (-jnp.arange(quarter, dtype=jnp.float32) / quarter)
    ar = row[:, None] * freqs
    ac = col[:, None] * freqs
    ang = jnp.concatenate([ar, ar, ac, ac], axis=-1)
    return jnp.cos(ang), jnp.sin(ang)


def apply_axial_rope(x, cos, sin):
    x1, x2, x3, x4 = jnp.split(x, 4, axis=-1)
    rot = jnp.concatenate([-x2, x1, -x4, x3], axis=-1)
    cos = cos[None, :, None, None, :]
    sin = sin[None, :, None, None, :]
    return (x.astype(jnp.float32) * cos + rot.astype(jnp.float32) * sin).astype(x.dtype)


def diff_softmax_attention(q, k, v, lam):
    b, t, h = q.shape[:3]
    qb = jnp.moveaxis(q.reshape((b, t // Q_BLOCK, Q_BLOCK) + q.shape[2:]), 1, 0)
    vf = v.astype(jnp.float32)

    def block(qi):
        s = jnp.einsum('bqhmd,bkhmd->bhmqk', qi, k).astype(jnp.float32) * DQK_B ** -0.5
        p = jax.nn.softmax(s, axis=-1)
        w = p[:, :, 0] - lam * p[:, :, 1]
        return jnp.einsum('bhqk,bkhd->bqhd', w, vf)

    o = lax.map(block, qb)
    return jnp.moveaxis(o, 0, 1).reshape(b, t, h, v.shape[-1])


def even_mixer(h, w_in, conv_w, a_log, dt_bias, gdn_g, lam_p, diff_g, w_out, lam_init, ctx):
    f32 = jnp.float32
    b, t, _ = h.shape
    qa, ka, va, za, ba, aa, qb, kb, vb, zb = _split(h @ w_in, WIDTHS_EVEN)
    qkv = jax.nn.silu(short_conv(jnp.concatenate([qa, ka, va], axis=-1), conv_w))
    qa, ka, va = _split(qkv, (H_A * DK_A, H_A * DK_A, H_A * DV_A))
    qa = l2_normalize(qa.reshape(b, t, H_A, DK_A))
    ka = l2_normalize(ka.reshape(b, t, H_A, DK_A))
    va = va.reshape(b, t, H_A, DV_A)
    beta = jax.nn.sigmoid(ba.astype(f32)).reshape(b, t, 2, H_A)
    g = -jnp.exp(a_log.astype(f32)) * jax.nn.softplus(
        aa.astype(f32).reshape(b, t, 2, H_A) + dt_bias.astype(f32))
    if ctx is None:
        s_f0 = jnp.zeros((b, H_A, DK_A, DV_A), f32)
        s_b0 = s_f0
    else:
        s_f0, s_b0, k_ctx, v_ctx = ctx
    o_f, s_f = gated_delta_rule(qa, ka, va, beta[:, :, 0], g[:, :, 0], s_f0)
    o_r, s_b = gated_delta_rule(qa[:, ::-1], ka[:, ::-1], va[:, ::-1],
                                beta[:, ::-1, 1], g[:, ::-1, 1], s_b0)
    o_a = rms_norm((o_f + o_r[:, ::-1]).astype(h.dtype), gdn_g) * jax.nn.silu(za.reshape(b, t, H_A, DV_A))
    q = qb.reshape(b, t, H_B, 2, DQK_B)
    k = kb.reshape(b, t, H_B, 2, DQK_B)
    v = vb.reshape(b, t, H_B, DV_B)
    if ctx is None:
        k_all, v_all = k, v
    else:
        cos, sin = axial_rope_tables(t, DQK_B)
        q = apply_axial_rope(q, cos, sin)
        k_all = jnp.concatenate([apply_axial_rope(k, cos, sin),
                                 k_ctx.reshape(b, -1, H_B, 2, DQK_B)], axis=1)
        v_all = jnp.concatenate([v, v_ctx], axis=1)
    lp = lam_p.astype(f32)
    lam = jnp.exp(jnp.sum(lp[0] * lp[1])) - jnp.exp(jnp.sum(lp[2] * lp[3])) + lam_init
    o = diff_softmax_attention(q, k_all, v_all, lam)
    o_bb = rms_norm(o.astype(h.dtype), diff_g) * (1.0 - lam_init) * jax.nn.silu(zb.reshape(b, t, H_B, DV_B))
    out = jnp.concatenate([o_a.reshape(b, t, -1), o_bb.reshape(b, t, -1)], axis=-1) @ w_out
    state = None if ctx is not None else (s_f, s_b, k.reshape(b, t, H_B, 2 * DQK_B), v)
    return out, state


def odd_mixer(h, w_in, w_gate, b_gate, gla_g, w_out, ctx):
    f32 = jnp.float32
    b, t, _ = h.shape
    q, k, v, z, glr = _split(h @ w_in, WIDTHS_ODD)
    q = q.reshape(b, t, H_C, DK_C)
    k = k.reshape(b, t, H_C, DK_C)
    v = v.reshape(b, t, H_C, DV_C)
    gate = jnp.einsum('btrl,rlk->btrk', glr.reshape(b, t, 2, GATE_RANK).astype(f32),
                      w_gate.astype(f32)) + b_gate.astype(f32)
    log_a = (jax.nn.log_sigmoid(gate) / GLA_NORMALIZER).reshape(b, t, 2, H_C, DK_C)
    if ctx is None:
        s_f0 = jnp.zeros((b, H_C, DK_C, DV_C), f32)
        s_b0 = s_f0
    else:
        s_f0, s_b0 = ctx
    o_f, s_f = gla_chunked(q, k, v, log_a[:, :, 0], s_f0)
    o_r, s_b = gla_chunked(q[:, ::-1], k[:, ::-1], v[:, ::-1], log_a[:, ::-1, 1], s_b0)
    o = rms_norm((o_f + o_r[:, ::-1]).astype(h.dtype), gla_g) * jax.nn.silu(z.reshape(b, t, H_C, DV_C))
    out = o.reshape(b, t, -1) @ w_out
    state = None if ctx is not None else (s_f, s_b)
    return out, state


def setup_inputs(seed: int = 0) -> dict:
    key = jax.random.key(seed)
    ks = jax.random.split(key, 26)
    f32 = jnp.float32
    D = D_MODEL

    def nrm(k, shape, scale):
        return jax.random.normal(k, shape, f32) * scale

    def gain(k, shape):
        return 1.0 + 0.02 * jax.random.normal(k, shape, f32)

    dt = jnp.exp(jax.random.uniform(ks[13], (N_EVEN, 2, H_A), f32, math.log(1e-3), math.log(1e-1)))
    return {
        'x_prompt': nrm(ks[0], (BATCH, SEQ, D), 1.0),
        'x_sample': nrm(ks[1], (DEC_BATCH, DEC_SEQ, D), 1.0),
        'c': nrm(ks[2], (DEC_BATCH, D), 1.0),
        'state_gdn': nrm(ks[3], (DEC_BATCH, N_EVEN, 2, H_A, DK_A, DV_A), 0.1),
        'cache_k': nrm(ks[4], (DEC_BATCH, N_EVEN, PAST_LEN, H_B, 2 * DQK_B), 1.0),
        'cache_v': nrm(ks[5], (DEC_BATCH, N_EVEN, PAST_LEN, H_B, DV_B), 1.0),
        'state_gla': nrm(ks[6], (DEC_BATCH, N_ODD, 2, H_C, DK_C, DV_C), 1.0),
        'c_ctx': nrm(ks[7], (D,), 1.0),
        'w_ada': nrm(ks[8], (DEPTH, D, 3 * D), D ** -0.5),
        'b_ada': nrm(ks[9], (DEPTH, 3 * D), 0.01),
        'norm_pre': gain(ks[10], (DEPTH, D)),
        'norm_post': gain(ks[11], (DEPTH, D)),
        'w_in_even': nrm(ks[12], (N_EVEN, D, D_IN_EVEN), D ** -0.5),
        'conv_even': nrm(ks[14], (N_EVEN, CONV_K, 2 * H_A * DK_A + H_A * DV_A), CONV_K ** -0.5),
        'a_log_even': jnp.log(jax.random.uniform(ks[15], (N_EVEN, 2, H_A), f32, 1.0, 16.0)),
        'dt_bias_even': dt + jnp.log(-jnp.expm1(-dt)),
        'gdn_norm_even': gain(ks[16], (N_EVEN, DV_A)),
        'lam_even': nrm(ks[17], (N_EVEN, 4, DQK_B), 0.1),
        'diff_norm_even': gain(ks[18], (N_EVEN, DV_B)),
        'w_out_even': nrm(ks[19], (N_EVEN, MIX_EVEN, D), MIX_EVEN ** -0.5),
        'w_in_odd': nrm(ks[20], (N_ODD, D, D_IN_ODD), D ** -0.5),
        'w_gate_odd': nrm(ks[21], (N_ODD, 2, GATE_RANK, H_C * DK_C), GATE_RANK ** -0.5),
        'b_gate_odd': nrm(ks[22], (N_ODD, 2, H_C * DK_C), 0.1),
        'gla_norm_odd': gain(ks[23], (N_ODD, DV_C)),
        'w_out_odd': nrm(ks[24], (N_ODD, MIX_ODD, D), MIX_ODD ** -0.5),
    }


def reference(x_prompt, x_sample, c, state_gdn, cache_k, cache_v, state_gla, c_ctx,
              w_ada, b_ada, norm_pre, norm_post, w_in_even, conv_even, a_log_even,
              dt_bias_even, gdn_norm_even, lam_even, diff_norm_even, w_out_even,
              w_in_odd, w_gate_odd, b_gate_odd, gla_norm_odd, w_out_odd):
    y_p, y_s = x_prompt, x_sample
    gdn_states, ctx_keys, ctx_vals, gla_states = [], [], [], []
    for layer in range(DEPTH):
        sh_p, sc_p, gt_p = adaln(c_ctx[None, :], w_ada[layer], b_ada[layer])
        sh_s, sc_s, gt_s = adaln(c, w_ada[layer], b_ada[layer])
        h_p = rms_norm(y_p, norm_pre[layer]) * (1.0 + sc_p) + sh_p
        h_s = rms_norm(y_s, norm_pre[layer]) * (1.0 + sc_s) + sh_s
        if layer % 2 == 0:
            e = layer // 2
            lam_init = 0.8 - 0.6 * math.exp(-0.3 * layer)
            prm = (w_in_even[e], conv_even[e], a_log_even[e], dt_bias_even[e], gdn_norm_even[e],
                   lam_even[e], diff_norm_even[e], w_out_even[e], lam_init)
            o_p, (s_f, s_b, k_c, v_c) = even_mixer(h_p, *prm, None)
            gdn_states.append(jnp.stack([s_f, s_b], axis=1).astype(x_prompt.dtype))
            ctx_keys.append(k_c)
            ctx_vals.append(v_c)
            o_s, _ = even_mixer(h_s, *prm, (state_gdn[:, e, 0], state_gdn[:, e, 1],
                                            cache_k[:, e], cache_v[:, e]))
        else:
            od = layer // 2
            prm = (w_in_odd[od], w_gate_odd[od], b_gate_odd[od], gla_norm_odd[od], w_out_odd[od])
            o_p, (s_f, s_b) = odd_mixer(h_p, *prm, None)
            gla_states.append(jnp.stack([s_f, s_b], axis=1).astype(x_prompt.dtype))
            o_s, _ = odd_mixer(h_s, *prm, (state_gla[:, od, 0], state_gla[:, od, 1]))
        y_p = y_p + gt_p * rms_norm(o_p, norm_post[layer])
        y_s = y_s + gt_s * rms_norm(o_s, norm_post[layer])
    new_state_gdn = jnp.stack(gdn_states, axis=1)
    new_cache_k = jnp.stack(ctx_keys, axis=1)
    new_cache_v = jnp.stack(ctx_vals, axis=1)
    new_state_gla = jnp.stack(gla_states, axis=1)
    return (y_p, y_s, new_state_gdn, new_cache_k, new_cache_v, new_state_gla)
```

```python
import functools
import math

import jax
import jax.numpy as jnp
from jax import lax
from jax.experimental import pallas as pl
from jax.experimental.pallas import tpu as pltpu

F32 = jnp.float32
BF16 = jnp.bfloat16
HIGHEST = lax.Precision.HIGHEST

D_MODEL = 2048
DEPTH = 4
EPS = 1e-6
GRID_W = 64
H_A = 8
DK_A = 128
DV_A = 128
CONV_K = 5
CHUNK_A = 64
H_B = 8
DQK_B = 64
DV_B = 128
ROPE_BASE = 10000.0
Q_TILE = 128
H_C = 4
DK_C = 256
DV_C = 512
GATE_RANK = 16
GLA_NORMALIZER = 16.0
CHUNK_C = 64

LANES = 128
VMEM_LIMIT = 48 * 1024 * 1024
ROW_TILE = 256


def _params(*semantics):
    return pltpu.CompilerParams(dimension_semantics=semantics,
                                vmem_limit_bytes=VMEM_LIMIT)


def _mm(a, b):
    return jnp.dot(a.astype(BF16), b.astype(BF16), preferred_element_type=F32)


def _mm_nt(a, b):
    return lax.dot_general(a.astype(BF16), b.astype(BF16),
                           (((1,), (1,)), ((), ())), preferred_element_type=F32)


def _mm_tn(a, b):
    return lax.dot_general(a.astype(BF16), b.astype(BF16),
                           (((0,), (0,)), ((), ())), preferred_element_type=F32)


def _mm_hi(a, b):
    return jnp.dot(a, b, precision=HIGHEST, preferred_element_type=F32)


def _mm_tn_hi(a, b):
    return lax.dot_general(a, b, (((0,), (0,)), ((), ())), precision=HIGHEST,
                           preferred_element_type=F32)


def _silu(x):
    return x * jax.nn.sigmoid(x)


def _softplus(x):
    return jnp.maximum(x, 0.0) + jnp.log1p(jnp.exp(-jnp.abs(x)))


def _rms(x):
    return x * lax.rsqrt(jnp.mean(x * x, axis=-1, keepdims=True) + EPS)


def _adaln_kernel(c_ref, w_ref, b_ref, o_ref):
    o_ref[...] = _mm(_silu(c_ref[...]), w_ref[...]) + b_ref[...]


def _adaln(cvec, w_ada, b_ada):
    tn = 1024
    n3 = 3 * D_MODEL
    return pl.pallas_call(
        _adaln_kernel,
        grid=(DEPTH, n3 // tn),
        in_specs=[pl.BlockSpec((8, D_MODEL), lambda l, j: (0, 0)),
                  pl.BlockSpec((None, D_MODEL, tn), lambda l, j: (l, 0, j)),
                  pl.BlockSpec((None, 1, tn), lambda l, j: (l, 0, j))],
        out_specs=pl.BlockSpec((None, 8, tn), lambda l, j: (l, 0, j)),
        out_shape=jax.ShapeDtypeStruct((DEPTH, 8, n3), F32),
        compiler_params=_params("parallel", "parallel"),
        name="adaln",
    )(cvec, w_ada, b_ada.reshape(DEPTH, 1, n3))


def _prenorm_kernel(x_ref, g_ref, m_ref, h_ref):
    y = _rms(x_ref[...]) * g_ref[...]
    m = m_ref[...]
    h_ref[...] = (y * (1.0 + m[:, D_MODEL:2 * D_MODEL]) + m[:, :D_MODEL]).astype(BF16)


def _group_index(layer, group0, tiles_per_group):
    if tiles_per_group is None:
        return lambda i: (layer * 8 + group0, 0, 0)
    return lambda i: (layer * 8 + group0 + i // tiles_per_group, 0, 0)


def _prenorm(y, gain, mods, gidx):
    m = y.shape[0]
    return pl.pallas_call(
        _prenorm_kernel,
        grid=(m // ROW_TILE,),
        in_specs=[pl.BlockSpec((ROW_TILE, D_MODEL), lambda i: (i, 0)),
                  pl.BlockSpec((1, D_MODEL), lambda i: (0, 0)),
                  pl.BlockSpec((None, 1, 3 * D_MODEL), gidx)],
        out_specs=pl.BlockSpec((ROW_TILE, D_MODEL), lambda i: (i, 0)),
        out_shape=jax.ShapeDtypeStruct((m, D_MODEL), BF16),
        compiler_params=_params("parallel"),
        name="prenorm",
    )(y, gain.reshape(1, D_MODEL), mods)


def _matmul_kernel(x_ref, w_ref, o_ref):
    o_ref[...] = jnp.dot(x_ref[...], w_ref[...], preferred_element_type=F32)


def _matmul(x, w, tm, tn):
    m, k = x.shape
    n = w.shape[1]
    return pl.pallas_call(
        _matmul_kernel,
        grid=(n // tn, m // tm),
        in_specs=[pl.BlockSpec((tm, k), lambda j, i: (i, 0)),
                  pl.BlockSpec((k, tn), lambda j, i: (0, j))],
        out_specs=pl.BlockSpec((tm, tn), lambda j, i: (i, j)),
        out_shape=jax.ShapeDtypeStruct((m, n), F32),
        compiler_params=_params("parallel", "parallel"),
        name="inproj",
    )(x, w)


def _outproj_kernel(*refs, n_in):
    x_refs = refs[:n_in]
    w_ref, y_ref, g_ref, m_ref, o_ref = refs[n_in:]
    acc = None
    off = 0
    for x_ref in x_refs:
        kk = x_ref.shape[1]
        part = jnp.dot(x_ref[...], w_ref[off:off + kk, :], preferred_element_type=F32)
        acc = part if acc is None else acc + part
        off += kk
    m = m_ref[...]
    o_ref[...] = y_ref[...] + m[:, 2 * D_MODEL:] * (_rms(acc) * g_ref[...])


def _outproj(xs, w, y, gain, mods, gidx):
    m = y.shape[0]
    in_specs = [pl.BlockSpec((ROW_TILE, x.shape[1]), lambda i: (i, 0)) for x in xs]
    in_specs += [pl.BlockSpec(w.shape, lambda i: (0, 0)),
                 pl.BlockSpec((ROW_TILE, D_MODEL), lambda i: (i, 0)),
                 pl.BlockSpec((1, D_MODEL), lambda i: (0, 0)),
                 pl.BlockSpec((None, 1, 3 * D_MODEL), gidx)]
    return pl.pallas_call(
        functools.partial(_outproj_kernel, n_in=len(xs)),
        grid=(m // ROW_TILE,),
        in_specs=in_specs,
        out_specs=pl.BlockSpec((ROW_TILE, D_MODEL), lambda i: (i, 0)),
        out_shape=jax.ShapeDtypeStruct((m, D_MODEL), F32),
        compiler_params=_params("parallel"),
        name="outproj",
    )(*xs, w, y, gain.reshape(1, D_MODEL), mods)


def _gdn_kernel(*refs, seq_len, has_state):
    t = seq_len
    c = CHUNK_A
    nc = t // c
    (q_ref, k_ref, v_ref, z_ref, gp_ref, cwq_ref, cwk_ref, cwv_ref,
     alog_ref, dtb_ref, gn_ref) = refs[:11]
    if has_state:
        s0_ref, o_ref = refs[11:13]
        scratch = refs[13:]
    else:
        o_ref, sf_ref = refs[11:13]
        scratch = refs[13:]
    qn, kn, vn, bt, gt, wv_s, wkqd_s, qk_s, kd_s, gl_s, oacc, s_s = scratch
    head = pl.program_id(1)

    row_t = lax.broadcasted_iota(jnp.int32, (t, LANES), 0)

    def conv_silu(x_ref, w_ref):
        x = x_ref[...]
        w = w_ref[...]
        acc = x * w[CONV_K // 2:CONV_K // 2 + 1, :]
        for tap in range(CONV_K):
            d = tap - CONV_K // 2
            if d == 0:
                continue
            shifted = pltpu.roll(x, (-d) % t, axis=0)
            valid = (row_t >= -d) if d < 0 else (row_t < t - d)
            acc = acc + jnp.where(valid, shifted, 0.0) * w[tap:tap + 1, :]
        return _silu(acc)

    def l2n(x):
        return x * lax.rsqrt(jnp.sum(x * x, axis=-1, keepdims=True) + EPS)

    qn[...] = l2n(conv_silu(q_ref, cwq_ref))
    kn[...] = l2n(conv_silu(k_ref, cwk_ref))
    vn[...] = conv_silu(v_ref, cwv_ref)

    gp = gp_ref[...]
    lane_t = lax.broadcasted_iota(jnp.int32, (t, LANES), 1)
    beta_all = jax.nn.sigmoid(gp)
    g_all = -jnp.exp(alog_ref[...]) * _softplus(gp + dtb_ref[...])

    def pick(x, idx):
        col = jnp.sum(jnp.where(lane_t == idx, x, 0.0), axis=-1, keepdims=True)
        return jnp.broadcast_to(col, (t, LANES))

    for d in range(2):
        bt[d] = pick(beta_all, d * H_A + head)
        gt[d] = pick(g_all, 2 * H_A + d * H_A + head)

    r2 = 2 * c
    row = lax.broadcasted_iota(jnp.int32, (r2, r2), 0)
    col = lax.broadcasted_iota(jnp.int32, (r2, r2), 1)
    rc_xor = row ^ col
    pos = row & (c - 1)

    def prep(gi, d):
        rev = d == 1
        after = (col - row) if rev else (row - col)
        after = jnp.where(rc_xor < c, after, -1)
        tri_inc = after >= 0
        tri_str = after > 0
        rows = pl.ds(pl.multiple_of(gi * r2, r2), r2)
        q = qn[rows, :] * DK_A ** -0.5
        k = kn[rows, :]
        v = vn[rows, :]
        beta = bt[d, rows, :]
        gcum = gt[d, rows, :]
        sh = 1
        while sh < c:
            if rev:
                gcum = gcum + jnp.where(pos < c - sh, pltpu.roll(gcum, r2 - sh, axis=0), 0.0)
            else:
                gcum = gcum + jnp.where(pos >= sh, pltpu.roll(gcum, sh, axis=0), 0.0)
            sh *= 2
        gdiff = gcum - gcum.T
        decay = jnp.where(tri_inc, jnp.exp(jnp.minimum(gdiff, 0.0)), 0.0)
        kb = k * beta
        amat = jnp.where(tri_str, _mm_nt(kb, k) * decay, 0.0)
        tinv = jnp.where(row == col, 1.0, 0.0) - jnp.where(rc_xor < 2, amat, 0.0)
        b = 2
        while b < c:
            a_off = jnp.where((rc_xor >= b) & (rc_xor < 2 * b), amat, 0.0)
            tinv = tinv - _mm(tinv, _mm(a_off, tinv))
            b *= 2
        eg = jnp.exp(gcum)
        x = _mm(tinv, jnp.concatenate([v * beta, kb * eg], axis=1))
        qk = _mm_nt(q, k) * decay
        qk = jnp.where(row < c, qk, pltpu.roll(qk, c, axis=1))[:, :c]
        g_end = _block_row_bcast(gcum, c, 0 if rev else c - 1)
        qd = q * eg
        wv_s[d, rows, :] = x[:, :DV_A]
        qk_s[d, rows, :] = qk
        kd_s[d, rows, :] = k * jnp.exp(g_end - gcum)
        gl = jnp.exp(g_end)
        for j in range(2):
            base = pl.multiple_of(gi * 2 * r2 + j * r2, r2)
            wkqd_s[d, pl.ds(base, c), :] = x[j * c:(j + 1) * c, DV_A:]
            wkqd_s[d, pl.ds(base + c, c), :] = qd[j * c:(j + 1) * c, :]
            gl_s[d, pl.ds(pl.multiple_of(gi * 16 + j * 8, 8), 8), :] = gl[j * c:j * c + 8, :]

    def rec(ci, d):
        rows = pl.ds(pl.multiple_of(ci * c, c), c)
        s = s_s[d]
        ws = _mm(wkqd_s[d, pl.ds(pl.multiple_of(ci * 2 * c, 2 * c), 2 * c), :], s)
        u = wv_s[d, rows, :] - ws[:c]
        o = ws[c:] + _mm(qk_s[d, rows, :], u)
        gl = gl_s[d, pl.ds(pl.multiple_of(ci * 8, 8), 1), :]
        s_s[d] = s * gl + _mm_tn(kd_s[d, rows, :], u)
        oacc[rows, :] += o

    for d in range(2):
        if has_state:
            s_s[d] = s0_ref[d]
        else:
            s_s[d] = jnp.zeros((DK_A, DV_A), F32)
    oacc[...] = jnp.zeros((t, DV_A), F32)

    def prep_body(gi, carry):
        prep(gi, 0)
        prep(gi, 1)
        return carry

    lax.fori_loop(0, nc // 2, prep_body, 0)

    def rec_body(i, carry):
        rec(i, 0)
        rec(nc - 1 - i, 1)
        return carry

    lax.fori_loop(0, nc, rec_body, 0)

    o_ref[...] = (_rms(oacc[...]) * gn_ref[...] * _silu(z_ref[...])).astype(BF16)
    if not has_state:
        sf_ref[0] = s_s[0]
        sf_ref[1] = s_s[1]


def _gdn(proj, gp, conv_w, alog_row, dtb_row, gnorm, state, layer_e, nseq, seq_len):
    t = seq_len
    nc = t // CHUNK_A
    has_state = state is not None

    def col(off):
        return pl.BlockSpec((t, LANES), lambda b, h: (b, off + h))

    def cw(off):
        return pl.BlockSpec((CONV_K, LANES), lambda b, h: (0, off + h))

    row_spec = pl.BlockSpec((1, LANES), lambda b, h: (0, 0))
    in_specs = [col(0), col(H_A), col(2 * H_A), col(3 * H_A),
                pl.BlockSpec((t, LANES), lambda b, h: (b, 0)),
                cw(0), cw(H_A), cw(2 * H_A), row_spec, row_spec, row_spec]
    args = [proj, proj, proj, proj, gp, conv_w, conv_w, conv_w, alog_row, dtb_row, gnorm]
    o_shape = jax.ShapeDtypeStruct((nseq * t, H_A * DV_A), BF16)
    o_spec = pl.BlockSpec((t, LANES), lambda b, h: (b, h))
    if has_state:
        in_specs.append(pl.BlockSpec((None, None, 2, None, DK_A, DV_A),
                                     lambda b, h: (b, layer_e, 0, h, 0, 0)))
        args.append(state)
        out_shape, out_specs = o_shape, o_spec
    else:
        out_shape = (o_shape, jax.ShapeDtypeStruct((nseq, 2, H_A, DK_A, DV_A), F32))
        out_specs = (o_spec, pl.BlockSpec((None, 2, None, DK_A, DV_A), lambda b, h: (b, 0, h, 0, 0)))
    scratch = [pltpu.VMEM((t, LANES), F32)] * 3
    scratch += [pltpu.VMEM((2, t, LANES), F32)] * 2
    scratch += [pltpu.VMEM((2, t, LANES), F32),
                pltpu.VMEM((2, 2 * t, LANES), F32),
                pltpu.VMEM((2, t, CHUNK_A), F32),
                pltpu.VMEM((2, t, LANES), F32),
                pltpu.VMEM((2, nc * 8, LANES), F32),
                pltpu.VMEM((t, LANES), F32),
                pltpu.VMEM((2, DK_A, DV_A), F32)]
    return pl.pallas_call(
        functools.partial(_gdn_kernel, seq_len=t, has_state=has_state),
        grid=(nseq, H_A),
        in_specs=in_specs,
        out_specs=out_specs,
        out_shape=out_shape,
        scratch_shapes=scratch,
        compiler_params=_params("parallel", "parallel"),
        name="gdn",
    )(*args)


def _rope(x, cos, sin):
    lane = lax.broadcasted_iota(jnp.int32, x.shape, 1)
    quarter = DQK_B // 4
    rot = jnp.where((lane & quarter) == 0,
                    -pltpu.roll(x, LANES - quarter, axis=1),
                    pltpu.roll(x, quarter, axis=1))
    return x * cos + rot * sin


def _attn_kernel(*refs, seq_len, lam_init, has_ctx):
    t = seq_len
    q_ref, k_ref, v_ref, z_ref, lam_ref, dg_ref = refs[:6]
    if has_ctx:
        cq_ref, sq_ref, ck_ref, sk_ref, kc_ref, vc_ref, o_ref, k_all, v_all = refs[6:]
    else:
        o_ref = refs[6]
    q = q_ref[...]
    if has_ctx:
        @pl.when(pl.program_id(2) == 0)
        def _():
            k_all[0:t, :] = _rope(k_ref[...], ck_ref[...], sk_ref[...])
            k_all[t:, :] = kc_ref[...]
            v_all[0:t, :] = v_ref[...]
            v_all[t:, :] = vc_ref[...]
        q = _rope(q, cq_ref[...], sq_ref[...])
        k = k_all[...]
        v = v_all[...]
    else:
        k = k_ref[...]
        v = v_ref[...]
    lp = lam_ref[...]
    lam = (jnp.exp(jnp.sum(lp[0:1] * lp[1:2], axis=-1, keepdims=True))
           - jnp.exp(jnp.sum(lp[2:3] * lp[3:4], axis=-1, keepdims=True)) + lam_init)
    lane = lax.broadcasted_iota(jnp.int32, q.shape, 1)

    def probs(qm):
        s = _mm_nt(qm, k) * DQK_B ** -0.5
        e = jnp.exp(s - jnp.max(s, axis=-1, keepdims=True))
        return e / jnp.sum(e, axis=-1, keepdims=True)

    w = probs(jnp.where(lane < DQK_B, q, 0.0)) - lam * probs(jnp.where(lane < DQK_B, 0.0, q))
    o = _mm(w, v)
    o_ref[...] = (_rms(o) * dg_ref[...] * (1.0 - lam_init) * _silu(z_ref[...])).astype(BF16)


def _attn(proj, lam_p, dgain, lam_init, ctx, layer_e, nseq, seq_len):
    t = seq_len
    nq = t // Q_TILE
    has_ctx = ctx is not None
    off = 4 * H_A
    in_specs = [pl.BlockSpec((Q_TILE, LANES), lambda b, h, i: (b * nq + i, off + h)),
                pl.BlockSpec((t, LANES), lambda b, h, i: (b, off + H_B + h)),
                pl.BlockSpec((t, LANES), lambda b, h, i: (b, off + 2 * H_B + h)),
                pl.BlockSpec((Q_TILE, LANES), lambda b, h, i: (b * nq + i, off + 3 * H_B + h)),
                pl.BlockSpec((4, DQK_B), lambda b, h, i: (0, 0)),
                pl.BlockSpec((1, LANES), lambda b, h, i: (0, 0))]
    args = [proj, proj, proj, proj, lam_p, dgain]
    scratch = []
    if has_ctx:
        cos, sin, cache_k, cache_v = ctx
        past = cache_k.shape[2]
        in_specs += [pl.BlockSpec((Q_TILE, LANES), lambda b, h, i: (i, 0)),
                     pl.BlockSpec((Q_TILE, LANES), lambda b, h, i: (i, 0)),
                     pl.BlockSpec((t, LANES), lambda b, h, i: (0, 0)),
                     pl.BlockSpec((t, LANES), lambda b, h, i: (0, 0)),
                     pl.BlockSpec((None, None, past, LANES), lambda b, h, i: (b, layer_e, 0, h)),
                     pl.BlockSpec((None, None, past, LANES), lambda b, h, i: (b, layer_e, 0, h))]
        args += [cos, sin, cos, sin, cache_k.reshape(cache_k.shape[:3] + (-1,)),
                 cache_v.reshape(cache_v.shape[:3] + (-1,))]
        scratch = [pltpu.VMEM((t + past, LANES), F32)] * 2
    return pl.pallas_call(
        functools.partial(_attn_kernel, seq_len=t, lam_init=lam_init, has_ctx=has_ctx),
        grid=(nseq, H_B, nq),
        in_specs=in_specs,
        out_specs=pl.BlockSpec((Q_TILE, LANES), lambda b, h, i: (b * nq + i, h)),
        out_shape=jax.ShapeDtypeStruct((nseq * t, H_B * DV_B), BF16),
        scratch_shapes=scratch,
        compiler_params=_params("parallel", "parallel", "arbitrary"),
        name="diffattn",
    )(*args)


def _block_row_bcast(x, blk, r):
    n, w = x.shape
    if blk >= 8:
        x3 = x.reshape(n // blk, blk, w)
        return jnp.broadcast_to(x3[:, r:r + 1, :], x3.shape).reshape(n, w)
    x3 = x.reshape(n // 8, 8, w)
    sub = lax.broadcasted_iota(jnp.int32, x3.shape, 1)
    out = None
    for b in range(8 // blk):
        cand = jnp.broadcast_to(x3[:, b * blk + r:b * blk + r + 1, :], x3.shape)
        out = cand if out is None else jnp.where(sub >= b * blk, cand, out)
    return out.reshape(n, w)


def _gla_kernel(*refs, seq_len, has_state):
    t = seq_len
    c = CHUNK_C
    nc = t // c
    q_ref, k_ref, v_ref, z_ref, glr_ref, wg_ref, bg_ref, gg_ref = refs[:8]
    if has_state:
        s0_ref, o_ref = refs[8:10]
    else:
        o_ref, sf_ref = refs[8:10]
    la_s, oacc, s_s = refs[10:]

    glr = glr_ref[...]
    for d in range(2):
        gate = _mm(glr, wg_ref[d]) + bg_ref[d]
        la_s[d] = (jnp.minimum(gate, 0.0) - jnp.log1p(jnp.exp(-jnp.abs(gate)))) * (1.0 / GLA_NORMALIZER)

    row = lax.broadcasted_iota(jnp.int32, (c, c), 0)
    col = lax.broadcasted_iota(jnp.int32, (c, c), 1)
    row_k = lax.broadcasted_iota(jnp.int32, (c, DK_C), 0)
    ones = jnp.ones((c, LANES), F32)

    def chunk(ci, d):
        rev = d == 1
        rows = pl.ds(pl.multiple_of(ci * c, c), c)
        q = q_ref[rows, :] * DK_C ** -0.5
        k = k_ref[rows, :]
        v = v_ref[rows, :]
        la = la_s[d, rows, :]
        lmat = ((row <= col) if rev else (row >= col)).astype(F32)
        bc = _mm_hi(lmat, la)
        attn = jnp.where(row == col, _mm_nt(q, k), 0.0)
        s = c // 2
        while s >= 1:
            blk = 2 * s
            bref = _block_row_bcast(bc, blk, s if rev else s - 1)
            pos = row_k & (blk - 1)
            is_q = (pos < s) if rev else (pos >= s)
            qs = jnp.where(is_q, q * jnp.exp(jnp.minimum(bc - bref, 0.0)), 0.0)
            ks = jnp.where(is_q, 0.0, k * jnp.exp(jnp.minimum(bref - bc, 0.0)))
            attn = attn + jnp.where((row ^ col) < blk, _mm_nt(qs, ks), 0.0)
            s //= 2
        b_end = bc[0:1, :] if rev else bc[c - 1:c, :]
        state = s_s[d]
        o = _mm(attn, v) + _mm(q * jnp.exp(bc), state)
        kd = k * jnp.exp(b_end - bc)
        gl_col = jnp.exp(_mm_tn_hi(la, ones))
        s_s[d] = state * jnp.concatenate([gl_col] * (DV_C // LANES), axis=1) + _mm_tn(kd, v)
        oacc[rows, :] += o

    for d in range(2):
        if has_state:
            s_s[d] = s0_ref[d]
        else:
            s_s[d] = jnp.zeros((DK_C, DV_C), F32)
    oacc[...] = jnp.zeros((t, DV_C), F32)

    def body(i, carry):
        chunk(i, 0)
        chunk(nc - 1 - i, 1)
        return carry

    lax.fori_loop(0, nc, body, 0)

    o_ref[...] = (_rms(oacc[...]) * gg_ref[...] * _silu(z_ref[...])).astype(BF16)
    if not has_state:
        sf_ref[0] = s_s[0]
        sf_ref[1] = s_s[1]


def _gla(proj, glr, wg_pad, bgate, ggain, state, layer_o, nseq, seq_len):
    t = seq_len
    has_state = state is not None
    in_specs = [pl.BlockSpec((t, DK_C), lambda b, h: (b, h)),
                pl.BlockSpec((t, DK_C), lambda b, h: (b, H_C + h)),
                pl.BlockSpec((t, DV_C), lambda b, h: (b, H_C + h)),
                pl.BlockSpec((t, DV_C), lambda b, h: (b, 2 * H_C + h)),
                pl.BlockSpec((t, LANES), lambda b, h: (b, 0)),
                pl.BlockSpec((2, LANES, DK_C), lambda b, h: (0, 0, h)),
                pl.BlockSpec((2, 1, DK_C), lambda b, h: (0, 0, h)),
                pl.BlockSpec((1, DV_C), lambda b, h: (0, 0))]
    args = [proj, proj, proj, proj, glr, wg_pad, bgate, ggain]
    o_shape = jax.ShapeDtypeStruct((nseq * t, H_C * DV_C), BF16)
    o_spec = pl.BlockSpec((t, DV_C), lambda b, h: (b, h))
    if has_state:
        in_specs.append(pl.BlockSpec((None, None, 2, None, DK_C, DV_C),
                                     lambda b, h: (b, layer_o, 0, h, 0, 0)))
        args.append(state)
        out_shape, out_specs = o_shape, o_spec
    else:
        out_shape = (o_shape, jax.ShapeDtypeStruct((nseq, 2, H_C, DK_C, DV_C), F32))
        out_specs = (o_spec, pl.BlockSpec((None, 2, None, DK_C, DV_C), lambda b, h: (b, 0, h, 0, 0)))
    scratch = [pltpu.VMEM((2, t, DK_C), F32),
               pltpu.VMEM((t, DV_C), F32),
               pltpu.VMEM((2, DK_C, DV_C), F32)]
    return pl.pallas_call(
        functools.partial(_gla_kernel, seq_len=t, has_state=has_state),
        grid=(nseq, H_C),
        in_specs=in_specs,
        out_specs=out_specs,
        out_shape=out_shape,
        scratch_shapes=scratch,
        compiler_params=_params("parallel", "parallel"),
        name="gla",
    )(*args)


def _rope_tables(n_tokens):
    rows = n_tokens // GRID_W
    row = jnp.repeat(jnp.arange(rows), GRID_W).astype(F32)
    col = jnp.tile(jnp.arange(GRID_W), rows).astype(F32)
    quarter = DQK_B // 4
    freqs = ROPE_BASE ** (-jnp.arange(quarter, dtype=F32) / quarter)
    ar = row[:, None] * freqs
    ac = col[:, None] * freqs
    ang = jnp.concatenate([ar, ar, ac, ac] * (LANES // DQK_B), axis=-1)
    return jnp.cos(ang), jnp.sin(ang)


def _pad_lanes(x):
    return jnp.pad(x, ((0, 0), (0, LANES - x.shape[1])))


def kernel(x_prompt, x_sample, c, state_gdn, cache_k, cache_v, state_gla, c_ctx, w_ada, b_ada, norm_pre, norm_post, w_in_even, conv_even, a_log_even, dt_bias_even, gdn_norm_even, lam_even, diff_norm_even, w_out_even, w_in_odd, w_gate_odd, b_gate_odd, gla_norm_odd, w_out_odd):
    n_p, t_p, _ = x_prompt.shape
    n_s, t_s, _ = x_sample.shape
    wa = H_A * DK_A

    cvec = jnp.zeros((8, D_MODEL), F32).at[0].set(c_ctx).at[1:1 + n_s].set(c)
    mods = _adaln(cvec, w_ada, b_ada).reshape(DEPTH * 8, 1, 3 * D_MODEL)
    cos, sin = _rope_tables(t_s)

    y_p = x_prompt.reshape(n_p * t_p, D_MODEL)
    y_s = x_sample.reshape(n_s * t_s, D_MODEL)
    gdn_states, ctx_keys, ctx_vals, gla_states = [], [], [], []

    for layer in range(DEPTH):
        gidx_p = _group_index(layer, 0, None)
        gidx_s = _group_index(layer, 1, t_s // ROW_TILE)
        h_p = _prenorm(y_p, norm_pre[layer], mods, gidx_p)
        h_s = _prenorm(y_s, norm_pre[layer], mods, gidx_s)
        if layer % 2 == 0:
            e = layer // 2
            lam_init = 0.8 - 0.6 * math.exp(-0.3 * layer)
            w = w_in_even[e]
            n_gate = 4 * H_A
            w_main = jnp.concatenate([w[:, :4 * wa], w[:, 4 * wa + n_gate:]], axis=1).astype(BF16)
            w_gate = _pad_lanes(w[:, 4 * wa:4 * wa + n_gate]).astype(BF16)
            w_out = w_out_even[e].astype(BF16)
            zeros16 = jnp.zeros((2 * H_A,), F32)
            alog_row = _pad_lanes(jnp.concatenate([zeros16, a_log_even[e].reshape(-1)])[None, :])
            dtb_row = _pad_lanes(jnp.concatenate([zeros16, dt_bias_even[e].reshape(-1)])[None, :])
            gnorm = gdn_norm_even[e][None, :]
            dgain = diff_norm_even[e][None, :]

            proj_p = _matmul(h_p, w_main, 512, 1024)
            gp_p = _matmul(h_p, w_gate, 512, LANES)
            oa_p, s_new = _gdn(proj_p, gp_p, conv_even[e], alog_row, dtb_row, gnorm, None, e, n_p, t_p)
            ob_p = _attn(proj_p, lam_even[e], dgain, lam_init, None, e, n_p, t_p)
            gdn_states.append(s_new)
            ctx_keys.append(proj_p[:, 5 * wa:6 * wa].reshape(n_p, t_p, H_B, 2 * DQK_B))
            ctx_vals.append(proj_p[:, 6 * wa:7 * wa].reshape(n_p, t_p, H_B, DV_B))
            y_p = _outproj([oa_p, ob_p], w_out, y_p, norm_post[layer], mods, gidx_p)

            proj_s = _matmul(h_s, w_main, 512, 1024)
            gp_s = _matmul(h_s, w_gate, 512, LANES)
            oa_s = _gdn(proj_s, gp_s, conv_even[e], alog_row, dtb_row, gnorm, state_gdn, e, n_s, t_s)
            ob_s = _attn(proj_s, lam_even[e], dgain, lam_init, (cos, sin, cache_k, cache_v), e, n_s, t_s)
            y_s = _outproj([oa_s, ob_s], w_out, y_s, norm_post[layer], mods, gidx_s)
        else:
            od = layer // 2
            w = w_in_odd[od]
            n_main = 2 * H_C * DK_C + 2 * H_C * DV_C
            w_main = w[:, :n_main].astype(BF16)
            w_glr = _pad_lanes(w[:, n_main:]).astype(BF16)
            w_out = w_out_odd[od].astype(BF16)
            wg_pad = jnp.zeros((2, LANES, H_C * DK_C), F32)
            for d in range(2):
                wg_pad = wg_pad.at[d, d * GATE_RANK:(d + 1) * GATE_RANK].set(w_gate_odd[od, d])
            bgate = b_gate_odd[od].reshape(2, 1, H_C * DK_C)
            ggain = gla_norm_odd[od][None, :]

            proj_p = _matmul(h_p, w_main, 512, 1024)
            glr_p = _matmul(h_p, w_glr, 512, LANES)
            o_p, s_new = _gla(proj_p, glr_p, wg_pad, bgate, ggain, None, od, n_p, t_p)
            gla_states.append(s_new)
            y_p = _outproj([o_p], w_out, y_p, norm_post[layer], mods, gidx_p)

            proj_s = _matmul(h_s, w_main, 512, 1024)
            glr_s = _matmul(h_s, w_glr, 512, LANES)
            o_s = _gla(proj_s, glr_s, wg_pad, bgate, ggain, state_gla, od, n_s, t_s)
            y_s = _outproj([o_s], w_out, y_s, norm_post[layer], mods, gidx_s)

    return (y_p.reshape(n_p, t_p, D_MODEL),
            y_s.reshape(n_s, t_s, D_MODEL),
            jnp.stack(gdn_states, axis=1),
            jnp.stack(ctx_keys, axis=1),
            jnp.stack(ctx_vals, axis=1),
            jnp.stack(gla_states, axis=1))
```

```python
import functools
import math

import jax
import jax.numpy as jnp
from jax import lax
from jax.experimental import pallas as pl
from jax.experimental.pallas import tpu as pltpu

F32 = jnp.float32
BF16 = jnp.bfloat16
HIGHEST = lax.Precision.HIGHEST

D_MODEL = 2048
DEPTH = 4
EPS = 1e-6
GRID_W = 64
H_A = 8
DK_A = 128
DV_A = 128
CONV_K = 5
CHUNK_A = 64
PREP_UNROLL = 2
GDN_MAX_HEADS = 4
H_B = 8
DQK_B = 64
DV_B = 128
ROPE_BASE = 10000.0
Q_TILE = 256
Q_TILE_CTX = 256
ATTN_HEADS = 4
ATTN_HEADS_CTX = 2
H_C = 4
DK_C = 256
DV_C = 512
GATE_RANK = 16
GLA_NORMALIZER = 16.0
CHUNK_C = 64

LANES = 128
VMEM_LIMIT = 48 * 1024 * 1024
ROW_TILE = 256


def _params(*semantics):
    return pltpu.CompilerParams(dimension_semantics=semantics,
                                vmem_limit_bytes=VMEM_LIMIT)


def _mm(a, b):
    return jnp.dot(a.astype(BF16), b.astype(BF16), preferred_element_type=F32)


def _mm_nt(a, b):
    return lax.dot_general(a.astype(BF16), b.astype(BF16),
                           (((1,), (1,)), ((), ())), preferred_element_type=F32)


def _mm_tn(a, b):
    return lax.dot_general(a.astype(BF16), b.astype(BF16),
                           (((0,), (0,)), ((), ())), preferred_element_type=F32)


def _mm_hi(a, b):
    return jnp.dot(a, b, precision=HIGHEST, preferred_element_type=F32)


def _mm_tn_hi(a, b):
    return lax.dot_general(a, b, (((0,), (0,)), ((), ())), precision=HIGHEST,
                           preferred_element_type=F32)


def _aligned(start, size, align):
    if isinstance(start, int):
        return pl.ds(start, size)
    return pl.ds(pl.multiple_of(start, align), size)


def _silu(x):
    return x * jax.nn.sigmoid(x)


def _softplus(x):
    return jnp.maximum(x, 0.0) + jnp.log1p(jnp.exp(-jnp.abs(x)))


def _rms(x):
    return x * lax.rsqrt(jnp.mean(x * x, axis=-1, keepdims=True) + EPS)


def _adaln_kernel(c_ref, w_ref, b_ref, o_ref):
    o_ref[...] = _mm(_silu(c_ref[...]), w_ref[...]) + b_ref[...]


def _adaln(cvec, w_ada, b_ada):
    tn = 1024
    n3 = 3 * D_MODEL
    return pl.pallas_call(
        _adaln_kernel,
        grid=(DEPTH, n3 // tn),
        in_specs=[pl.BlockSpec((8, D_MODEL), lambda l, j: (0, 0)),
                  pl.BlockSpec((None, D_MODEL, tn), lambda l, j: (l, 0, j)),
                  pl.BlockSpec((None, 1, tn), lambda l, j: (l, 0, j))],
        out_specs=pl.BlockSpec((None, 8, tn), lambda l, j: (l, 0, j)),
        out_shape=jax.ShapeDtypeStruct((DEPTH, 8, n3), F32),
        compiler_params=_params("parallel", "parallel"),
        name="adaln",
    )(cvec, w_ada, b_ada.reshape(DEPTH, 1, n3))


def _prenorm_kernel(x_ref, g_ref, m_ref, h_ref):
    y = _rms(x_ref[...]) * g_ref[...]
    m = m_ref[...]
    h_ref[...] = (y * (1.0 + m[:, D_MODEL:2 * D_MODEL]) + m[:, :D_MODEL]).astype(BF16)


def _group_index(layer, group0, tiles_per_group):
    if tiles_per_group is None:
        return lambda i: (layer * 8 + group0, 0, 0)
    return lambda i: (layer * 8 + group0 + i // tiles_per_group, 0, 0)


def _prenorm(y, gain, mods, gidx):
    m = y.shape[0]
    return pl.pallas_call(
        _prenorm_kernel,
        grid=(m // ROW_TILE,),
        in_specs=[pl.BlockSpec((ROW_TILE, D_MODEL), lambda i: (i, 0)),
                  pl.BlockSpec((1, D_MODEL), lambda i: (0, 0)),
                  pl.BlockSpec((None, 1, 3 * D_MODEL), gidx)],
        out_specs=pl.BlockSpec((ROW_TILE, D_MODEL), lambda i: (i, 0)),
        out_shape=jax.ShapeDtypeStruct((m, D_MODEL), BF16),
        compiler_params=_params("parallel"),
        name="prenorm",
    )(y, gain.reshape(1, D_MODEL), mods)


def _inproj_kernel(*refs, shift):
    x_ref, wa_ref = refs[:2]
    wb_ref = refs[2] if shift else None
    o_ref, wbf = refs[-2:]

    @pl.when(pl.program_id(1) == 0)
    def _():
        w = wa_ref[...]
        if shift:
            w = jnp.concatenate([w[:, shift:], wb_ref[:, :shift]], axis=1)
        wbf[...] = w.astype(BF16)

    o_ref[...] = jnp.dot(x_ref[...], wbf[...], preferred_element_type=F32)


def _inproj(x, w3, layer_idx, col0, n_out, tm, tn, shift=0, cache_slot=None):
    m, k = x.shape
    in_specs = [pl.BlockSpec((tm, k), lambda j, i: (i, 0)),
                pl.BlockSpec((None, k, tn), lambda j, i: (layer_idx, 0, col0 // tn + j))]
    args = [x, w3]
    if shift:
        in_specs.append(pl.BlockSpec((None, k, LANES),
                                     lambda j, i: (layer_idx, 0, (col0 + (j + 1) * tn) // LANES)))
        args.append(w3)
    aliases = {}
    if cache_slot is None:
        out_spec = pl.BlockSpec((tm, tn), lambda j, i: (i, j))
        out_shape = jax.ShapeDtypeStruct((m, n_out), F32)
    else:
        prev, slot, t = cache_slot
        out_spec = pl.BlockSpec((None, None, t, tn), lambda j, i: (i, slot, 0, 0))
        out_shape = jax.ShapeDtypeStruct((m // t, w3.shape[0], t, n_out), F32)
        if prev is not None:
            in_specs.append(pl.BlockSpec(memory_space=pl.ANY))
            args.append(prev)
            aliases = {len(args) - 1: 0}

    return pl.pallas_call(
        functools.partial(_inproj_kernel, shift=shift),
        grid=(n_out // tn, m // tm),
        in_specs=in_specs,
        out_specs=out_spec,
        out_shape=out_shape,
        scratch_shapes=[pltpu.VMEM((k, tn), BF16)],
        input_output_aliases=aliases,
        compiler_params=_params("parallel", "arbitrary"),
        name="inproj",
    )(*args)


def _outproj_kernel(*refs, n_in, emit_h):
    x_refs = refs[:n_in]
    w_ref, y_ref, g_ref, m_ref = refs[n_in:n_in + 4]
    acc = None
    off = 0
    for x_ref in x_refs:
        kk = x_ref.shape[1]
        part = jnp.dot(x_ref[...], w_ref[off:off + kk, :], preferred_element_type=F32)
        acc = part if acc is None else acc + part
        off += kk
    m = m_ref[...]
    y = y_ref[...] + m[:, 2 * D_MODEL:] * (_rms(acc) * g_ref[...])
    if emit_h:
        gn_ref, mn_ref, o_ref, h_ref = refs[n_in + 4:]
        mn = mn_ref[...]
        h_ref[...] = (_rms(y) * gn_ref[...] * (1.0 + mn[:, D_MODEL:2 * D_MODEL]) + mn[:, :D_MODEL]).astype(BF16)
    else:
        o_ref = refs[n_in + 4]
    o_ref[...] = y


def _outproj(xs, w, y, gain, mods, gidx, next_gain=None, gidx_next=None):
    m = y.shape[0]
    emit_h = next_gain is not None
    row_spec = pl.BlockSpec((ROW_TILE, D_MODEL), lambda i: (i, 0))
    vec_spec = pl.BlockSpec((1, D_MODEL), lambda i: (0, 0))
    in_specs = [pl.BlockSpec((ROW_TILE, x.shape[1]), lambda i: (i, 0)) for x in xs]
    in_specs += [pl.BlockSpec(w.shape, lambda i: (0, 0)), row_spec, vec_spec,
                 pl.BlockSpec((None, 1, 3 * D_MODEL), gidx)]
    args = [*xs, w, y, gain.reshape(1, D_MODEL), mods]
    out_specs = row_spec
    out_shape = jax.ShapeDtypeStruct((m, D_MODEL), F32)
    if emit_h:
        in_specs += [vec_spec, pl.BlockSpec((None, 1, 3 * D_MODEL), gidx_next)]
        args += [next_gain.reshape(1, D_MODEL), mods]
        out_specs = (row_spec, row_spec)
        out_shape = (out_shape, jax.ShapeDtypeStruct((m, D_MODEL), BF16))
    return pl.pallas_call(
        functools.partial(_outproj_kernel, n_in=len(xs), emit_h=emit_h),
        grid=(m // ROW_TILE,),
        in_specs=in_specs,
        out_specs=out_specs,
        out_shape=out_shape,
        compiler_params=_params("parallel"),
        name="outproj",
    )(*args)


def _gdn_kernel(*refs, seq_len, heads, has_state):
    t = seq_len
    c = CHUNK_A
    nc = t // c
    (q_ref, k_ref, v_ref, z_ref, gp_ref, cwq_ref, cwk_ref, cwv_ref,
     alog_ref, dtb_ref, gn_ref) = refs[:11]
    qn, kn, vn, bt, gt, wv_s, wkqd_s, qk_s, kd_s, gl_s, oacc, s_s = refs[-12:]
    if has_state:
        s0_ref, o_ref = refs[11], refs[-13]
    else:
        o_ref, sf_ref = refs[-14:-12]
    head0 = pl.program_id(1) * heads

    def lanes_of(hd):
        return slice(hd * LANES, (hd + 1) * LANES)

    row_t = lax.broadcasted_iota(jnp.int32, (t, LANES), 0)

    def conv_silu(x_ref, w_ref, hd):
        x = x_ref[:, lanes_of(hd)]
        w = w_ref[:, lanes_of(hd)]
        acc = x * w[CONV_K // 2:CONV_K // 2 + 1, :]
        for tap in range(CONV_K):
            d = tap - CONV_K // 2
            if d == 0:
                continue
            shifted = pltpu.roll(x, (-d) % t, axis=0)
            valid = (row_t >= -d) if d < 0 else (row_t < t - d)
            acc = acc + jnp.where(valid, shifted, 0.0) * w[tap:tap + 1, :]
        return _silu(acc)

    def l2n(x):
        return x * lax.rsqrt(jnp.sum(x * x, axis=-1, keepdims=True) + EPS)

    for hd in range(heads):
        qn[hd] = l2n(conv_silu(q_ref, cwq_ref, hd))
        kn[hd] = l2n(conv_silu(k_ref, cwk_ref, hd))
        vn[hd] = conv_silu(v_ref, cwv_ref, hd)

    gp = gp_ref[...]
    lane_t = lax.broadcasted_iota(jnp.int32, (t, LANES), 1)
    beta_all = jax.nn.sigmoid(gp)
    g_all = -jnp.exp(alog_ref[...]) * _softplus(gp + dtb_ref[...])

    def pick(x, idx):
        col = jnp.sum(jnp.where(lane_t == idx, x, 0.0), axis=-1, keepdims=True)
        return jnp.broadcast_to(col, (t, LANES))

    for hd in range(heads):
        for d in range(2):
            bt[hd * 2 + d] = pick(beta_all, d * H_A + head0 + hd)
            gt[hd * 2 + d] = pick(g_all, 2 * H_A + d * H_A + head0 + hd)

    r2 = 2 * c
    row = lax.broadcasted_iota(jnp.int32, (r2, r2), 0)
    col = lax.broadcasted_iota(jnp.int32, (r2, r2), 1)
    rc_xor = row ^ col
    pos = row & (c - 1)

    tri_inc, tri_str = [], []
    for rev in (False, True):
        after = (col - row) if rev else (row - col)
        after = jnp.where(rc_xor < c, after, -1)
        tri_inc.append(after >= 0)
        tri_str.append(after > 0)
    eye = jnp.where(row == col, 1.0, 0.0)

    def prep(chains):
        n = range(len(chains))
        rows = [_aligned(gi * r2, r2, r2) for _, gi, _ in chains]
        dirs = [d for _, _, d in chains]
        hdir = [hd * 2 + d for hd, _, d in chains]
        q = [qn[chains[i][0], rows[i], :] * DK_A ** -0.5 for i in n]
        k = [kn[chains[i][0], rows[i], :] for i in n]
        beta = [bt[hdir[i], rows[i], :] for i in n]
        kb = [k[i] * beta[i] for i in n]
        kk = [_mm_nt(kb[i], k[i]) for i in n]
        qk = [_mm_nt(q[i], k[i]) for i in n]
        gcum = [gt[hdir[i], rows[i], :] for i in n]
        sh = 1
        while sh < c:
            for i in n:
                if dirs[i] == 1:
                    gcum[i] = gcum[i] + jnp.where(pos < c - sh, pltpu.roll(gcum[i], r2 - sh, axis=0), 0.0)
                else:
                    gcum[i] = gcum[i] + jnp.where(pos >= sh, pltpu.roll(gcum[i], sh, axis=0), 0.0)
            sh *= 2
        decay = [jnp.where(tri_inc[dirs[i]], jnp.exp(jnp.minimum(gcum[i] - gcum[i].T, 0.0)), 0.0) for i in n]
        amat = [jnp.where(tri_str[dirs[i]], kk[i] * decay[i], 0.0) for i in n]
        tinv = [eye - jnp.where(rc_xor < 2, amat[i], 0.0) for i in n]
        b = 2
        while b < c:
            level = (rc_xor >= b) & (rc_xor < 2 * b)
            m1 = [_mm(jnp.where(level, amat[i], 0.0), tinv[i]) for i in n]
            m2 = [_mm(tinv[i], m1[i]) for i in n]
            tinv = [tinv[i] - m2[i] for i in n]
            b *= 2
        eg = [jnp.exp(gcum[i]) for i in n]
        x = [_mm(tinv[i], jnp.concatenate([vn[chains[i][0], rows[i], :] * beta[i], kb[i] * eg[i]], axis=1))
             for i in n]
        for i in n:
            _, gi, d = chains[i]
            hd2 = hdir[i]
            qkd = qk[i] * decay[i]
            g_end = _block_row_bcast(gcum[i], c, 0 if d == 1 else c - 1)
            qd = q[i] * eg[i]
            gl = jnp.exp(g_end)
            wv_s[hd2, rows[i], :] = x[i][:, :DV_A]
            qk_s[hd2, rows[i], :] = jnp.where(row < c, qkd, pltpu.roll(qkd, c, axis=1))[:, :c]
            kd_s[hd2, rows[i], :] = k[i] * jnp.exp(g_end - gcum[i])
            for j in range(2):
                base = gi * 2 * r2 + j * r2
                wkqd_s[hd2, _aligned(base, c, c), :] = x[i][j * c:(j + 1) * c, DV_A:]
                wkqd_s[hd2, _aligned(base + c, c, c), :] = qd[j * c:(j + 1) * c, :]
                gl_s[hd2, _aligned(gi * 16 + j * 8, 8, 8), :] = gl[j * c:j * c + 8, :]

    def rec(chains):
        n = range(len(chains))
        rows = [_aligned(ci * c, c, c) for _, ci, _ in chains]
        hdir = [hd * 2 + d for hd, _, d in chains]
        s = [s_s[hdir[i]] for i in n]
        ws = [_mm(wkqd_s[hdir[i], _aligned(chains[i][1] * 2 * c, 2 * c, 2 * c), :], s[i]) for i in n]
        u = [wv_s[hdir[i], rows[i], :] - ws[i][:c] for i in n]
        o2 = [_mm(qk_s[hdir[i], rows[i], :], u[i]) for i in n]
        ds = [_mm_tn(kd_s[hdir[i], rows[i], :], u[i]) for i in n]
        for i in n:
            s_s[hdir[i]] = s[i] * gl_s[hdir[i], _aligned(chains[i][1] * 8, 1, 8), :] + ds[i]
        for i in n:
            oacc[chains[i][0], rows[i], :] += ws[i][c:] + o2[i]

    for hd in range(heads):
        for d in range(2):
            if has_state:
                s_s[hd * 2 + d] = s0_ref[d, hd]
            else:
                s_s[hd * 2 + d] = jnp.zeros((DK_A, DV_A), F32)
    oacc[...] = jnp.zeros((heads, t, DV_A), F32)

    def prep_body(gi, carry):
        prep([(hd, gi * PREP_UNROLL + j, d)
              for hd in range(heads) for j in range(PREP_UNROLL) for d in range(2)])
        return carry

    if nc // 2 == PREP_UNROLL:
        prep_body(0, 0)
    else:
        lax.fori_loop(0, nc // 2 // PREP_UNROLL, prep_body, 0)

    def rec_body(i, carry):
        rec([(hd, i if d == 0 else nc - 1 - i, d) for hd in range(heads) for d in range(2)])
        return carry

    lax.fori_loop(0, nc, rec_body, 0)

    for hd in range(heads):
        o_ref[:, lanes_of(hd)] = (_rms(oacc[hd]) * gn_ref[...] * _silu(z_ref[:, lanes_of(hd)])).astype(BF16)
        if not has_state:
            sf_ref[0, hd] = s_s[hd * 2]
            sf_ref[1, hd] = s_s[hd * 2 + 1]


def _gdn_heads(t):
    for g in (GDN_MAX_HEADS, 2, 1):
        scratch = g * t * LANES * 4 * 17 + 2 * g * DK_A * DV_A * 4
        blocks = 2 * (4 * t * g * LANES * 4 + t * LANES * 4 + t * g * LANES * 2)
        if scratch + blocks <= VMEM_LIMIT - (8 << 20):
            return g
    return 1


def _gdn(proj, gp, conv_w, alog_row, dtb_row, gnorm, state, state_out, layer_e, nseq, seq_len):
    t = seq_len
    nc = t // CHUNK_A
    has_state = state is not None
    aliases = {}

    g = _gdn_heads(t)
    w = g * LANES
    nb = H_A // g

    def col(seg):
        return pl.BlockSpec((t, w), lambda b, h: (b, seg * nb + h))

    def cw(seg):
        return pl.BlockSpec((CONV_K, w), lambda b, h: (0, seg * nb + h))

    row_spec = pl.BlockSpec((1, LANES), lambda b, h: (0, 0))
    in_specs = [col(0), col(1), col(2), col(3),
                pl.BlockSpec((t, LANES), lambda b, h: (b, 0)),
                cw(0), cw(1), cw(2), row_spec, row_spec, row_spec]
    args = [proj, proj, proj, proj, gp, conv_w, conv_w, conv_w, alog_row, dtb_row, gnorm]
    o_shape = jax.ShapeDtypeStruct((nseq * t, H_A * DV_A), BF16)
    o_spec = pl.BlockSpec((t, w), lambda b, h: (b, h))
    if has_state:
        in_specs.append(pl.BlockSpec((None, None, 2, g, DK_A, DV_A),
                                     lambda b, h: (b, layer_e, 0, h, 0, 0)))
        args.append(state)
        out_shape, out_specs = o_shape, o_spec
    else:
        prev, n_slots = state_out
        out_shape = (o_shape, jax.ShapeDtypeStruct((nseq, n_slots, 2, H_A, DK_A, DV_A), F32))
        out_specs = (o_spec, pl.BlockSpec((None, None, 2, g, DK_A, DV_A),
                                          lambda b, h: (b, layer_e, 0, h, 0, 0)))
        if prev is not None:
            in_specs.append(pl.BlockSpec(memory_space=pl.ANY))
            args.append(prev)
            aliases = {len(args) - 1: 1}
    scratch = [pltpu.VMEM((g, t, LANES), F32)] * 3
    scratch += [pltpu.VMEM((2 * g, t, LANES), F32)] * 2
    scratch += [pltpu.VMEM((2 * g, t, LANES), F32),
                pltpu.VMEM((2 * g, 2 * t, LANES), F32),
                pltpu.VMEM((2 * g, t, CHUNK_A), F32),
                pltpu.VMEM((2 * g, t, LANES), F32),
                pltpu.VMEM((2 * g, nc * 8, LANES), F32),
                pltpu.VMEM((g, t, LANES), F32),
                pltpu.VMEM((2 * g, DK_A, DV_A), F32)]
    return pl.pallas_call(
        functools.partial(_gdn_kernel, seq_len=t, heads=g, has_state=has_state),
        grid=(nseq, nb),
        in_specs=in_specs,
        out_specs=out_specs,
        out_shape=out_shape,
        scratch_shapes=scratch,
        input_output_aliases=aliases,
        compiler_params=_params("parallel", "parallel"),
        name="gdn",
    )(*args)


def _rope(x, cos, sin):
    lane = lax.broadcasted_iota(jnp.int32, x.shape, 1)
    quarter = DQK_B // 4
    rot = jnp.where((lane & quarter) == 0,
                    -pltpu.roll(x, LANES - quarter, axis=1),
                    pltpu.roll(x, quarter, axis=1))
    return x * cos + rot * sin


def _attn_kernel(*refs, seq_len, heads, lam_init, has_ctx):
    t = seq_len
    q_ref, k_ref, v_ref, z_ref, lam_ref, dg_ref = refs[:6]
    if has_ctx:
        cq_ref, sq_ref, ck_ref, sk_ref, kc_ref, vc_ref, o_ref, k_all, v_all = refs[6:]
    else:
        o_ref = refs[6]

    def lanes_of(hd):
        return slice(hd * LANES, (hd + 1) * LANES)

    if has_ctx:
        @pl.when(pl.program_id(2) == 0)
        def _():
            for hd in range(heads):
                k_all[hd, 0:t, :] = _rope(k_ref[:, lanes_of(hd)], ck_ref[...], sk_ref[...])
                k_all[hd, t:, :] = kc_ref[:, lanes_of(hd)]
                v_all[hd, 0:t, :] = v_ref[:, lanes_of(hd)]
                v_all[hd, t:, :] = vc_ref[:, lanes_of(hd)]
    lp = lam_ref[...]
    lam = (jnp.exp(jnp.sum(lp[0:1] * lp[1:2], axis=-1, keepdims=True))
           - jnp.exp(jnp.sum(lp[2:3] * lp[3:4], axis=-1, keepdims=True)) + lam_init)
    lane = lax.broadcasted_iota(jnp.int32, (q_ref.shape[0], LANES), 1)

    chains = [(hd, m) for hd in range(heads) for m in range(2)]
    n = range(len(chains))
    q = []
    for hd in range(heads):
        qh = q_ref[:, lanes_of(hd)]
        if has_ctx:
            qh = _rope(qh, cq_ref[...], sq_ref[...])
        q.append(qh * DQK_B ** -0.5)
    k = [k_all[hd] if has_ctx else k_ref[:, lanes_of(hd)] for hd in range(heads)]
    v = [v_all[hd] if has_ctx else v_ref[:, lanes_of(hd)] for hd in range(heads)]
    qm = [jnp.where(lane < DQK_B, q[hd], 0.0) if m == 0 else jnp.where(lane < DQK_B, 0.0, q[hd])
          for hd, m in chains]
    s = [_mm_nt(qm[i], k[chains[i][0]]) for i in n]
    e = [jnp.exp(s[i] - jnp.max(s[i], axis=-1, keepdims=True)) for i in n]
    den = [jnp.sum(e[i], axis=-1, keepdims=True) for i in n]
    ev = [_mm(e[i], v[chains[i][0]]) for i in n]
    for hd in range(heads):
        o = ev[2 * hd] / den[2 * hd] - lam * (ev[2 * hd + 1] / den[2 * hd + 1])
        o_ref[:, lanes_of(hd)] = (_rms(o) * dg_ref[...] * (1.0 - lam_init)
                                  * _silu(z_ref[:, lanes_of(hd)])).astype(BF16)


def _attn(q, k, v, z, lam_p, dgain, lam_init, ctx, layer_e, nseq, seq_len):
    t = seq_len
    has_ctx = ctx is not None
    g = ATTN_HEADS_CTX if has_ctx else ATTN_HEADS
    tq = min(t, Q_TILE_CTX if has_ctx else Q_TILE)
    nq = t // tq
    w = g * LANES
    nb = H_B // g

    def rows_spec(src, rows):
        arr, seg = src
        if arr.ndim == 4:
            assert rows == t
            return pl.BlockSpec((None, None, t, w), lambda b, h, i: (b, seg, 0, h))
        if rows == t:
            return pl.BlockSpec((t, w), lambda b, h, i: (b, seg * nb + h))
        return pl.BlockSpec((rows, w), lambda b, h, i: (b * nq + i, seg * nb + h))

    in_specs = [rows_spec(q, tq), rows_spec(k, t), rows_spec(v, t), rows_spec(z, tq),
                pl.BlockSpec((4, DQK_B), lambda b, h, i: (0, 0)),
                pl.BlockSpec((1, LANES), lambda b, h, i: (0, 0))]
    args = [q[0], k[0], v[0], z[0], lam_p, dgain]
    scratch = []
    if has_ctx:
        cos, sin, cache_k, cache_v = ctx
        past = cache_k.shape[2]
        in_specs += [pl.BlockSpec((tq, LANES), lambda b, h, i: (i, 0)),
                     pl.BlockSpec((tq, LANES), lambda b, h, i: (i, 0)),
                     pl.BlockSpec((t, LANES), lambda b, h, i: (0, 0)),
                     pl.BlockSpec((t, LANES), lambda b, h, i: (0, 0)),
                     pl.BlockSpec((None, None, past, w), lambda b, h, i: (b, layer_e, 0, h)),
                     pl.BlockSpec((None, None, past, w), lambda b, h, i: (b, layer_e, 0, h))]
        args += [cos, sin, cos, sin, cache_k.reshape(cache_k.shape[:3] + (-1,)),
                 cache_v.reshape(cache_v.shape[:3] + (-1,))]
        scratch = [pltpu.VMEM((g, t + past, LANES), F32)] * 2
    return pl.pallas_call(
        functools.partial(_attn_kernel, seq_len=t, heads=g, lam_init=lam_init, has_ctx=has_ctx),
        grid=(nseq, nb, nq),
        in_specs=in_specs,
        out_specs=pl.BlockSpec((tq, w), lambda b, h, i: (b * nq + i, h)),
        out_shape=jax.ShapeDtypeStruct((nseq * t, H_B * DV_B), BF16),
        scratch_shapes=scratch,
        compiler_params=_params("parallel", "parallel", "arbitrary"),
        name="diffattn",
    )(*args)


def _block_row_bcast(x, blk, r):
    n, w = x.shape
    if blk >= 8:
        x3 = x.reshape(n // blk, blk, w)
        return jnp.broadcast_to(x3[:, r:r + 1, :], x3.shape).reshape(n, w)
    x3 = x.reshape(n // 8, 8, w)
    sub = lax.broadcasted_iota(jnp.int32, x3.shape, 1)
    out = None
    for b in range(8 // blk):
        cand = jnp.broadcast_to(x3[:, b * blk + r:b * blk + r + 1, :], x3.shape)
        out = cand if out is None else jnp.where(sub >= b * blk, cand, out)
    return out.reshape(n, w)


def _gla_kernel(*refs, seq_len, has_state):
    t = seq_len
    c = CHUNK_C
    nc = t // c
    q_ref, k_ref, v_ref, z_ref, glr_ref, wg_ref, bg_ref, gg_ref = refs[:8]
    la_s, oacc, st_s = refs[-3:]
    if has_state:
        s0_ref, o_ref = refs[8], refs[-4]
    else:
        o_ref, sf_ref = refs[-5:-3]

    glr = glr_ref[...]
    for d in range(2):
        gate = _mm(glr, wg_ref[d]) + bg_ref[d]
        la_s[d] = (jnp.minimum(gate, 0.0) - jnp.log1p(jnp.exp(-jnp.abs(gate)))) * (1.0 / GLA_NORMALIZER)

    row = lax.broadcasted_iota(jnp.int32, (c, c), 0)
    col = lax.broadcasted_iota(jnp.int32, (c, c), 1)
    row_k = lax.broadcasted_iota(jnp.int32, (c, DK_C), 0)

    def chunks(chains):
        n = range(len(chains))
        rows = [_aligned(ci * c, c, c) for ci, _ in chains]
        rev = [d == 1 for _, d in chains]
        q = [q_ref[rows[i], :] * DK_C ** -0.5 for i in n]
        k = [k_ref[rows[i], :] for i in n]
        v = [v_ref[rows[i], :] for i in n]
        attn = [jnp.where(row == col, _mm_nt(q[i], k[i]), 0.0) for i in n]
        bc = [la_s[d, rows[i], :] for i, (_, d) in enumerate(chains)]
        sh = 1
        while sh < c:
            for i in n:
                if rev[i]:
                    bc[i] = bc[i] + jnp.where(row_k < c - sh, pltpu.roll(bc[i], c - sh, axis=0), 0.0)
                else:
                    bc[i] = bc[i] + jnp.where(row_k >= sh, pltpu.roll(bc[i], sh, axis=0), 0.0)
            sh *= 2
        s = c // 2
        while s >= 1:
            blk = 2 * s
            pos = row_k & (blk - 1)
            for i in n:
                bref = _block_row_bcast(bc[i], blk, s if rev[i] else s - 1)
                is_q = (pos < s) if rev[i] else (pos >= s)
                x = jnp.where(is_q, q[i], k[i]) * jnp.exp(-jnp.abs(bc[i] - bref))
                qs = jnp.where(is_q, x, 0.0)
                ks = jnp.where(is_q, 0.0, x)
                attn[i] = attn[i] + jnp.where((row ^ col) < blk, _mm_nt(qs, ks), 0.0)
            s //= 2
        b_end = [bc[i][0:1, :] if rev[i] else bc[i][c - 1:c, :] for i in n]
        st = [st_s[d] for _, d in chains]
        o = [_mm(attn[i], v[i]) + _mm_nt(q[i] * jnp.exp(bc[i]), st[i]) for i in n]
        upd = [_mm_tn(v[i], k[i] * jnp.exp(b_end[i] - bc[i])) for i in n]
        for i, (_, d) in enumerate(chains):
            st_s[d] = st[i] * jnp.exp(b_end[i]) + upd[i]
        for i in n:
            oacc[rows[i], :] += o[i]

    for d in range(2):
        if has_state:
            st_s[d] = s0_ref[d].T
        else:
            st_s[d] = jnp.zeros((DV_C, DK_C), F32)
    oacc[...] = jnp.zeros((t, DV_C), F32)

    def body(i, carry):
        chunks([(i, 0), (nc - 1 - i, 1)])
        return carry

    lax.fori_loop(0, nc, body, 0)

    o_ref[...] = (_rms(oacc[...]) * gg_ref[...] * _silu(z_ref[...])).astype(BF16)
    if not has_state:
        sf_ref[0] = st_s[0].T
        sf_ref[1] = st_s[1].T


def _gla(proj, glr, wg_pad, bgate, ggain, state, state_out, layer_o, nseq, seq_len):
    t = seq_len
    has_state = state is not None
    aliases = {}
    in_specs = [pl.BlockSpec((t, DK_C), lambda b, h: (b, h)),
                pl.BlockSpec((t, DK_C), lambda b, h: (b, H_C + h)),
                pl.BlockSpec((t, DV_C), lambda b, h: (b, H_C + h)),
                pl.BlockSpec((t, DV_C), lambda b, h: (b, 2 * H_C + h)),
                pl.BlockSpec((t, LANES), lambda b, h: (b, 0)),
                pl.BlockSpec((2, LANES, DK_C), lambda b, h: (0, 0, h)),
                pl.BlockSpec((2, 1, DK_C), lambda b, h: (0, 0, h)),
                pl.BlockSpec((1, DV_C), lambda b, h: (0, 0))]
    args = [proj, proj, proj, proj, glr, wg_pad, bgate, ggain]
    o_shape = jax.ShapeDtypeStruct((nseq * t, H_C * DV_C), BF16)
    o_spec = pl.BlockSpec((t, DV_C), lambda b, h: (b, h))
    if has_state:
        in_specs.append(pl.BlockSpec((None, None, 2, None, DK_C, DV_C),
                                     lambda b, h: (b, layer_o, 0, h, 0, 0)))
        args.append(state)
        out_shape, out_specs = o_shape, o_spec
    else:
        prev, n_slots = state_out
        out_shape = (o_shape, jax.ShapeDtypeStruct((nseq, n_slots, 2, H_C, DK_C, DV_C), F32))
        out_specs = (o_spec, pl.BlockSpec((None, None, 2, None, DK_C, DV_C),
                                          lambda b, h: (b, layer_o, 0, h, 0, 0)))
        if prev is not None:
            in_specs.append(pl.BlockSpec(memory_space=pl.ANY))
            args.append(prev)
            aliases = {len(args) - 1: 1}
    scratch = [pltpu.VMEM((2, t, DK_C), F32),
               pltpu.VMEM((t, DV_C), F32),
               pltpu.VMEM((2, DV_C, DK_C), F32)]
    return pl.pallas_call(
        functools.partial(_gla_kernel, seq_len=t, has_state=has_state),
        grid=(nseq, H_C),
        in_specs=in_specs,
        out_specs=out_specs,
        out_shape=out_shape,
        scratch_shapes=scratch,
        input_output_aliases=aliases,
        compiler_params=_params("parallel", "parallel"),
        name="gla",
    )(*args)


def _rope_tables(n_tokens):
    rows = n_tokens // GRID_W
    row = jnp.repeat(jnp.arange(rows), GRID_W).astype(F32)
    col = jnp.tile(jnp.arange(GRID_W), rows).astype(F32)
    quarter = DQK_B // 4
    freqs = ROPE_BASE ** (-jnp.arange(quarter, dtype=F32) / quarter)
    ar = row[:, None] * freqs
    ac = col[:, None] * freqs
    ang = jnp.concatenate([ar, ar, ac, ac] * (LANES // DQK_B), axis=-1)
    return jnp.cos(ang), jnp.sin(ang)


def _pad_lanes(x):
    return jnp.pad(x, ((0, 0), (0, LANES - x.shape[1])))


def kernel(x_prompt, x_sample, c, state_gdn, cache_k, cache_v, state_gla, c_ctx, w_ada, b_ada, norm_pre, norm_post, w_in_even, conv_even, a_log_even, dt_bias_even, gdn_norm_even, lam_even, diff_norm_even, w_out_even, w_in_odd, w_gate_odd, b_gate_odd, gla_norm_odd, w_out_odd):
    n_p, t_p, _ = x_prompt.shape
    n_s, t_s, _ = x_sample.shape
    wa = H_A * DK_A

    cvec = jnp.zeros((8, D_MODEL), F32).at[0].set(c_ctx).at[1:1 + n_s].set(c)
    mods = _adaln(cvec, w_ada, b_ada).reshape(DEPTH * 8, 1, 3 * D_MODEL)
    cos, sin = _rope_tables(t_s)

    y_p = x_prompt.reshape(n_p * t_p, D_MODEL)
    y_s = x_sample.reshape(n_s * t_s, D_MODEL)
    n_even, n_odd = w_in_even.shape[0], w_in_odd.shape[0]
    gdn_states = new_k = new_v = gla_states = None

    def gidx(layer):
        return _group_index(layer, 0, None), _group_index(layer, 1, t_s // ROW_TILE)

    gidx_p, gidx_s = gidx(0)
    h_p = _prenorm(y_p, norm_pre[0], mods, gidx_p)
    h_s = _prenorm(y_s, norm_pre[0], mods, gidx_s)

    for layer in range(DEPTH):
        gidx_p, gidx_s = gidx(layer)
        if layer + 1 < DEPTH:
            next_gain = norm_pre[layer + 1]
            gnext_p, gnext_s = gidx(layer + 1)
        else:
            next_gain = gnext_p = gnext_s = None
        if layer % 2 == 0:
            e = layer // 2
            lam_init = 0.8 - 0.6 * math.exp(-0.3 * layer)
            w_out = w_out_even[e].astype(BF16)
            zeros16 = jnp.zeros((2 * H_A,), F32)
            alog_row = _pad_lanes(jnp.concatenate([zeros16, a_log_even[e].reshape(-1)])[None, :])
            dtb_row = _pad_lanes(jnp.concatenate([zeros16, dt_bias_even[e].reshape(-1)])[None, :])
            gnorm = gdn_norm_even[e][None, :]
            dgain = diff_norm_even[e][None, :]
            col_b = 4 * wa
            shift_b = 4 * H_A

            pa_p = _inproj(h_p, w_in_even, e, 0, 4 * wa, 512, 1024)
            gp_p = _inproj(h_p, w_in_even, e, col_b, LANES, 512, LANES)
            qb_p = _inproj(h_p, w_in_even, e, col_b, wa, 512, wa, shift_b)
            zb_p = _inproj(h_p, w_in_even, e, col_b + 3 * wa, wa, 512, wa, shift_b)
            new_k = _inproj(h_p, w_in_even, e, col_b + wa, wa, t_p, wa, shift_b, (new_k, e, t_p))
            new_v = _inproj(h_p, w_in_even, e, col_b + 2 * wa, wa, t_p, wa, shift_b, (new_v, e, t_p))
            oa_p, gdn_states = _gdn(pa_p, gp_p, conv_even[e], alog_row, dtb_row, gnorm, None,
                                    (gdn_states, n_even), e, n_p, t_p)
            ob_p = _attn((qb_p, 0), (new_k, e), (new_v, e), (zb_p, 0), lam_even[e], dgain, lam_init,
                         None, e, n_p, t_p)
            out_p = _outproj([oa_p, ob_p], w_out, y_p, norm_post[layer], mods, gidx_p, next_gain, gnext_p)

            pa_s = _inproj(h_s, w_in_even, e, 0, 4 * wa, 512, 1024)
            gp_s = _inproj(h_s, w_in_even, e, col_b, LANES, 512, LANES)
            pb_s = _inproj(h_s, w_in_even, e, col_b, 4 * wa, 512, 1024, shift_b)
            oa_s = _gdn(pa_s, gp_s, conv_even[e], alog_row, dtb_row, gnorm, state_gdn, None, e, n_s, t_s)
            ob_s = _attn((pb_s, 0), (pb_s, 1), (pb_s, 2), (pb_s, 3), lam_even[e], dgain, lam_init,
                         (cos, sin, cache_k, cache_v), e, n_s, t_s)
            out_s = _outproj([oa_s, ob_s], w_out, y_s, norm_post[layer], mods, gidx_s, next_gain, gnext_s)
        else:
            od = layer // 2
            n_main = 2 * H_C * DK_C + 2 * H_C * DV_C
            w_glr = _pad_lanes(w_in_odd[od][:, n_main:])[None]
            w_out = w_out_odd[od].astype(BF16)
            wg_pad = jnp.zeros((2, LANES, H_C * DK_C), F32)
            for d in range(2):
                wg_pad = wg_pad.at[d, d * GATE_RANK:(d + 1) * GATE_RANK].set(w_gate_odd[od, d])
            bgate = b_gate_odd[od].reshape(2, 1, H_C * DK_C)
            ggain = gla_norm_odd[od][None, :]

            proj_p = _inproj(h_p, w_in_odd, od, 0, n_main, 512, 1024)
            glr_p = _inproj(h_p, w_glr, 0, 0, LANES, 512, LANES)
            o_p, gla_states = _gla(proj_p, glr_p, wg_pad, bgate, ggain, None, (gla_states, n_odd),
                                   od, n_p, t_p)
            out_p = _outproj([o_p], w_out, y_p, norm_post[layer], mods, gidx_p, next_gain, gnext_p)

            proj_s = _inproj(h_s, w_in_odd, od, 0, n_main, 512, 1024)
            glr_s = _inproj(h_s, w_glr, 0, 0, LANES, 512, LANES)
            o_s = _gla(proj_s, glr_s, wg_pad, bgate, ggain, state_gla, None, od, n_s, t_s)
            out_s = _outproj([o_s], w_out, y_s, norm_post[layer], mods, gidx_s, next_gain, gnext_s)
        if next_gain is None:
            y_p, y_s = out_p, out_s
        else:
            (y_p, h_p), (y_s, h_s) = out_p, out_s

    return (y_p.reshape(n_p, t_p, D_MODEL),
            y_s.reshape(n_s, t_s, D_MODEL),
            gdn_states,
            new_k.reshape(n_p, n_even, t_p, H_B, 2 * DQK_B),
            new_v.reshape(n_p, n_even, t_p, H_B, DV_B),
            gla_states)
```

```python
import functools
import math

import jax
import jax.numpy as jnp
from jax import lax
from jax.experimental import pallas as pl
from jax.experimental.pallas import tpu as pltpu

F32 = jnp.float32
BF16 = jnp.bfloat16
HIGHEST = lax.Precision.HIGHEST

D_MODEL = 2048
DEPTH = 4
EPS = 1e-6
GRID_W = 64
H_A = 8
DK_A = 128
DV_A = 128
CONV_K = 5
CHUNK_A = 64
PREP_UNROLL = 2
GDN_MAX_HEADS = 4
H_B = 8
DQK_B = 64
DV_B = 128
ROPE_BASE = 10000.0
Q_TILE = 256
Q_TILE_CTX = 256
ATTN_HEADS = 4
ATTN_HEADS_CTX = 2
H_C = 4
DK_C = 256
DV_C = 512
GATE_RANK = 16
GLA_NORMALIZER = 16.0
CHUNK_C = 64
GLA_MAX_HEADS = 2

LANES = 128
VMEM_LIMIT = 48 * 1024 * 1024
ROW_TILE = 512
IN_TILE = 1024


def _params(*semantics):
    return pltpu.CompilerParams(dimension_semantics=semantics,
                                vmem_limit_bytes=VMEM_LIMIT)


def _mm(a, b):
    return jnp.dot(a.astype(BF16), b.astype(BF16), preferred_element_type=F32)


def _mm_nt(a, b):
    return lax.dot_general(a.astype(BF16), b.astype(BF16),
                           (((1,), (1,)), ((), ())), preferred_element_type=F32)


def _mm_tn(a, b):
    return lax.dot_general(a.astype(BF16), b.astype(BF16),
                           (((0,), (0,)), ((), ())), preferred_element_type=F32)


def _mm_hi(a, b):
    return jnp.dot(a, b, precision=HIGHEST, preferred_element_type=F32)


def _mm_tn_hi(a, b):
    return lax.dot_general(a, b, (((0,), (0,)), ((), ())), precision=HIGHEST,
                           preferred_element_type=F32)


def _aligned(start, size, align):
    if isinstance(start, int):
        return pl.ds(start, size)
    return pl.ds(pl.multiple_of(start, align), size)


def _silu(x):
    return x * jax.nn.sigmoid(x)


def _softplus(x):
    return jnp.maximum(x, 0.0) + jnp.log1p(jnp.exp(-jnp.abs(x)))


def _rms(x):
    return x * lax.rsqrt(jnp.mean(x * x, axis=-1, keepdims=True) + EPS)


def _adaln_kernel(c_ref, w_ref, b_ref, o_ref):
    o_ref[...] = _mm(_silu(c_ref[...]), w_ref[...]) + b_ref[...]


def _adaln(cvec, w_ada, b_ada):
    tn = 1024
    n3 = 3 * D_MODEL
    return pl.pallas_call(
        _adaln_kernel,
        grid=(DEPTH, n3 // tn),
        in_specs=[pl.BlockSpec((8, D_MODEL), lambda l, j: (0, 0)),
                  pl.BlockSpec((None, D_MODEL, tn), lambda l, j: (l, 0, j)),
                  pl.BlockSpec((None, 1, tn), lambda l, j: (l, 0, j))],
        out_specs=pl.BlockSpec((None, 8, tn), lambda l, j: (l, 0, j)),
        out_shape=jax.ShapeDtypeStruct((DEPTH, 8, n3), F32),
        compiler_params=_params("parallel", "parallel"),
        name="adaln",
    )(cvec, w_ada, b_ada.reshape(DEPTH, 1, n3))


def _prenorm_kernel(x_ref, g_ref, m_ref, h_ref):
    y = _rms(x_ref[...]) * g_ref[...]
    m = m_ref[...]
    h_ref[...] = (y * (1.0 + m[:, D_MODEL:2 * D_MODEL]) + m[:, :D_MODEL]).astype(BF16)


def _group_index(layer, group0, tiles_per_group):
    if tiles_per_group is None:
        return lambda i: (layer * 8 + group0, 0, 0)
    return lambda i: (layer * 8 + group0 + i // tiles_per_group, 0, 0)


def _prenorm(y, gain, mods, gidx):
    m = y.shape[0]
    return pl.pallas_call(
        _prenorm_kernel,
        grid=(m // ROW_TILE,),
        in_specs=[pl.BlockSpec((ROW_TILE, D_MODEL), lambda i: (i, 0)),
                  pl.BlockSpec((1, D_MODEL), lambda i: (0, 0)),
                  pl.BlockSpec((None, 1, 3 * D_MODEL), gidx)],
        out_specs=pl.BlockSpec((ROW_TILE, D_MODEL), lambda i: (i, 0)),
        out_shape=jax.ShapeDtypeStruct((m, D_MODEL), BF16),
        compiler_params=_params("parallel"),
        name="prenorm",
    )(y, gain.reshape(1, D_MODEL), mods)


def _inproj_kernel(*refs, shift, valid):
    x_ref, wa_ref = refs[:2]
    wb_ref = refs[2] if shift else None
    o_ref, wbf = refs[-2:]

    @pl.when(pl.program_id(1) == 0)
    def _():
        w = wa_ref[...]
        if shift:
            w = jnp.concatenate([w[shift:], wb_ref[...]], axis=0)
        if valid is not None:
            w = jnp.where(lax.broadcasted_iota(jnp.int32, w.shape, 0) < valid, w, 0.0)
        wbf[...] = w.astype(BF16)

    res = lax.dot_general(x_ref[...], wbf[...], (((1,), (1,)), ((), ())), preferred_element_type=F32)
    o_ref[...] = res.reshape(o_ref.shape)


def _inproj(x, wt3, layer_idx, row0, n_out, tm, tn, shift=0, cache_slot=None):
    m, k = x.shape
    n_rows = wt3.shape[1]
    valid = n_rows - row0 if row0 + n_out > n_rows else None
    assert valid is None or (n_out == tn and not shift)
    in_specs = [pl.BlockSpec((tm, k), lambda j, i: (i, 0)),
                pl.BlockSpec((None, tn, k), lambda j, i: (layer_idx, row0 // tn + j, 0))]
    args = [x, wt3]
    if shift:
        in_specs.append(pl.BlockSpec((None, shift, k),
                                     lambda j, i: (layer_idx, (row0 + (j + 1) * tn) // shift, 0)))
        args.append(wt3)
    aliases = {}
    if cache_slot is None:
        out_spec = pl.BlockSpec((tm, tn), lambda j, i: (i, j))
        out_shape = jax.ShapeDtypeStruct((m, n_out), F32)
    else:
        prev, slot, t = cache_slot
        out_spec = pl.BlockSpec((tm // t, None, t, tn), lambda j, i: (i, slot, 0, 0))
        out_shape = jax.ShapeDtypeStruct((m // t, wt3.shape[0], t, n_out), F32)
        if prev is not None:
            in_specs.append(pl.BlockSpec(memory_space=pl.ANY))
            args.append(prev)
            aliases = {len(args) - 1: 0}

    return pl.pallas_call(
        functools.partial(_inproj_kernel, shift=shift, valid=valid),
        grid=(n_out // tn, m // tm),
        in_specs=in_specs,
        out_specs=out_spec,
        out_shape=out_shape,
        scratch_shapes=[pltpu.VMEM((tn, k), BF16)],
        input_output_aliases=aliases,
        compiler_params=_params("parallel", "arbitrary"),
        name="inproj",
    )(*args)


def _outproj_kernel(*refs, n_in, emit_h):
    x_refs = refs[:n_in]
    w_ref, y_ref, g_ref, m_ref = refs[n_in:n_in + 4]
    acc = None
    off = 0
    for x_ref in x_refs:
        kk = x_ref.shape[1]
        part = jnp.dot(x_ref[...], w_ref[off:off + kk, :], preferred_element_type=F32)
        acc = part if acc is None else acc + part
        off += kk
    m = m_ref[...]
    y = y_ref[...] + m[:, 2 * D_MODEL:] * (_rms(acc) * g_ref[...])
    if emit_h:
        gn_ref, mn_ref, o_ref, h_ref = refs[n_in + 4:]
        mn = mn_ref[...]
        h_ref[...] = (_rms(y) * gn_ref[...] * (1.0 + mn[:, D_MODEL:2 * D_MODEL]) + mn[:, :D_MODEL]).astype(BF16)
    else:
        o_ref = refs[n_in + 4]
    o_ref[...] = y


def _outproj(xs, w, y, gain, mods, gidx, next_gain=None, gidx_next=None):
    m = y.shape[0]
    emit_h = next_gain is not None
    row_spec = pl.BlockSpec((ROW_TILE, D_MODEL), lambda i: (i, 0))
    vec_spec = pl.BlockSpec((1, D_MODEL), lambda i: (0, 0))
    in_specs = [pl.BlockSpec((ROW_TILE, x.shape[1]), lambda i: (i, 0)) for x in xs]
    in_specs += [pl.BlockSpec(w.shape, lambda i: (0, 0)), row_spec, vec_spec,
                 pl.BlockSpec((None, 1, 3 * D_MODEL), gidx)]
    args = [*xs, w, y, gain.reshape(1, D_MODEL), mods]
    out_specs = row_spec
    out_shape = jax.ShapeDtypeStruct((m, D_MODEL), F32)
    if emit_h:
        in_specs += [vec_spec, pl.BlockSpec((None, 1, 3 * D_MODEL), gidx_next)]
        args += [next_gain.reshape(1, D_MODEL), mods]
        out_specs = (row_spec, row_spec)
        out_shape = (out_shape, jax.ShapeDtypeStruct((m, D_MODEL), BF16))
    return pl.pallas_call(
        functools.partial(_outproj_kernel, n_in=len(xs), emit_h=emit_h),
        grid=(m // ROW_TILE,),
        in_specs=in_specs,
        out_specs=out_specs,
        out_shape=out_shape,
        compiler_params=_params("parallel"),
        name="outproj",
    )(*args)


def _gdn_kernel(*refs, seq_len, heads, has_state):
    t = seq_len
    c = CHUNK_A
    nc = t // c
    (q_ref, k_ref, v_ref, z_ref, gp_ref, cwq_ref, cwk_ref, cwv_ref,
     alog_ref, dtb_ref, gn_ref) = refs[:11]
    qn, kn, vn, bt, gt, wv_s, wkqd_s, qk_s, kd_s, gl_s, oacc, s_s = refs[-12:]
    if has_state:
        s0_ref, o_ref = refs[11], refs[-13]
    else:
        o_ref, sf_ref = refs[-14:-12]
    head0 = pl.program_id(1) * heads

    def lanes_of(hd):
        return slice(hd * LANES, (hd + 1) * LANES)

    row_t = lax.broadcasted_iota(jnp.int32, (t, LANES), 0)

    def conv_silu(x_ref, w_ref, hd):
        x = x_ref[:, lanes_of(hd)]
        w = w_ref[:, lanes_of(hd)]
        acc = x * w[CONV_K // 2:CONV_K // 2 + 1, :]
        for tap in range(CONV_K):
            d = tap - CONV_K // 2
            if d == 0:
                continue
            shifted = pltpu.roll(x, (-d) % t, axis=0)
            valid = (row_t >= -d) if d < 0 else (row_t < t - d)
            acc = acc + jnp.where(valid, shifted, 0.0) * w[tap:tap + 1, :]
        return _silu(acc)

    def l2n(x):
        return x * lax.rsqrt(jnp.sum(x * x, axis=-1, keepdims=True) + EPS)

    for hd in range(heads):
        qn[hd] = l2n(conv_silu(q_ref, cwq_ref, hd))
        kn[hd] = l2n(conv_silu(k_ref, cwk_ref, hd))
        vn[hd] = conv_silu(v_ref, cwv_ref, hd)

    gp = gp_ref[...]
    lane_t = lax.broadcasted_iota(jnp.int32, (t, LANES), 1)
    beta_all = jax.nn.sigmoid(gp)
    g_all = -jnp.exp(alog_ref[...]) * _softplus(gp + dtb_ref[...])

    def pick(x, idx):
        col = jnp.sum(jnp.where(lane_t == idx, x, 0.0), axis=-1, keepdims=True)
        return jnp.broadcast_to(col, (t, LANES))

    pos_t = row_t & (c - 1)
    g_cum = [g_all, g_all]
    sh = 1
    while sh < c:
        g_cum[0] = g_cum[0] + jnp.where(pos_t >= sh, pltpu.roll(g_cum[0], sh, axis=0), 0.0)
        g_cum[1] = g_cum[1] + jnp.where(pos_t < c - sh, pltpu.roll(g_cum[1], t - sh, axis=0), 0.0)
        sh *= 2

    for hd in range(heads):
        for d in range(2):
            bt[hd * 2 + d] = pick(beta_all, d * H_A + head0 + hd)
            gt[hd * 2 + d] = pick(g_cum[d], 2 * H_A + d * H_A + head0 + hd)

    r2 = 2 * c
    row = lax.broadcasted_iota(jnp.int32, (r2, r2), 0)
    col = lax.broadcasted_iota(jnp.int32, (r2, r2), 1)
    rc_xor = row ^ col

    tri_inc, tri_str = [], []
    for rev in (False, True):
        after = (col - row) if rev else (row - col)
        after = jnp.where(rc_xor < c, after, -1)
        tri_inc.append(after >= 0)
        tri_str.append(after > 0)
    eye = jnp.where(row == col, 1.0, 0.0)

    def prep(chains):
        n = range(len(chains))
        rows = [_aligned(gi * r2, r2, r2) for _, gi, _ in chains]
        dirs = [d for _, _, d in chains]
        hdir = [hd * 2 + d for hd, _, d in chains]
        q = [qn[chains[i][0], rows[i], :] * DK_A ** -0.5 for i in n]
        k = [kn[chains[i][0], rows[i], :] for i in n]
        beta = [bt[hdir[i], rows[i], :] for i in n]
        kb = [k[i] * beta[i] for i in n]
        kk = [_mm_nt(kb[i], k[i]) for i in n]
        qk = [_mm_nt(q[i], k[i]) for i in n]
        gcum = [gt[hdir[i], rows[i], :] for i in n]
        decay = [jnp.where(tri_inc[dirs[i]], jnp.exp(jnp.minimum(gcum[i] - gcum[i].T, 0.0)), 0.0) for i in n]
        amat = [jnp.where(tri_str[dirs[i]], kk[i] * decay[i], 0.0) for i in n]
        tinv = [eye - jnp.where(rc_xor < 2, amat[i], 0.0) for i in n]
        b = 2
        while b < c:
            level = (rc_xor >= b) & (rc_xor < 2 * b)
            m1 = [_mm(jnp.where(level, amat[i], 0.0), tinv[i]) for i in n]
            m2 = [_mm(tinv[i], m1[i]) for i in n]
            tinv = [tinv[i] - m2[i] for i in n]
            b *= 2
        eg = [jnp.exp(gcum[i]) for i in n]
        x = [_mm(tinv[i], jnp.concatenate([vn[chains[i][0], rows[i], :] * beta[i], kb[i] * eg[i]], axis=1))
             for i in n]
        for i in n:
            _, gi, d = chains[i]
            hd2 = hdir[i]
            qkd = qk[i] * decay[i]
            g_end = _block_row_bcast(gcum[i], c, 0 if d == 1 else c - 1)
            qd = q[i] * eg[i]
            gl = jnp.exp(g_end)
            wv_s[hd2, rows[i], :] = x[i][:, :DV_A]
            qk_s[hd2, rows[i], :] = jnp.where(row < c, qkd, pltpu.roll(qkd, c, axis=1))[:, :c]
            kd_s[hd2, rows[i], :] = k[i] * jnp.exp(g_end - gcum[i])
            for j in range(2):
                base = gi * 2 * r2 + j * r2
                wkqd_s[hd2, _aligned(base, c, c), :] = x[i][j * c:(j + 1) * c, DV_A:]
                wkqd_s[hd2, _aligned(base + c, c, c), :] = qd[j * c:(j + 1) * c, :]
                gl_s[hd2, _aligned(gi * 16 + j * 8, 8, 8), :] = gl[j * c:j * c + 8, :]

    def rec(chains):
        n = range(len(chains))
        rows = [_aligned(ci * c, c, c) for _, ci, _ in chains]
        hdir = [hd * 2 + d for hd, _, d in chains]
        s = [s_s[hdir[i]] for i in n]
        ws = [_mm(wkqd_s[hdir[i], _aligned(chains[i][1] * 2 * c, 2 * c, 2 * c), :], s[i]) for i in n]
        u = [wv_s[hdir[i], rows[i], :] - ws[i][:c] for i in n]
        o2 = [_mm(qk_s[hdir[i], rows[i], :], u[i]) for i in n]
        ds = [_mm_tn(kd_s[hdir[i], rows[i], :], u[i]) for i in n]
        for i in n:
            s_s[hdir[i]] = s[i] * gl_s[hdir[i], _aligned(chains[i][1] * 8, 1, 8), :] + ds[i]
        for i in n:
            oacc[chains[i][0], rows[i], :] += ws[i][c:] + o2[i]

    for hd in range(heads):
        for d in range(2):
            if has_state:
                s_s[hd * 2 + d] = s0_ref[d, hd]
            else:
                s_s[hd * 2 + d] = jnp.zeros((DK_A, DV_A), F32)
    oacc[...] = jnp.zeros((heads, t, DV_A), F32)

    def prep_body(gi, carry):
        prep([(hd, gi * PREP_UNROLL + j, d)
              for hd in range(heads) for j in range(PREP_UNROLL) for d in range(2)])
        return carry

    if nc // 2 == PREP_UNROLL:
        prep_body(0, 0)
    else:
        lax.fori_loop(0, nc // 2 // PREP_UNROLL, prep_body, 0)

    def rec_body(i, carry):
        rec([(hd, i if d == 0 else nc - 1 - i, d) for hd in range(heads) for d in range(2)])
        return carry

    lax.fori_loop(0, nc, rec_body, 0)

    for hd in range(heads):
        o_ref[:, lanes_of(hd)] = (_rms(oacc[hd]) * gn_ref[...] * _silu(z_ref[:, lanes_of(hd)])).astype(BF16)
        if not has_state:
            sf_ref[0, hd] = s_s[hd * 2]
            sf_ref[1, hd] = s_s[hd * 2 + 1]


def _gdn_heads(t):
    for g in (GDN_MAX_HEADS, 2, 1):
        scratch = g * t * LANES * 4 * 17 + 2 * g * DK_A * DV_A * 4
        blocks = 2 * (4 * t * g * LANES * 4 + t * LANES * 4 + t * g * LANES * 2)
        if scratch + blocks <= VMEM_LIMIT - (8 << 20):
            return g
    return 1


def _gdn(proj, gp, conv_w, alog_row, dtb_row, gnorm, state, state_out, layer_e, nseq, seq_len):
    t = seq_len
    nc = t // CHUNK_A
    has_state = state is not None
    aliases = {}

    g = _gdn_heads(t)
    w = g * LANES
    nb = H_A // g

    def col(seg):
        return pl.BlockSpec((t, w), lambda b, h: (b, seg * nb + h))

    def cw(seg):
        return pl.BlockSpec((CONV_K, w), lambda b, h: (0, seg * nb + h))

    row_spec = pl.BlockSpec((1, LANES), lambda b, h: (0, 0))
    in_specs = [col(0), col(1), col(2), col(3),
                pl.BlockSpec((t, LANES), lambda b, h: (b, 0)),
                cw(0), cw(1), cw(2), row_spec, row_spec, row_spec]
    args = [proj, proj, proj, proj, gp, conv_w, conv_w, conv_w, alog_row, dtb_row, gnorm]
    o_shape = jax.ShapeDtypeStruct((nseq * t, H_A * DV_A), BF16)
    o_spec = pl.BlockSpec((t, w), lambda b, h: (b, h))
    if has_state:
        in_specs.append(pl.BlockSpec((None, None, 2, g, DK_A, DV_A),
                                     lambda b, h: (b, layer_e, 0, h, 0, 0)))
        args.append(state)
        out_shape, out_specs = o_shape, o_spec
    else:
        prev, n_slots = state_out
        out_shape = (o_shape, jax.ShapeDtypeStruct((nseq, n_slots, 2, H_A, DK_A, DV_A), F32))
        out_specs = (o_spec, pl.BlockSpec((None, None, 2, g, DK_A, DV_A),
                                          lambda b, h: (b, layer_e, 0, h, 0, 0)))
        if prev is not None:
            in_specs.append(pl.BlockSpec(memory_space=pl.ANY))
            args.append(prev)
            aliases = {len(args) - 1: 1}
    scratch = [pltpu.VMEM((g, t, LANES), F32)] * 3
    scratch += [pltpu.VMEM((2 * g, t, LANES), F32)] * 2
    scratch += [pltpu.VMEM((2 * g, t, LANES), F32),
                pltpu.VMEM((2 * g, 2 * t, LANES), F32),
                pltpu.VMEM((2 * g, t, CHUNK_A), F32),
                pltpu.VMEM((2 * g, t, LANES), F32),
                pltpu.VMEM((2 * g, nc * 8, LANES), F32),
                pltpu.VMEM((g, t, LANES), F32),
                pltpu.VMEM((2 * g, DK_A, DV_A), F32)]
    return pl.pallas_call(
        functools.partial(_gdn_kernel, seq_len=t, heads=g, has_state=has_state),
        grid=(nseq, nb),
        in_specs=in_specs,
        out_specs=out_specs,
        out_shape=out_shape,
        scratch_shapes=scratch,
        input_output_aliases=aliases,
        compiler_params=_params("parallel", "parallel"),
        name="gdn",
    )(*args)


def _rope(x, cos, sin):
    lane = lax.broadcasted_iota(jnp.int32, x.shape, 1)
    quarter = DQK_B // 4
    rot = jnp.where((lane & quarter) == 0,
                    -pltpu.roll(x, LANES - quarter, axis=1),
                    pltpu.roll(x, quarter, axis=1))
    return x * cos + rot * sin


def _attn_kernel(*refs, seq_len, heads, lam_init, has_ctx):
    t = seq_len
    q_ref, k_ref, v_ref, z_ref, lam_ref, dg_ref = refs[:6]
    if has_ctx:
        cq_ref, sq_ref, ck_ref, sk_ref, kc_ref, vc_ref, o_ref, k_all, v_all = refs[6:]
    else:
        o_ref = refs[6]

    def lanes_of(hd):
        return slice(hd * LANES, (hd + 1) * LANES)

    if has_ctx:
        @pl.when(pl.program_id(2) == 0)
        def _():
            for hd in range(heads):
                k_all[hd, 0:t, :] = _rope(k_ref[:, lanes_of(hd)], ck_ref[...], sk_ref[...])
                k_all[hd, t:, :] = kc_ref[:, lanes_of(hd)]
                v_all[hd, 0:t, :] = v_ref[:, lanes_of(hd)]
                v_all[hd, t:, :] = vc_ref[:, lanes_of(hd)]
    lp = lam_ref[...]
    lam = (jnp.exp(jnp.sum(lp[0:1] * lp[1:2], axis=-1, keepdims=True))
           - jnp.exp(jnp.sum(lp[2:3] * lp[3:4], axis=-1, keepdims=True)) + lam_init)
    lane = lax.broadcasted_iota(jnp.int32, (q_ref.shape[0], LANES), 1)

    chains = [(hd, m) for hd in range(heads) for m in range(2)]
    n = range(len(chains))
    q = []
    for hd in range(heads):
        qh = q_ref[:, lanes_of(hd)]
        if has_ctx:
            qh = _rope(qh, cq_ref[...], sq_ref[...])
        q.append(qh * DQK_B ** -0.5)
    k = [k_all[hd] if has_ctx else k_ref[:, lanes_of(hd)] for hd in range(heads)]
    v = [v_all[hd] if has_ctx else v_ref[:, lanes_of(hd)] for hd in range(heads)]
    qm = [jnp.where(lane < DQK_B, q[hd], 0.0) if m == 0 else jnp.where(lane < DQK_B, 0.0, q[hd])
          for hd, m in chains]
    s = [_mm_nt(qm[i], k[chains[i][0]]) for i in n]
    e = [jnp.exp(s[i] - jnp.max(s[i], axis=-1, keepdims=True)) for i in n]
    den = [jnp.sum(e[i], axis=-1, keepdims=True) for i in n]
    ev = [_mm(e[i], v[chains[i][0]]) for i in n]
    for hd in range(heads):
        o = ev[2 * hd] / den[2 * hd] - lam * (ev[2 * hd + 1] / den[2 * hd + 1])
        o_ref[:, lanes_of(hd)] = (_rms(o) * dg_ref[...] * (1.0 - lam_init)
                                  * _silu(z_ref[:, lanes_of(hd)])).astype(BF16)


def _attn(q, k, v, z, lam_p, dgain, lam_init, ctx, layer_e, nseq, seq_len):
    t = seq_len
    has_ctx = ctx is not None
    g = ATTN_HEADS_CTX if has_ctx else ATTN_HEADS
    tq = min(t, Q_TILE_CTX if has_ctx else Q_TILE)
    nq = t // tq
    w = g * LANES
    nb = H_B // g

    def rows_spec(src, rows):
        arr, seg = src
        if arr.ndim == 4:
            assert rows == t
            return pl.BlockSpec((None, None, t, w), lambda b, h, i: (b, seg, 0, h))
        if rows == t:
            return pl.BlockSpec((t, w), lambda b, h, i: (b, seg * nb + h))
        return pl.BlockSpec((rows, w), lambda b, h, i: (b * nq + i, seg * nb + h))

    in_specs = [rows_spec(q, tq), rows_spec(k, t), rows_spec(v, t), rows_spec(z, tq),
                pl.BlockSpec((4, DQK_B), lambda b, h, i: (0, 0)),
                pl.BlockSpec((1, LANES), lambda b, h, i: (0, 0))]
    args = [q[0], k[0], v[0], z[0], lam_p, dgain]
    scratch = []
    if has_ctx:
        cos, sin, cache_k, cache_v = ctx
        past = cache_k.shape[2]
        in_specs += [pl.BlockSpec((tq, LANES), lambda b, h, i: (i, 0)),
                     pl.BlockSpec((tq, LANES), lambda b, h, i: (i, 0)),
                     pl.BlockSpec((t, LANES), lambda b, h, i: (0, 0)),
                     pl.BlockSpec((t, LANES), lambda b, h, i: (0, 0)),
                     pl.BlockSpec((None, None, past, w), lambda b, h, i: (b, layer_e, 0, h)),
                     pl.BlockSpec((None, None, past, w), lambda b, h, i: (b, layer_e, 0, h))]
        args += [cos, sin, cos, sin, cache_k.reshape(cache_k.shape[:3] + (-1,)),
                 cache_v.reshape(cache_v.shape[:3] + (-1,))]
        scratch = [pltpu.VMEM((g, t + past, LANES), F32)] * 2
    return pl.pallas_call(
        functools.partial(_attn_kernel, seq_len=t, heads=g, lam_init=lam_init, has_ctx=has_ctx),
        grid=(nseq, nb, nq),
        in_specs=in_specs,
        out_specs=pl.BlockSpec((tq, w), lambda b, h, i: (b * nq + i, h)),
        out_shape=jax.ShapeDtypeStruct((nseq * t, H_B * DV_B), BF16),
        scratch_shapes=scratch,
        compiler_params=_params("parallel", "parallel", "arbitrary"),
        name="diffattn",
    )(*args)


def _block_row_bcast(x, blk, r):
    n, w = x.shape
    if blk >= 8:
        x3 = x.reshape(n // blk, blk, w)
        return jnp.broadcast_to(x3[:, r:r + 1, :], x3.shape).reshape(n, w)
    x3 = x.reshape(n // 8, 8, w)
    sub = lax.broadcasted_iota(jnp.int32, x3.shape, 1)
    out = None
    for b in range(8 // blk):
        cand = jnp.broadcast_to(x3[:, b * blk + r:b * blk + r + 1, :], x3.shape)
        out = cand if out is None else jnp.where(sub >= b * blk, cand, out)
    return out.reshape(n, w)


def _gla_kernel(*refs, seq_len, heads, has_state):
    t = seq_len
    c = CHUNK_C
    nc = t // c
    q_ref, k_ref, v_ref, z_ref, glr_ref, wg_ref, bg_ref, gg_ref = refs[:8]
    la_s, oacc, st_s = refs[-3:]
    if has_state:
        s0_ref, o_ref = refs[8], refs[-4]
    else:
        o_ref, sf_ref = refs[-5:-3]

    def kcols(hd):
        return slice(hd * DK_C, (hd + 1) * DK_C)

    def vcols(hd):
        return slice(hd * DV_C, (hd + 1) * DV_C)

    glr = glr_ref[...]
    for hd in range(heads):
        for d in range(2):
            gate = _mm(glr, wg_ref[d, :, kcols(hd)]) + bg_ref[d, :, kcols(hd)]
            la_s[hd * 2 + d] = ((jnp.minimum(gate, 0.0) - jnp.log1p(jnp.exp(-jnp.abs(gate))))
                                * (1.0 / GLA_NORMALIZER))

    row = lax.broadcasted_iota(jnp.int32, (c, c), 0)
    col = lax.broadcasted_iota(jnp.int32, (c, c), 1)
    row_k = lax.broadcasted_iota(jnp.int32, (c, DK_C), 0)

    def chunks(chains):
        n = range(len(chains))
        rows = [_aligned(ci * c, c, c) for _, ci, _ in chains]
        rev = [d == 1 for _, _, d in chains]
        hdir = [hd * 2 + d for hd, _, d in chains]
        q = [q_ref[rows[i], kcols(chains[i][0])] * DK_C ** -0.5 for i in n]
        k = [k_ref[rows[i], kcols(chains[i][0])] for i in n]
        v = [v_ref[rows[i], vcols(chains[i][0])] for i in n]
        attn = [jnp.where(row == col, _mm_nt(q[i], k[i]), 0.0) for i in n]
        bc = [la_s[hdir[i], rows[i], :] for i in n]
        sh = 1
        while sh < c:
            for i in n:
                if rev[i]:
                    bc[i] = bc[i] + jnp.where(row_k < c - sh, pltpu.roll(bc[i], c - sh, axis=0), 0.0)
                else:
                    bc[i] = bc[i] + jnp.where(row_k >= sh, pltpu.roll(bc[i], sh, axis=0), 0.0)
            sh *= 2
        s = c // 2
        while s >= 1:
            blk = 2 * s
            pos = row_k & (blk - 1)
            for i in n:
                bref = _block_row_bcast(bc[i], blk, s if rev[i] else s - 1)
                is_q = (pos < s) if rev[i] else (pos >= s)
                x = jnp.where(is_q, q[i], k[i]) * jnp.exp(-jnp.abs(bc[i] - bref))
                qs = jnp.where(is_q, x, 0.0)
                ks = jnp.where(is_q, 0.0, x)
                attn[i] = attn[i] + jnp.where((row ^ col) < blk, _mm_nt(qs, ks), 0.0)
            s //= 2
        b_end = [bc[i][0:1, :] if rev[i] else bc[i][c - 1:c, :] for i in n]
        st = [st_s[hdir[i]] for i in n]
        o = [_mm(attn[i], v[i]) + _mm_nt(q[i] * jnp.exp(bc[i]), st[i]) for i in n]
        upd = [_mm_tn(v[i], k[i] * jnp.exp(b_end[i] - bc[i])) for i in n]
        for i in n:
            st_s[hdir[i]] = st[i] * jnp.exp(b_end[i]) + upd[i]
        for i in n:
            oacc[chains[i][0], rows[i], :] += o[i]

    for hd in range(heads):
        for d in range(2):
            if has_state:
                st_s[hd * 2 + d] = s0_ref[d, hd].T
            else:
                st_s[hd * 2 + d] = jnp.zeros((DV_C, DK_C), F32)
    oacc[...] = jnp.zeros((heads, t, DV_C), F32)

    def body(i, carry):
        chunks([(hd, i if d == 0 else nc - 1 - i, d) for hd in range(heads) for d in range(2)])
        return carry

    lax.fori_loop(0, nc, body, 0)

    for hd in range(heads):
        o_ref[:, vcols(hd)] = (_rms(oacc[hd]) * gg_ref[...] * _silu(z_ref[:, vcols(hd)])).astype(BF16)
        if not has_state:
            sf_ref[0, hd] = st_s[hd * 2].T
            sf_ref[1, hd] = st_s[hd * 2 + 1].T


def _gla_heads(t):
    for g in (GLA_MAX_HEADS, 1):
        scratch = g * (2 * t * DK_C + t * DV_C + 2 * DV_C * DK_C) * 4
        blocks = 2 * (g * t * (2 * DK_C + 2 * DV_C) * 4 + t * LANES * 4 + g * t * DV_C * 2
                      + 2 * g * DK_C * DV_C * 4)
        if scratch + blocks <= VMEM_LIMIT - (12 << 20):
            return g
    return 1


def _gla(proj, glr, wg_pad, bgate, ggain, state, state_out, layer_o, nseq, seq_len):
    t = seq_len
    has_state = state is not None
    aliases = {}
    g = _gla_heads(t)
    nb = H_C // g
    wk, wv = g * DK_C, g * DV_C
    in_specs = [pl.BlockSpec((t, wk), lambda b, h: (b, h)),
                pl.BlockSpec((t, wk), lambda b, h: (b, nb + h)),
                pl.BlockSpec((t, wv), lambda b, h: (b, nb + h)),
                pl.BlockSpec((t, wv), lambda b, h: (b, 2 * nb + h)),
                pl.BlockSpec((t, LANES), lambda b, h: (b, 0)),
                pl.BlockSpec((2, LANES, wk), lambda b, h: (0, 0, h)),
                pl.BlockSpec((2, 1, wk), lambda b, h: (0, 0, h)),
                pl.BlockSpec((1, DV_C), lambda b, h: (0, 0))]
    args = [proj, proj, proj, proj, glr, wg_pad, bgate, ggain]
    o_shape = jax.ShapeDtypeStruct((nseq * t, H_C * DV_C), BF16)
    o_spec = pl.BlockSpec((t, wv), lambda b, h: (b, h))
    if has_state:
        in_specs.append(pl.BlockSpec((None, None, 2, g, DK_C, DV_C),
                                     lambda b, h: (b, layer_o, 0, h, 0, 0)))
        args.append(state)
        out_shape, out_specs = o_shape, o_spec
    else:
        prev, n_slots = state_out
        out_shape = (o_shape, jax.ShapeDtypeStruct((nseq, n_slots, 2, H_C, DK_C, DV_C), F32))
        out_specs = (o_spec, pl.BlockSpec((None, None, 2, g, DK_C, DV_C),
                                          lambda b, h: (b, layer_o, 0, h, 0, 0)))
        if prev is not None:
            in_specs.append(pl.BlockSpec(memory_space=pl.ANY))
            args.append(prev)
            aliases = {len(args) - 1: 1}
    scratch = [pltpu.VMEM((2 * g, t, DK_C), F32),
               pltpu.VMEM((g, t, DV_C), F32),
               pltpu.VMEM((2 * g, DV_C, DK_C), F32)]
    return pl.pallas_call(
        functools.partial(_gla_kernel, seq_len=t, heads=g, has_state=has_state),
        grid=(nseq, nb),
        in_specs=in_specs,
        out_specs=out_specs,
        out_shape=out_shape,
        scratch_shapes=scratch,
        input_output_aliases=aliases,
        compiler_params=_params("parallel", "parallel"),
        name="gla",
    )(*args)


def _rope_tables(n_tokens):
    rows = n_tokens // GRID_W
    row = jnp.repeat(jnp.arange(rows), GRID_W).astype(F32)
    col = jnp.tile(jnp.arange(GRID_W), rows).astype(F32)
    quarter = DQK_B // 4
    freqs = ROPE_BASE ** (-jnp.arange(quarter, dtype=F32) / quarter)
    ar = row[:, None] * freqs
    ac = col[:, None] * freqs
    ang = jnp.concatenate([ar, ar, ac, ac] * (LANES // DQK_B), axis=-1)
    return jnp.cos(ang), jnp.sin(ang)


def _pad_lanes(x):
    return jnp.pad(x, ((0, 0), (0, LANES - x.shape[1])))


def kernel(x_prompt, x_sample, c, state_gdn, cache_k, cache_v, state_gla, c_ctx, w_ada, b_ada, norm_pre, norm_post, w_in_even, conv_even, a_log_even, dt_bias_even, gdn_norm_even, lam_even, diff_norm_even, w_out_even, w_in_odd, w_gate_odd, b_gate_odd, gla_norm_odd, w_out_odd):
    n_p, t_p, _ = x_prompt.shape
    n_s, t_s, _ = x_sample.shape
    wa = H_A * DK_A

    cvec = jnp.zeros((8, D_MODEL), F32).at[0].set(c_ctx).at[1:1 + n_s].set(c)
    mods = _adaln(cvec, w_ada, b_ada).reshape(DEPTH * 8, 1, 3 * D_MODEL)
    cos, sin = _rope_tables(t_s)

    y_p = x_prompt.reshape(n_p * t_p, D_MODEL)
    y_s = x_sample.reshape(n_s * t_s, D_MODEL)
    n_even, n_odd = w_in_even.shape[0], w_in_odd.shape[0]
    wt_even = jnp.swapaxes(w_in_even, 1, 2)
    wt_odd = jnp.swapaxes(w_in_odd, 1, 2)
    gdn_states = new_k = new_v = gla_states = None

    def gidx(layer):
        return _group_index(layer, 0, None), _group_index(layer, 1, t_s // ROW_TILE)

    gidx_p, gidx_s = gidx(0)
    h_p = _prenorm(y_p, norm_pre[0], mods, gidx_p)
    h_s = _prenorm(y_s, norm_pre[0], mods, gidx_s)

    for layer in range(DEPTH):
        gidx_p, gidx_s = gidx(layer)
        if layer + 1 < DEPTH:
            next_gain = norm_pre[layer + 1]
            gnext_p, gnext_s = gidx(layer + 1)
        else:
            next_gain = gnext_p = gnext_s = None
        if layer % 2 == 0:
            e = layer // 2
            lam_init = 0.8 - 0.6 * math.exp(-0.3 * layer)
            w_out = w_out_even[e].astype(BF16)
            zeros16 = jnp.zeros((2 * H_A,), F32)
            alog_row = _pad_lanes(jnp.concatenate([zeros16, a_log_even[e].reshape(-1)])[None, :])
            dtb_row = _pad_lanes(jnp.concatenate([zeros16, dt_bias_even[e].reshape(-1)])[None, :])
            gnorm = gdn_norm_even[e][None, :]
            dgain = diff_norm_even[e][None, :]
            col_b = 4 * wa
            shift_b = 4 * H_A

            pa_p = _inproj(h_p, wt_even, e, 0, 4 * wa, IN_TILE, 1024)
            gp_p = _inproj(h_p, wt_even, e, col_b, LANES, IN_TILE, LANES)
            qb_p = _inproj(h_p, wt_even, e, col_b, wa, IN_TILE, wa, shift_b)
            zb_p = _inproj(h_p, wt_even, e, col_b + 3 * wa, wa, IN_TILE, wa, shift_b)
            new_k = _inproj(h_p, wt_even, e, col_b + wa, wa, IN_TILE, wa, shift_b, (new_k, e, t_p))
            new_v = _inproj(h_p, wt_even, e, col_b + 2 * wa, wa, IN_TILE, wa, shift_b, (new_v, e, t_p))
            oa_p, gdn_states = _gdn(pa_p, gp_p, conv_even[e], alog_row, dtb_row, gnorm, None,
                                    (gdn_states, n_even), e, n_p, t_p)
            ob_p = _attn((qb_p, 0), (new_k, e), (new_v, e), (zb_p, 0), lam_even[e], dgain, lam_init,
                         None, e, n_p, t_p)
            out_p = _outproj([oa_p, ob_p], w_out, y_p, norm_post[layer], mods, gidx_p, next_gain, gnext_p)

            pa_s = _inproj(h_s, wt_even, e, 0, 4 * wa, IN_TILE, 1024)
            gp_s = _inproj(h_s, wt_even, e, col_b, LANES, IN_TILE, LANES)
            pb_s = _inproj(h_s, wt_even, e, col_b, 4 * wa, IN_TILE, 1024, shift_b)
            oa_s = _gdn(pa_s, gp_s, conv_even[e], alog_row, dtb_row, gnorm, state_gdn, None, e, n_s, t_s)
            ob_s = _attn((pb_s, 0), (pb_s, 1), (pb_s, 2), (pb_s, 3), lam_even[e], dgain, lam_init,
                         (cos, sin, cache_k, cache_v), e, n_s, t_s)
            out_s = _outproj([oa_s, ob_s], w_out, y_s, norm_post[layer], mods, gidx_s, next_gain, gnext_s)
        else:
            od = layer // 2
            n_main = 2 * H_C * DK_C + 2 * H_C * DV_C
            w_out = w_out_odd[od].astype(BF16)
            wg_pad = jnp.zeros((2, LANES, H_C * DK_C), F32)
            for d in range(2):
                wg_pad = wg_pad.at[d, d * GATE_RANK:(d + 1) * GATE_RANK].set(w_gate_odd[od, d])
            bgate = b_gate_odd[od].reshape(2, 1, H_C * DK_C)
            ggain = gla_norm_odd[od][None, :]

            proj_p = _inproj(h_p, wt_odd, od, 0, n_main, IN_TILE, 1024)
            glr_p = _inproj(h_p, wt_odd, od, n_main, LANES, IN_TILE, LANES)
            o_p, gla_states = _gla(proj_p, glr_p, wg_pad, bgate, ggain, None, (gla_states, n_odd),
                                   od, n_p, t_p)
            out_p = _outproj([o_p], w_out, y_p, norm_post[layer], mods, gidx_p, next_gain, gnext_p)

            proj_s = _inproj(h_s, wt_odd, od, 0, n_main, IN_TILE, 1024)
            glr_s = _inproj(h_s, wt_odd, od, n_main, LANES, IN_TILE, LANES)
            o_s = _gla(proj_s, glr_s, wg_pad, bgate, ggain, state_gla, None, od, n_s, t_s)
            out_s = _outproj([o_s], w_out, y_s, norm_post[layer], mods, gidx_s, next_gain, gnext_s)
        if next_gain is None:
            y_p, y_s = out_p, out_s
        else:
            (y_p, h_p), (y_s, h_s) = out_p, out_s

    return (y_p.reshape(n_p, t_p, D_MODEL),
            y_s.reshape(n_s, t_s, D_MODEL),
            gdn_states,
            new_k.reshape(n_p, n_even, t_p, H_B, 2 * DQK_B),
            new_v.reshape(n_p, n_even, t_p, H_B, DV_B),
            gla_states)
```

```python
import functools
import math

import jax
import jax.numpy as jnp
from jax import lax
from jax.experimental import pallas as pl
from jax.experimental.pallas import tpu as pltpu

F32 = jnp.float32
BF16 = jnp.bfloat16
HIGHEST = lax.Precision.HIGHEST

D_MODEL = 2048
DEPTH = 4
EPS = 1e-6
GRID_W = 64
H_A = 8
DK_A = 128
DV_A = 128
CONV_K = 5
CHUNK_A = 64
PREP_UNROLL = 2
GDN_MAX_HEADS = 4
H_B = 8
DQK_B = 64
DV_B = 128
ROPE_BASE = 10000.0
Q_TILE = 256
Q_TILE_CTX = 256
ATTN_HEADS = 4
ATTN_HEADS_CTX = 2
H_C = 4
DK_C = 256
DV_C = 512
GATE_RANK = 16
GLA_NORMALIZER = 16.0
CHUNK_C = 128
GLA_MAX_HEADS = 2

LOG2E = math.log2(math.e)
LANES = 128
VMEM_LIMIT = 48 * 1024 * 1024
ROW_TILE = 512
IN_TILE = 1024


def _params(*semantics):
    return pltpu.CompilerParams(dimension_semantics=semantics,
                                vmem_limit_bytes=VMEM_LIMIT)


def _mm(a, b):
    return jnp.dot(a.astype(BF16), b.astype(BF16), preferred_element_type=F32)


def _mm_nt(a, b):
    return lax.dot_general(a.astype(BF16), b.astype(BF16),
                           (((1,), (1,)), ((), ())), preferred_element_type=F32)


def _dot_nt(a, b):
    return lax.dot_general(a, b, (((1,), (1,)), ((), ())), preferred_element_type=F32)


def _mm_tn(a, b):
    return lax.dot_general(a.astype(BF16), b.astype(BF16),
                           (((0,), (0,)), ((), ())), preferred_element_type=F32)


def _mm_hi(a, b):
    return jnp.dot(a, b, precision=HIGHEST, preferred_element_type=F32)


def _mm_tn_hi(a, b):
    return lax.dot_general(a, b, (((0,), (0,)), ((), ())), precision=HIGHEST,
                           preferred_element_type=F32)


def _aligned(start, size, align):
    if isinstance(start, int):
        return pl.ds(start, size)
    return pl.ds(pl.multiple_of(start, align), size)


def _store_state(sf_ref, fill_slot, d, hd, state):
    if fill_slot is None:
        sf_ref[d, hd] = state
    else:
        for s in range(sf_ref.shape[0]):
            sf_ref[s, d, hd] = state if s == fill_slot else jnp.zeros_like(state)


def _silu(x):
    return x * jax.nn.sigmoid(x)


def _softplus(x):
    return jnp.maximum(x, 0.0) + jnp.log1p(jnp.exp(-jnp.abs(x)))


def _rms(x):
    return x * lax.rsqrt(jnp.mean(x * x, axis=-1, keepdims=True) + EPS)


def _adaln_kernel(c_ref, w_ref, b_ref, o_ref):
    o_ref[...] = _mm(_silu(c_ref[...]), w_ref[...]) + b_ref[...]


def _adaln(cvec, w_ada, b_ada):
    tn = 1024
    n3 = 3 * D_MODEL
    return pl.pallas_call(
        _adaln_kernel,
        grid=(DEPTH, n3 // tn),
        in_specs=[pl.BlockSpec((8, D_MODEL), lambda l, j: (0, 0)),
                  pl.BlockSpec((None, D_MODEL, tn), lambda l, j: (l, 0, j)),
                  pl.BlockSpec((None, 1, tn), lambda l, j: (l, 0, j))],
        out_specs=pl.BlockSpec((None, 8, tn), lambda l, j: (l, 0, j)),
        out_shape=jax.ShapeDtypeStruct((DEPTH, 8, n3), F32),
        compiler_params=_params("parallel", "parallel"),
        name="adaln",
    )(cvec, w_ada, b_ada.reshape(DEPTH, 1, n3))


def _prenorm_kernel(x_ref, g_ref, m_ref, h_ref):
    y = _rms(x_ref[...]) * g_ref[...]
    m = m_ref[...]
    h_ref[...] = (y * (1.0 + m[:, D_MODEL:2 * D_MODEL]) + m[:, :D_MODEL]).astype(BF16)


def _group_index(layer, group0, tiles_per_group):
    if tiles_per_group is None:
        return lambda i: (layer * 8 + group0, 0, 0)
    return lambda i: (layer * 8 + group0 + i // tiles_per_group, 0, 0)


def _prenorm(y, gain, mods, gidx):
    m = y.shape[0]
    return pl.pallas_call(
        _prenorm_kernel,
        grid=(m // ROW_TILE,),
        in_specs=[pl.BlockSpec((ROW_TILE, D_MODEL), lambda i: (i, 0)),
                  pl.BlockSpec((1, D_MODEL), lambda i: (0, 0)),
                  pl.BlockSpec((None, 1, 3 * D_MODEL), gidx)],
        out_specs=pl.BlockSpec((ROW_TILE, D_MODEL), lambda i: (i, 0)),
        out_shape=jax.ShapeDtypeStruct((m, D_MODEL), BF16),
        compiler_params=_params("parallel"),
        name="prenorm",
    )(y, gain.reshape(1, D_MODEL), mods)


def _inproj_kernel(*refs, shift, valid, fill_slot):
    x_ref, wa_ref = refs[:2]
    wb_ref = refs[2] if shift else None
    o_ref, wbf = refs[-2:]

    @pl.when(pl.program_id(1) == 0)
    def _():
        w = wa_ref[...]
        if shift:
            w = jnp.concatenate([w[shift:], wb_ref[...]], axis=0)
        if valid is not None:
            w = jnp.where(lax.broadcasted_iota(jnp.int32, w.shape, 0) < valid, w, 0.0)
        wbf[...] = w.astype(BF16)

    res = lax.dot_general(x_ref[...], wbf[...], (((1,), (1,)), ((), ())), preferred_element_type=F32)
    if fill_slot is None:
        o_ref[...] = res.reshape(o_ref.shape)
    else:
        for s in range(o_ref.shape[1]):
            blk = (o_ref.shape[0],) + o_ref.shape[2:]
            o_ref[:, s] = res.reshape(blk) if s == fill_slot else jnp.zeros(blk, F32)


def _inproj(x, wt3, layer_idx, row0, n_out, tm, tn, shift=0, cache_slot=None):
    m, k = x.shape
    n_rows = wt3.shape[1]
    valid = n_rows - row0 if row0 + n_out > n_rows else None
    assert valid is None or (n_out == tn and not shift)
    in_specs = [pl.BlockSpec((tm, k), lambda j, i: (i, 0)),
                pl.BlockSpec((None, tn, k), lambda j, i: (layer_idx, row0 // tn + j, 0))]
    args = [x, wt3]
    if shift:
        in_specs.append(pl.BlockSpec((None, shift, k),
                                     lambda j, i: (layer_idx, (row0 + (j + 1) * tn) // shift, 0)))
        args.append(wt3)
    aliases = {}
    fill_slot = None
    if cache_slot is None:
        out_spec = pl.BlockSpec((tm, tn), lambda j, i: (i, j))
        out_shape = jax.ShapeDtypeStruct((m, n_out), F32)
    else:
        prev, slot, t = cache_slot
        n_slots = wt3.shape[0]
        out_shape = jax.ShapeDtypeStruct((m // t, n_slots, t, n_out), F32)
        if prev is None:
            fill_slot = slot
            out_spec = pl.BlockSpec((tm // t, n_slots, t, tn), lambda j, i: (i, 0, 0, 0))
        else:
            out_spec = pl.BlockSpec((tm // t, None, t, tn), lambda j, i: (i, slot, 0, 0))
            in_specs.append(pl.BlockSpec(memory_space=pl.ANY))
            args.append(prev)
            aliases = {len(args) - 1: 0}

    return pl.pallas_call(
        functools.partial(_inproj_kernel, shift=shift, valid=valid, fill_slot=fill_slot),
        grid=(n_out // tn, m // tm),
        in_specs=in_specs,
        out_specs=out_spec,
        out_shape=out_shape,
        scratch_shapes=[pltpu.VMEM((tn, k), BF16)],
        input_output_aliases=aliases,
        compiler_params=_params("parallel", "arbitrary"),
        name="inproj",
    )(*args)


def _outproj_kernel(*refs, n_in, emit_h):
    x_refs = refs[:n_in]
    w_ref, y_ref, g_ref, m_ref = refs[n_in:n_in + 4]
    acc = None
    off = 0
    for x_ref in x_refs:
        kk = x_ref.shape[1]
        part = jnp.dot(x_ref[...], w_ref[off:off + kk, :], preferred_element_type=F32)
        acc = part if acc is None else acc + part
        off += kk
    m = m_ref[...]
    y = y_ref[...] + m[:, 2 * D_MODEL:] * (_rms(acc) * g_ref[...])
    if emit_h:
        gn_ref, mn_ref, o_ref, h_ref = refs[n_in + 4:]
        mn = mn_ref[...]
        h_ref[...] = (_rms(y) * gn_ref[...] * (1.0 + mn[:, D_MODEL:2 * D_MODEL]) + mn[:, :D_MODEL]).astype(BF16)
    else:
        o_ref = refs[n_in + 4]
    o_ref[...] = y


def _outproj(xs, w, y, gain, mods, gidx, next_gain=None, gidx_next=None):
    m = y.shape[0]
    emit_h = next_gain is not None
    row_spec = pl.BlockSpec((ROW_TILE, D_MODEL), lambda i: (i, 0))
    vec_spec = pl.BlockSpec((1, D_MODEL), lambda i: (0, 0))
    in_specs = [pl.BlockSpec((ROW_TILE, x.shape[1]), lambda i: (i, 0)) for x in xs]
    in_specs += [pl.BlockSpec(w.shape, lambda i: (0, 0)), row_spec, vec_spec,
                 pl.BlockSpec((None, 1, 3 * D_MODEL), gidx)]
    args = [*xs, w, y, gain.reshape(1, D_MODEL), mods]
    out_specs = row_spec
    out_shape = jax.ShapeDtypeStruct((m, D_MODEL), F32)
    if emit_h:
        in_specs += [vec_spec, pl.BlockSpec((None, 1, 3 * D_MODEL), gidx_next)]
        args += [next_gain.reshape(1, D_MODEL), mods]
        out_specs = (row_spec, row_spec)
        out_shape = (out_shape, jax.ShapeDtypeStruct((m, D_MODEL), BF16))
    return pl.pallas_call(
        functools.partial(_outproj_kernel, n_in=len(xs), emit_h=emit_h),
        grid=(m // ROW_TILE,),
        in_specs=in_specs,
        out_specs=out_specs,
        out_shape=out_shape,
        compiler_params=_params("parallel"),
        name="outproj",
    )(*args)


def _gdn_kernel(*refs, seq_len, heads, has_state, fill_slot):
    t = seq_len
    c = CHUNK_A
    nc = t // c
    (q_ref, k_ref, v_ref, z_ref, gp_ref, cwq_ref, cwk_ref, cwv_ref,
     alog_ref, dtb_ref, gn_ref) = refs[:11]
    qn, kn, vn, bt, gt, wv_s, wkqd_s, qk_s, kd_s, gl_s, oacc, s_s, xpad = refs[-13:]
    if has_state:
        s0_ref, o_ref = refs[11], refs[-14]
    else:
        o_ref, sf_ref = refs[-15:-13]
    head0 = pl.program_id(1) * heads

    def lanes_of(hd):
        return slice(hd * LANES, (hd + 1) * LANES)

    row_t = lax.broadcasted_iota(jnp.int32, (t, LANES), 0)
    xpad[0:8, :] = jnp.zeros((8, LANES), F32)
    xpad[8 + t:, :] = jnp.zeros((8, LANES), F32)

    def conv_silu(x_ref, w_ref, hd):
        w = w_ref[:, lanes_of(hd)]
        xpad[8:8 + t, :] = x_ref[:, lanes_of(hd)]
        acc = None
        for tap in range(CONV_K):
            start = 8 + tap - CONV_K // 2
            term = xpad[start:start + t, :] * w[tap:tap + 1, :]
            acc = term if acc is None else acc + term
        return _silu(acc)

    def l2n(x):
        return x * lax.rsqrt(jnp.sum(x * x, axis=-1, keepdims=True) + EPS)

    for hd in range(heads):
        qn[hd] = l2n(conv_silu(q_ref, cwq_ref, hd))
        kn[hd] = l2n(conv_silu(k_ref, cwk_ref, hd))
        vn[hd] = conv_silu(v_ref, cwv_ref, hd)

    gp = gp_ref[...]
    lane_t = lax.broadcasted_iota(jnp.int32, (t, LANES), 1)
    beta_all = jax.nn.sigmoid(gp)
    g_all = -jnp.exp(alog_ref[...]) * _softplus(gp + dtb_ref[...])

    def pick(x, idx):
        col = jnp.sum(jnp.where(lane_t == idx, x, 0.0), axis=-1, keepdims=True)
        return jnp.broadcast_to(col, (t, LANES))

    pos_t = row_t & (c - 1)
    g_cum = [g_all, g_all]
    sh = 1
    while sh < c:
        g_cum[0] = g_cum[0] + jnp.where(pos_t >= sh, pltpu.roll(g_cum[0], sh, axis=0), 0.0)
        g_cum[1] = g_cum[1] + jnp.where(pos_t < c - sh, pltpu.roll(g_cum[1], t - sh, axis=0), 0.0)
        sh *= 2

    for hd in range(heads):
        for d in range(2):
            bt[hd * 2 + d] = pick(beta_all, d * H_A + head0 + hd)
            gt[hd * 2 + d] = pick(g_cum[d], 2 * H_A + d * H_A + head0 + hd)

    r2 = 2 * c
    row = lax.broadcasted_iota(jnp.int32, (r2, r2), 0)
    col = lax.broadcasted_iota(jnp.int32, (r2, r2), 1)
    rc_xor = row ^ col

    tri_inc, tri_str = [], []
    for rev in (False, True):
        after = (col - row) if rev else (row - col)
        after = jnp.where(rc_xor < c, after, -1)
        tri_inc.append(after >= 0)
        tri_str.append(after > 0)
    eye = jnp.where(row == col, 1.0, 0.0)

    def prep(chains):
        n = range(len(chains))
        rows = [_aligned(gi * r2, r2, r2) for _, gi, _ in chains]
        dirs = [d for _, _, d in chains]
        hdir = [hd * 2 + d for hd, _, d in chains]
        q = [qn[chains[i][0], rows[i], :] * DK_A ** -0.5 for i in n]
        k = [kn[chains[i][0], rows[i], :] for i in n]
        beta = [bt[hdir[i], rows[i], :] for i in n]
        kb = [k[i] * beta[i] for i in n]
        kk = [_mm_nt(kb[i], k[i]) for i in n]
        qk = [_mm_nt(q[i], k[i]) for i in n]
        gcum = [gt[hdir[i], rows[i], :] for i in n]
        decay = [jnp.where(tri_inc[dirs[i]], jnp.exp(jnp.minimum(gcum[i] - gcum[i].T, 0.0)), 0.0) for i in n]
        amat = [jnp.where(tri_str[dirs[i]], kk[i] * decay[i], 0.0) for i in n]
        tinv = [eye - jnp.where(rc_xor < 2, amat[i], 0.0) for i in n]
        b = 2
        while b < c:
            level = (rc_xor >= b) & (rc_xor < 2 * b)
            m1 = [_mm(jnp.where(level, amat[i], 0.0), tinv[i]) for i in n]
            m2 = [_mm(tinv[i], m1[i]) for i in n]
            tinv = [tinv[i] - m2[i] for i in n]
            b *= 2
        eg = [jnp.exp(gcum[i]) for i in n]
        x = [_mm(tinv[i], jnp.concatenate([vn[chains[i][0], rows[i], :] * beta[i], kb[i] * eg[i]], axis=1))
             for i in n]
        for i in n:
            _, gi, d = chains[i]
            hd2 = hdir[i]
            qkd = qk[i] * decay[i]
            g_end = _block_row_bcast(gcum[i], c, 0 if d == 1 else c - 1)
            qd = q[i] * eg[i]
            gl = jnp.exp(g_end)
            wv_s[hd2, rows[i], :] = x[i][:, :DV_A]
            qk_s[hd2, rows[i], :] = jnp.where(row < c, qkd, pltpu.roll(qkd, c, axis=1))[:, :c]
            kd_s[hd2, rows[i], :] = k[i] * jnp.exp(g_end - gcum[i])
            for j in range(2):
                base = gi * 2 * r2 + j * r2
                wkqd_s[hd2, _aligned(base, c, c), :] = x[i][j * c:(j + 1) * c, DV_A:]
                wkqd_s[hd2, _aligned(base + c, c, c), :] = qd[j * c:(j + 1) * c, :]
                gl_s[hd2, _aligned(gi * 16 + j * 8, 8, 8), :] = gl[j * c:j * c + 8, :]

    def rec(chains):
        n = range(len(chains))
        rows = [_aligned(ci * c, c, c) for _, ci, _ in chains]
        hdir = [hd * 2 + d for hd, _, d in chains]
        s = [s_s[hdir[i]] for i in n]
        ws = [_mm(wkqd_s[hdir[i], _aligned(chains[i][1] * 2 * c, 2 * c, 2 * c), :], s[i]) for i in n]
        u = [wv_s[hdir[i], rows[i], :] - ws[i][:c] for i in n]
        o2 = [_mm(qk_s[hdir[i], rows[i], :], u[i]) for i in n]
        ds = [_mm_tn(kd_s[hdir[i], rows[i], :], u[i]) for i in n]
        for i in n:
            s_s[hdir[i]] = s[i] * gl_s[hdir[i], _aligned(chains[i][1] * 8, 1, 8), :] + ds[i]
        for i in n:
            oacc[chains[i][0], rows[i], :] += ws[i][c:] + o2[i]

    for hd in range(heads):
        for d in range(2):
            if has_state:
                s_s[hd * 2 + d] = s0_ref[d, hd]
            else:
                s_s[hd * 2 + d] = jnp.zeros((DK_A, DV_A), F32)
    oacc[...] = jnp.zeros((heads, t, DV_A), F32)

    def prep_body(gi, carry):
        prep([(hd, gi * PREP_UNROLL + j, d)
              for hd in range(heads) for j in range(PREP_UNROLL) for d in range(2)])
        return carry

    if nc // 2 == PREP_UNROLL:
        prep_body(0, 0)
    else:
        lax.fori_loop(0, nc // 2 // PREP_UNROLL, prep_body, 0)

    def rec_body(i, carry):
        rec([(hd, i if d == 0 else nc - 1 - i, d) for hd in range(heads) for d in range(2)])
        return carry

    lax.fori_loop(0, nc, rec_body, 0)

    for hd in range(heads):
        o_ref[:, lanes_of(hd)] = (_rms(oacc[hd]) * gn_ref[...] * _silu(z_ref[:, lanes_of(hd)])).astype(BF16)
        if not has_state:
            for d in range(2):
                _store_state(sf_ref, fill_slot, d, hd, s_s[hd * 2 + d])


def _gdn_heads(t):
    for g in (GDN_MAX_HEADS, 2, 1):
        scratch = g * t * LANES * 4 * 17 + 2 * g * DK_A * DV_A * 4
        blocks = 2 * (4 * t * g * LANES * 4 + t * LANES * 4 + t * g * LANES * 2)
        if scratch + blocks <= VMEM_LIMIT - (8 << 20):
            return g
    return 1


def _gdn(proj, gp, conv_w, alog_row, dtb_row, gnorm, state, state_out, layer_e, nseq, seq_len):
    t = seq_len
    nc = t // CHUNK_A
    has_state = state is not None
    aliases = {}
    fill_slot = None

    g = _gdn_heads(t)
    w = g * LANES
    nb = H_A // g

    def col(seg):
        return pl.BlockSpec((t, w), lambda b, h: (b, seg * nb + h))

    def cw(seg):
        return pl.BlockSpec((CONV_K, w), lambda b, h: (0, seg * nb + h))

    row_spec = pl.BlockSpec((1, LANES), lambda b, h: (0, 0))
    in_specs = [col(0), col(1), col(2), col(3),
                pl.BlockSpec((t, LANES), lambda b, h: (b, 0)),
                cw(0), cw(1), cw(2), row_spec, row_spec, row_spec]
    args = [proj, proj, proj, proj, gp, conv_w, conv_w, conv_w, alog_row, dtb_row, gnorm]
    o_shape = jax.ShapeDtypeStruct((nseq * t, H_A * DV_A), BF16)
    o_spec = pl.BlockSpec((t, w), lambda b, h: (b, h))
    if has_state:
        in_specs.append(pl.BlockSpec((None, None, 2, g, DK_A, DV_A),
                                     lambda b, h: (b, layer_e, 0, h, 0, 0)))
        args.append(state)
        out_shape, out_specs = o_shape, o_spec
    else:
        prev, n_slots = state_out
        out_shape = (o_shape, jax.ShapeDtypeStruct((nseq, n_slots, 2, H_A, DK_A, DV_A), F32))
        if prev is None:
            fill_slot = layer_e
            s_spec = pl.BlockSpec((None, n_slots, 2, g, DK_A, DV_A), lambda b, h: (b, 0, 0, h, 0, 0))
        else:
            s_spec = pl.BlockSpec((None, None, 2, g, DK_A, DV_A), lambda b, h: (b, layer_e, 0, h, 0, 0))
            in_specs.append(pl.BlockSpec(memory_space=pl.ANY))
            args.append(prev)
            aliases = {len(args) - 1: 1}
        out_specs = (o_spec, s_spec)
    scratch = [pltpu.VMEM((g, t, LANES), F32)] * 3
    scratch += [pltpu.VMEM((2 * g, t, LANES), F32)] * 2
    scratch += [pltpu.VMEM((2 * g, t, LANES), F32),
                pltpu.VMEM((2 * g, 2 * t, LANES), F32),
                pltpu.VMEM((2 * g, t, CHUNK_A), F32),
                pltpu.VMEM((2 * g, t, LANES), F32),
                pltpu.VMEM((2 * g, nc * 8, LANES), F32),
                pltpu.VMEM((g, t, LANES), F32),
                pltpu.VMEM((2 * g, DK_A, DV_A), F32),
                pltpu.VMEM((t + 16, LANES), F32)]
    return pl.pallas_call(
        functools.partial(_gdn_kernel, seq_len=t, heads=g, has_state=has_state, fill_slot=fill_slot),
        grid=(nseq, nb),
        in_specs=in_specs,
        out_specs=out_specs,
        out_shape=out_shape,
        scratch_shapes=scratch,
        input_output_aliases=aliases,
        compiler_params=_params("parallel", "parallel"),
        name="gdn",
    )(*args)


def _rope(x, cos, sin):
    lane = lax.broadcasted_iota(jnp.int32, x.shape, 1)
    quarter = DQK_B // 4
    rot = jnp.where((lane & quarter) == 0,
                    -pltpu.roll(x, LANES - quarter, axis=1),
                    pltpu.roll(x, quarter, axis=1))
    return x * cos + rot * sin


def _attn_kernel(*refs, seq_len, heads, lam_init, has_ctx):
    t = seq_len
    q_ref, k_ref, v_ref, z_ref, lam_ref, dg_ref = refs[:6]
    if has_ctx:
        cq_ref, sq_ref, ck_ref, sk_ref, kc_ref, vc_ref, o_ref, k_all, v_all = refs[6:]
    else:
        o_ref = refs[6]

    def lanes_of(hd):
        return slice(hd * LANES, (hd + 1) * LANES)

    if has_ctx:
        @pl.when(pl.program_id(2) == 0)
        def _():
            for hd in range(heads):
                k_all[hd, 0:t, :] = _rope(k_ref[:, lanes_of(hd)], ck_ref[...], sk_ref[...])
                k_all[hd, t:, :] = kc_ref[:, lanes_of(hd)]
                v_all[hd, 0:t, :] = v_ref[:, lanes_of(hd)]
                v_all[hd, t:, :] = vc_ref[:, lanes_of(hd)]
    lp = lam_ref[...]
    lam = (jnp.exp(jnp.sum(lp[0:1] * lp[1:2], axis=-1, keepdims=True))
           - jnp.exp(jnp.sum(lp[2:3] * lp[3:4], axis=-1, keepdims=True)) + lam_init)
    lane = lax.broadcasted_iota(jnp.int32, (q_ref.shape[0], LANES), 1)

    chains = [(hd, m) for hd in range(heads) for m in range(2)]
    n = range(len(chains))
    q = []
    for hd in range(heads):
        qh = q_ref[:, lanes_of(hd)]
        if has_ctx:
            qh = _rope(qh, cq_ref[...], sq_ref[...])
        q.append(qh * DQK_B ** -0.5)
    k = [k_all[hd] if has_ctx else k_ref[:, lanes_of(hd)] for hd in range(heads)]
    v = [v_all[hd] if has_ctx else v_ref[:, lanes_of(hd)] for hd in range(heads)]
    qm = [jnp.where(lane < DQK_B, q[hd], 0.0) if m == 0 else jnp.where(lane < DQK_B, 0.0, q[hd])
          for hd, m in chains]
    s = [_mm_nt(qm[i], k[chains[i][0]]) for i in n]
    e = [jnp.exp(s[i] - jnp.max(s[i], axis=-1, keepdims=True)) for i in n]
    den = [jnp.sum(e[i], axis=-1, keepdims=True) for i in n]
    ev = [_mm(e[i], v[chains[i][0]]) for i in n]
    for hd in range(heads):
        o = ev[2 * hd] / den[2 * hd] - lam * (ev[2 * hd + 1] / den[2 * hd + 1])
        o_ref[:, lanes_of(hd)] = (_rms(o) * dg_ref[...] * (1.0 - lam_init)
                                  * _silu(z_ref[:, lanes_of(hd)])).astype(BF16)


def _attn(q, k, v, z, lam_p, dgain, lam_init, ctx, layer_e, nseq, seq_len):
    t = seq_len
    has_ctx = ctx is not None
    g = ATTN_HEADS_CTX if has_ctx else ATTN_HEADS
    tq = min(t, Q_TILE_CTX if has_ctx else Q_TILE)
    nq = t // tq
    w = g * LANES
    nb = H_B // g

    def rows_spec(src, rows):
        arr, seg = src
        if arr.ndim == 4:
            assert rows == t
            return pl.BlockSpec((None, None, t, w), lambda b, h, i: (b, seg, 0, h))
        if rows == t:
            return pl.BlockSpec((t, w), lambda b, h, i: (b, seg * nb + h))
        return pl.BlockSpec((rows, w), lambda b, h, i: (b * nq + i, seg * nb + h))

    in_specs = [rows_spec(q, tq), rows_spec(k, t), rows_spec(v, t), rows_spec(z, tq),
                pl.BlockSpec((4, DQK_B), lambda b, h, i: (0, 0)),
                pl.BlockSpec((1, LANES), lambda b, h, i: (0, 0))]
    args = [q[0], k[0], v[0], z[0], lam_p, dgain]
    scratch = []
    if has_ctx:
        cos, sin, cache_k, cache_v = ctx
        past = cache_k.shape[2]
        in_specs += [pl.BlockSpec((tq, LANES), lambda b, h, i: (i, 0)),
                     pl.BlockSpec((tq, LANES), lambda b, h, i: (i, 0)),
                     pl.BlockSpec((t, LANES), lambda b, h, i: (0, 0)),
                     pl.BlockSpec((t, LANES), lambda b, h, i: (0, 0)),
                     pl.BlockSpec((None, None, past, w), lambda b, h, i: (b, layer_e, 0, h)),
                     pl.BlockSpec((None, None, past, w), lambda b, h, i: (b, layer_e, 0, h))]
        args += [cos, sin, cos, sin, cache_k.reshape(cache_k.shape[:3] + (-1,)),
                 cache_v.reshape(cache_v.shape[:3] + (-1,))]
        scratch = [pltpu.VMEM((g, t + past, LANES), F32)] * 2
    return pl.pallas_call(
        functools.partial(_attn_kernel, seq_len=t, heads=g, lam_init=lam_init, has_ctx=has_ctx),
        grid=(nseq, nb, nq),
        in_specs=in_specs,
        out_specs=pl.BlockSpec((tq, w), lambda b, h, i: (b * nq + i, h)),
        out_shape=jax.ShapeDtypeStruct((nseq * t, H_B * DV_B), BF16),
        scratch_shapes=scratch,
        compiler_params=_params("parallel", "parallel", "arbitrary"),
        name="diffattn",
    )(*args)


def _block_row_bcast(x, blk, r):
    n, w = x.shape
    if blk >= 8:
        x3 = x.reshape(n // blk, blk, w)
        return jnp.broadcast_to(x3[:, r:r + 1, :], x3.shape).reshape(n, w)
    x3 = x.reshape(n // 8, 8, w)
    sub = lax.broadcasted_iota(jnp.int32, x3.shape, 1)
    out = None
    for b in range(8 // blk):
        cand = jnp.broadcast_to(x3[:, b * blk + r:b * blk + r + 1, :], x3.shape)
        out = cand if out is None else jnp.where(sub >= b * blk, cand, out)
    return out.reshape(n, w)


def _gla_kernel(*refs, seq_len, heads, has_state, fill_slot):
    t = seq_len
    c = CHUNK_C
    nc = t // c
    q_ref, k_ref, v_ref, z_ref, glr_ref, wg_ref, bg_ref, gg_ref = refs[:8]
    la_s, oacc, st_s = refs[-3:]
    if has_state:
        s0_ref, o_ref = refs[8], refs[-4]
    else:
        o_ref, sf_ref = refs[-5:-3]

    def kcols(hd):
        return slice(hd * DK_C, (hd + 1) * DK_C)

    def vcols(hd):
        return slice(hd * DV_C, (hd + 1) * DV_C)

    glr = glr_ref[...]
    for hd in range(heads):
        for d in range(2):
            gate = _mm(glr, wg_ref[d, :, kcols(hd)]) + bg_ref[d, :, kcols(hd)]
            la_s[hd * 2 + d] = ((jnp.minimum(gate, 0.0) - jnp.log1p(jnp.exp(-jnp.abs(gate))))
                                * (LOG2E / GLA_NORMALIZER))

    row = lax.broadcasted_iota(jnp.int32, (c, c), 0)
    col = lax.broadcasted_iota(jnp.int32, (c, c), 1)
    top = row ^ col
    sh = 1
    while sh < c:
        top = top | (top >> sh)
        sh *= 2
    top = (top + 1) >> 1
    pair_code = jnp.where(row > col, top, -top)
    tri = [jnp.where(row >= col, 1.0, 0.0).astype(BF16), jnp.where(row <= col, 1.0, 0.0).astype(BF16)]

    def chunks(chains):
        n = range(len(chains))
        rows = [_aligned(ci * c, c, c) for _, ci, _ in chains]
        rev = [d == 1 for _, _, d in chains]
        hdir = [hd * 2 + d for hd, _, d in chains]
        q = [q_ref[rows[i], kcols(chains[i][0])] * DK_C ** -0.5 for i in n]
        k = [k_ref[rows[i], kcols(chains[i][0])] for i in n]
        v = [v_ref[rows[i], vcols(chains[i][0])] for i in n]
        qb = [q[i].astype(BF16) for i in n]
        kb = [k[i].astype(BF16) for i in n]
        attn = [jnp.where(row == col, _dot_nt(qb[i], kb[i]), 0.0) for i in n]
        la = [la_s[hdir[i], rows[i], :] for i in n]
        la_hi = [la[i].astype(BF16) for i in n]
        la_lo = [(la[i] - la_hi[i].astype(F32)).astype(BF16) for i in n]
        bc = [jnp.dot(tri[chains[i][2]], la_hi[i], preferred_element_type=F32)
              + jnp.dot(tri[chains[i][2]], la_lo[i], preferred_element_type=F32) for i in n]
        s = c // 2
        while s >= 1:
            for i in n:
                bref = _block_row_bcast(bc[i], 2 * s, s if rev[i] else s - 1)
                e = jnp.exp2(-jnp.abs(bc[i] - bref)).astype(BF16)
                pairs = pair_code == (-s if rev[i] else s)
                attn[i] = attn[i] + jnp.where(pairs, _dot_nt(qb[i] * e, kb[i] * e), 0.0)
            s //= 2
        b_end = [bc[i][0:1, :] if rev[i] else bc[i][c - 1:c, :] for i in n]
        st = [st_s[hdir[i]] for i in n]
        o = [_mm(attn[i], v[i]) + _mm_nt(q[i] * jnp.exp2(bc[i]), st[i]) for i in n]
        upd = [_mm_tn(v[i], k[i] * jnp.exp2(b_end[i] - bc[i])) for i in n]
        for i in n:
            st_s[hdir[i]] = st[i] * jnp.exp2(b_end[i]) + upd[i]
        for i in n:
            oacc[chains[i][0], rows[i], :] += o[i]

    for hd in range(heads):
        for d in range(2):
            if has_state:
                st_s[hd * 2 + d] = s0_ref[d, hd].T
            else:
                st_s[hd * 2 + d] = jnp.zeros((DV_C, DK_C), F32)
    oacc[...] = jnp.zeros((heads, t, DV_C), F32)

    def body(i, carry):
        chunks([(hd, i if d == 0 else nc - 1 - i, d) for hd in range(heads) for d in range(2)])
        return carry

    lax.fori_loop(0, nc, body, 0)

    for hd in range(heads):
        o_ref[:, vcols(hd)] = (_rms(oacc[hd]) * gg_ref[...] * _silu(z_ref[:, vcols(hd)])).astype(BF16)
        if not has_state:
            for d in range(2):
                _store_state(sf_ref, fill_slot, d, hd, st_s[hd * 2 + d].T)


def _gla_heads(t):
    for g in (GLA_MAX_HEADS, 1):
        scratch = g * (2 * t * DK_C + t * DV_C + 2 * DV_C * DK_C) * 4
        blocks = 2 * (g * t * (2 * DK_C + 2 * DV_C) * 4 + t * LANES * 4 + g * t * DV_C * 2
                      + 2 * g * DK_C * DV_C * 4)
        if scratch + blocks <= VMEM_LIMIT - (12 << 20):
            return g
    return 1


def _gla(proj, glr, wg_pad, bgate, ggain, state, state_out, layer_o, nseq, seq_len):
    t = seq_len
    has_state = state is not None
    aliases = {}
    fill_slot = None
    g = _gla_heads(t)
    nb = H_C // g
    wk, wv = g * DK_C, g * DV_C
    in_specs = [pl.BlockSpec((t, wk), lambda b, h: (b, h)),
                pl.BlockSpec((t, wk), lambda b, h: (b, nb + h)),
                pl.BlockSpec((t, wv), lambda b, h: (b, nb + h)),
                pl.BlockSpec((t, wv), lambda b, h: (b, 2 * nb + h)),
                pl.BlockSpec((t, LANES), lambda b, h: (b, 0)),
                pl.BlockSpec((2, LANES, wk), lambda b, h: (0, 0, h)),
                pl.BlockSpec((2, 1, wk), lambda b, h: (0, 0, h)),
                pl.BlockSpec((1, DV_C), lambda b, h: (0, 0))]
    args = [proj, proj, proj, proj, glr, wg_pad, bgate, ggain]
    o_shape = jax.ShapeDtypeStruct((nseq * t, H_C * DV_C), BF16)
    o_spec = pl.BlockSpec((t, wv), lambda b, h: (b, h))
    if has_state:
        in_specs.append(pl.BlockSpec((None, None, 2, g, DK_C, DV_C),
                                     lambda b, h: (b, layer_o, 0, h, 0, 0)))
        args.append(state)
        out_shape, out_specs = o_shape, o_spec
    else:
        prev, n_slots = state_out
        out_shape = (o_shape, jax.ShapeDtypeStruct((nseq, n_slots, 2, H_C, DK_C, DV_C), F32))
        if prev is None:
            fill_slot = layer_o
            s_spec = pl.BlockSpec((None, n_slots, 2, g, DK_C, DV_C), lambda b, h: (b, 0, 0, h, 0, 0))
        else:
            s_spec = pl.BlockSpec((None, None, 2, g, DK_C, DV_C), lambda b, h: (b, layer_o, 0, h, 0, 0))
            in_specs.append(pl.BlockSpec(memory_space=pl.ANY))
            args.append(prev)
            aliases = {len(args) - 1: 1}
        out_specs = (o_spec, s_spec)
    scratch = [pltpu.VMEM((2 * g, t, DK_C), F32),
               pltpu.VMEM((g, t, DV_C), F32),
               pltpu.VMEM((2 * g, DV_C, DK_C), F32)]
    return pl.pallas_call(
        functools.partial(_gla_kernel, seq_len=t, heads=g, has_state=has_state, fill_slot=fill_slot),
        grid=(nseq, nb),
        in_specs=in_specs,
        out_specs=out_specs,
        out_shape=out_shape,
        scratch_shapes=scratch,
        input_output_aliases=aliases,
        compiler_params=_params("parallel", "parallel"),
        name="gla",
    )(*args)


def _rope_tables(n_tokens):
    rows = n_tokens // GRID_W
    row = jnp.repeat(jnp.arange(rows), GRID_W).astype(F32)
    col = jnp.tile(jnp.arange(GRID_W), rows).astype(F32)
    quarter = DQK_B // 4
    freqs = ROPE_BASE ** (-jnp.arange(quarter, dtype=F32) / quarter)
    ar = row[:, None] * freqs
    ac = col[:, None] * freqs
    ang = jnp.concatenate([ar, ar, ac, ac] * (LANES // DQK_B), axis=-1)
    return jnp.cos(ang), jnp.sin(ang)


def _pad_lanes(x):
    return jnp.pad(x, ((0, 0), (0, LANES - x.shape[1])))


def kernel(x_prompt, x_sample, c, state_gdn, cache_k, cache_v, state_gla, c_ctx, w_ada, b_ada, norm_pre, norm_post, w_in_even, conv_even, a_log_even, dt_bias_even, gdn_norm_even, lam_even, diff_norm_even, w_out_even, w_in_odd, w_gate_odd, b_gate_odd, gla_norm_odd, w_out_odd):
    n_p, t_p, _ = x_prompt.shape
    n_s, t_s, _ = x_sample.shape
    wa = H_A * DK_A

    cvec = jnp.zeros((8, D_MODEL), F32).at[0].set(c_ctx).at[1:1 + n_s].set(c)
    mods = _adaln(cvec, w_ada, b_ada).reshape(DEPTH * 8, 1, 3 * D_MODEL)
    cos, sin = _rope_tables(t_s)

    y_p = x_prompt.reshape(n_p * t_p, D_MODEL)
    y_s = x_sample.reshape(n_s * t_s, D_MODEL)
    n_even, n_odd = w_in_even.shape[0], w_in_odd.shape[0]
    wt_even = jnp.swapaxes(w_in_even, 1, 2)
    wt_odd = jnp.swapaxes(w_in_odd, 1, 2)
    gdn_states = new_k = new_v = gla_states = None

    def gidx(layer):
        return _group_index(layer, 0, None), _group_index(layer, 1, t_s // ROW_TILE)

    gidx_p, gidx_s = gidx(0)
    h_p = _prenorm(y_p, norm_pre[0], mods, gidx_p)
    h_s = _prenorm(y_s, norm_pre[0], mods, gidx_s)

    for layer in range(DEPTH):
        gidx_p, gidx_s = gidx(layer)
        if layer + 1 < DEPTH:
            next_gain = norm_pre[layer + 1]
            gnext_p, gnext_s = gidx(layer + 1)
        else:
            next_gain = gnext_p = gnext_s = None
        if layer % 2 == 0:
            e = layer // 2
            lam_init = 0.8 - 0.6 * math.exp(-0.3 * layer)
            w_out = w_out_even[e].astype(BF16)
            zeros16 = jnp.zeros((2 * H_A,), F32)
            alog_row = _pad_lanes(jnp.concatenate([zeros16, a_log_even[e].reshape(-1)])[None, :])
            dtb_row = _pad_lanes(jnp.concatenate([zeros16, dt_bias_even[e].reshape(-1)])[None, :])
            gnorm = gdn_norm_even[e][None, :]
            dgain = diff_norm_even[e][None, :]
            col_b = 4 * wa
            shift_b = 4 * H_A

            pa_p = _inproj(h_p, wt_even, e, 0, 4 * wa, IN_TILE, 1024)
            gp_p = _inproj(h_p, wt_even, e, col_b, LANES, IN_TILE, LANES)
            qb_p = _inproj(h_p, wt_even, e, col_b, wa, IN_TILE, wa, shift_b)
            zb_p = _inproj(h_p, wt_even, e, col_b + 3 * wa, wa, IN_TILE, wa, shift_b)
            kv_tile = IN_TILE if new_k is not None else IN_TILE // n_even
            new_k = _inproj(h_p, wt_even, e, col_b + wa, wa, kv_tile, wa, shift_b, (new_k, e, t_p))
            new_v = _inproj(h_p, wt_even, e, col_b + 2 * wa, wa, kv_tile, wa, shift_b, (new_v, e, t_p))
            oa_p, gdn_states = _gdn(pa_p, gp_p, conv_even[e], alog_row, dtb_row, gnorm, None,
                                    (gdn_states, n_even), e, n_p, t_p)
            ob_p = _attn((qb_p, 0), (new_k, e), (new_v, e), (zb_p, 0), lam_even[e], dgain, lam_init,
                         None, e, n_p, t_p)
            out_p = _outproj([oa_p, ob_p], w_out, y_p, norm_post[layer], mods, gidx_p, next_gain, gnext_p)

            pa_s = _inproj(h_s, wt_even, e, 0, 4 * wa, IN_TILE, 1024)
            gp_s = _inproj(h_s, wt_even, e, col_b, LANES, IN_TILE, LANES)
            pb_s = _inproj(h_s, wt_even, e, col_b, 4 * wa, IN_TILE, 1024, shift_b)
            oa_s = _gdn(pa_s, gp_s, conv_even[e], alog_row, dtb_row, gnorm, state_gdn, None, e, n_s, t_s)
            ob_s = _attn((pb_s, 0), (pb_s, 1), (pb_s, 2), (pb_s, 3), lam_even[e], dgain, lam_init,
                         (cos, sin, cache_k, cache_v), e, n_s, t_s)
            out_s = _outproj([oa_s, ob_s], w_out, y_s, norm_post[layer], mods, gidx_s, next_gain, gnext_s)
        else:
            od = layer // 2
            n_main = 2 * H_C * DK_C + 2 * H_C * DV_C
            w_out = w_out_odd[od].astype(BF16)
            wg_pad = jnp.zeros((2, LANES, H_C * DK_C), F32)
            for d in range(2):
                wg_pad = wg_pad.at[d, d * GATE_RANK:(d + 1) * GATE_RANK].set(w_gate_odd[od, d])
            bgate = b_gate_odd[od].reshape(2, 1, H_C * DK_C)
            ggain = gla_norm_odd[od][None, :]

            proj_p = _inproj(h_p, wt_odd, od, 0, n_main, IN_TILE, 1024)
            glr_p = _inproj(h_p, wt_odd, od, n_main, LANES, IN_TILE, LANES)
            o_p, gla_states = _gla(proj_p, glr_p, wg_pad, bgate, ggain, None, (gla_states, n_odd),
                                   od, n_p, t_p)
            out_p = _outproj([o_p], w_out, y_p, norm_post[layer], mods, gidx_p, next_gain, gnext_p)

            proj_s = _inproj(h_s, wt_odd, od, 0, n_main, IN_TILE, 1024)
            glr_s = _inproj(h_s, wt_odd, od, n_main, LANES, IN_TILE, LANES)
            o_s = _gla(proj_s, glr_s, wg_pad, bgate, ggain, state_gla, None, od, n_s, t_s)
            out_s = _outproj([o_s], w_out, y_s, norm_post[layer], mods, gidx_s, next_gain, gnext_s)
        if next_gain is None:
            y_p, y_s = out_p, out_s
        else:
            (y_p, h_p), (y_s, h_s) = out_p, out_s

    return (y_p.reshape(n_p, t_p, D_MODEL),
            y_s.reshape(n_s, t_s, D_MODEL),
            gdn_states,
            new_k.reshape(n_p, n_even, t_p, H_B, 2 * DQK_B),
            new_v.reshape(n_p, n_even, t_p, H_B, DV_B),
            gla_states)
```

```python
import functools
import math

import jax
import jax.numpy as jnp
from jax import lax
from jax.experimental import pallas as pl
from jax.experimental.pallas import tpu as pltpu

F32 = jnp.float32
BF16 = jnp.bfloat16
HIGHEST = lax.Precision.HIGHEST

D_MODEL = 2048
DEPTH = 4
EPS = 1e-6
GRID_W = 64
H_A = 8
DK_A = 128
DV_A = 128
CONV_K = 5
CHUNK_A = 64
PREP_CHAINS = 16
GDN_MAX_HEADS = 4
H_B = 8
DQK_B = 64
DV_B = 128
ROPE_BASE = 10000.0
Q_TILE = 256
Q_TILE_CTX = 256
ATTN_HEADS = 4
ATTN_HEADS_CTX = 2
H_C = 4
DK_C = 256
DV_C = 512
GATE_RANK = 16
GLA_NORMALIZER = 16.0
CHUNK_C = 128
GLA_MAX_HEADS = 2

LOG2E = math.log2(math.e)
LANES = 128
VMEM_LIMIT = 48 * 1024 * 1024
VMEM_MARGIN = 4 * 1024 * 1024
ROW_TILE = 512
OUT_SUBTILES = 4
IN_TILE = 1024


def _params(*semantics, vmem=VMEM_LIMIT):
    return pltpu.CompilerParams(dimension_semantics=semantics, vmem_limit_bytes=vmem)


def _mm(a, b):
    return jnp.dot(a.astype(BF16), b.astype(BF16), preferred_element_type=F32)


def _mm_nt(a, b):
    return lax.dot_general(a.astype(BF16), b.astype(BF16),
                           (((1,), (1,)), ((), ())), preferred_element_type=F32)


def _dot_nt(a, b):
    return lax.dot_general(a, b, (((1,), (1,)), ((), ())), preferred_element_type=F32)


def _mm_tn(a, b):
    return lax.dot_general(a.astype(BF16), b.astype(BF16),
                           (((0,), (0,)), ((), ())), preferred_element_type=F32)


def _mm_hi(a, b):
    return jnp.dot(a, b, precision=HIGHEST, preferred_element_type=F32)


def _mm_tn_hi(a, b):
    return lax.dot_general(a, b, (((0,), (0,)), ((), ())), precision=HIGHEST,
                           preferred_element_type=F32)


def _aligned(start, size, align):
    if isinstance(start, int):
        return pl.ds(start, size)
    return pl.ds(pl.multiple_of(start, align), size)


def _store_state(sf_ref, fill_slot, d, hd, state):
    if fill_slot is None:
        sf_ref[d, hd] = state
    else:
        for s in range(sf_ref.shape[0]):
            sf_ref[s, d, hd] = state if s == fill_slot else jnp.zeros_like(state)


def _silu(x):
    return x * jax.nn.sigmoid(x)


def _softplus(x):
    return jnp.maximum(x, 0.0) + jnp.log1p(jnp.exp(-jnp.abs(x)))


def _rms(x):
    return x * lax.rsqrt(jnp.mean(x * x, axis=-1, keepdims=True) + EPS)


def _adaln_kernel(c_ref, w_ref, b_ref, o_ref):
    o_ref[...] = _mm(_silu(c_ref[...]), w_ref[...]) + b_ref[...]


def _adaln(cvec, w_ada, b_ada):
    tn = 1024
    n3 = 3 * D_MODEL
    return pl.pallas_call(
        _adaln_kernel,
        grid=(DEPTH, n3 // tn),
        in_specs=[pl.BlockSpec((8, D_MODEL), lambda l, j: (0, 0)),
                  pl.BlockSpec((None, D_MODEL, tn), lambda l, j: (l, 0, j)),
                  pl.BlockSpec((None, 1, tn), lambda l, j: (l, 0, j))],
        out_specs=pl.BlockSpec((None, 8, tn), lambda l, j: (l, 0, j)),
        out_shape=jax.ShapeDtypeStruct((DEPTH, 8, n3), F32),
        compiler_params=_params("parallel", "parallel"),
        name="adaln",
    )(cvec, w_ada, b_ada.reshape(DEPTH, 1, n3))


def _prenorm_kernel(x_ref, g_ref, m_ref, h_ref):
    y = _rms(x_ref[...]) * g_ref[...]
    m = m_ref[...]
    h_ref[...] = (y * (1.0 + m[:, D_MODEL:2 * D_MODEL]) + m[:, :D_MODEL]).astype(BF16)


def _group_index(layer, group0, tiles_per_group):
    if tiles_per_group is None:
        return lambda i: (layer * 8 + group0, 0, 0)
    return lambda i: (layer * 8 + group0 + i // tiles_per_group, 0, 0)


def _prenorm(y, gain, mods, gidx):
    m = y.shape[0]
    return pl.pallas_call(
        _prenorm_kernel,
        grid=(m // ROW_TILE,),
        in_specs=[pl.BlockSpec((ROW_TILE, D_MODEL), lambda i: (i, 0)),
                  pl.BlockSpec((1, D_MODEL), lambda i: (0, 0)),
                  pl.BlockSpec((None, 1, 3 * D_MODEL), gidx)],
        out_specs=pl.BlockSpec((ROW_TILE, D_MODEL), lambda i: (i, 0)),
        out_shape=jax.ShapeDtypeStruct((m, D_MODEL), BF16),
        compiler_params=_params("parallel"),
        name="prenorm",
    )(y, gain.reshape(1, D_MODEL), mods)


def _cast_weight_block(wa_ref, wb_ref, wbf, shift, valid):
    @pl.when(pl.program_id(1) == 0)
    def _():
        w = wa_ref[...]
        if shift:
            w = jnp.concatenate([w[shift:], wb_ref[...]], axis=0)
        if valid is not None:
            w = jnp.where(lax.broadcasted_iota(jnp.int32, w.shape, 0) < valid, w, 0.0)
        wbf[...] = w.astype(BF16)


def _inproj2_kernel(*refs, shift, valid, tiles_a):
    xa_ref, xb_ref, wa_ref = refs[:3]
    wb_ref = refs[3] if shift else None
    oa_ref, ob_ref, wbf = refs[-3:]
    _cast_weight_block(wa_ref, wb_ref, wbf, shift, valid)

    @pl.when(pl.program_id(1) < tiles_a)
    def _():
        oa_ref[...] = _dot_nt(xa_ref[...], wbf[...])

    @pl.when(pl.program_id(1) >= tiles_a)
    def _():
        ob_ref[...] = _dot_nt(xb_ref[...], wbf[...])


def _inproj_kernel(*refs, shift, valid, fill_slot):
    x_ref, wa_ref = refs[:2]
    wb_ref = refs[2] if shift else None
    o_ref, wbf = refs[-2:]
    _cast_weight_block(wa_ref, wb_ref, wbf, shift, valid)
    res = _dot_nt(x_ref[...], wbf[...])
    if fill_slot is None:
        o_ref[...] = res.reshape(o_ref.shape)
    else:
        for s in range(o_ref.shape[1]):
            blk = (o_ref.shape[0],) + o_ref.shape[2:]
            o_ref[:, s] = res.reshape(blk) if s == fill_slot else jnp.zeros(blk, F32)


def _inproj(x, wt3, layer_idx, row0, n_out, tm, tn, shift=0, cache_slot=None):
    m, k = x.shape
    n_rows = wt3.shape[1]
    valid = n_rows - row0 if row0 + n_out > n_rows else None
    assert valid is None or (n_out == tn and not shift)
    in_specs = [pl.BlockSpec((tm, k), lambda j, i: (i, 0)),
                pl.BlockSpec((None, tn, k), lambda j, i: (layer_idx, row0 // tn + j, 0))]
    args = [x, wt3]
    if shift:
        in_specs.append(pl.BlockSpec((None, shift, k),
                                     lambda j, i: (layer_idx, (row0 + (j + 1) * tn) // shift, 0)))
        args.append(wt3)
    aliases = {}
    fill_slot = None
    if cache_slot is None:
        out_spec = pl.BlockSpec((tm, tn), lambda j, i: (i, j))
        out_shape = jax.ShapeDtypeStruct((m, n_out), F32)
    else:
        prev, slot, t = cache_slot
        n_slots = wt3.shape[0]
        out_shape = jax.ShapeDtypeStruct((m // t, n_slots, t, n_out), F32)
        if prev is None:
            fill_slot = slot
            out_spec = pl.BlockSpec((tm // t, n_slots, t, tn), lambda j, i: (i, 0, 0, 0))
        else:
            out_spec = pl.BlockSpec((tm // t, None, t, tn), lambda j, i: (i, slot, 0, 0))
            in_specs.append(pl.BlockSpec(memory_space=pl.ANY))
            args.append(prev)
            aliases = {len(args) - 1: 0}

    return pl.pallas_call(
        functools.partial(_inproj_kernel, shift=shift, valid=valid, fill_slot=fill_slot),
        grid=(n_out // tn, m // tm),
        in_specs=in_specs,
        out_specs=out_spec,
        out_shape=out_shape,
        scratch_shapes=[pltpu.VMEM((tn, k), BF16)],
        input_output_aliases=aliases,
        compiler_params=_params("parallel", "arbitrary"),
        name="inproj",
    )(*args)


def _inproj2(xa, xb, wt3, layer_idx, row0, n_out, tm, tn, shift=0):
    (ma, k), mb = xa.shape, xb.shape[0]
    ta, tb = ma // tm, mb // tm
    n_rows = wt3.shape[1]
    valid = n_rows - row0 if row0 + n_out > n_rows else None
    assert valid is None or (n_out == tn and not shift)

    def a_tile(i):
        return jnp.minimum(i, ta - 1)

    def b_tile(i):
        return jnp.maximum(i - ta, 0)

    in_specs = [pl.BlockSpec((tm, k), lambda j, i: (a_tile(i), 0)),
                pl.BlockSpec((tm, k), lambda j, i: (b_tile(i), 0)),
                pl.BlockSpec((None, tn, k), lambda j, i: (layer_idx, row0 // tn + j, 0))]
    args = [xa, xb, wt3]
    if shift:
        in_specs.append(pl.BlockSpec((None, shift, k),
                                     lambda j, i: (layer_idx, (row0 + (j + 1) * tn) // shift, 0)))
        args.append(wt3)
    return pl.pallas_call(
        functools.partial(_inproj2_kernel, shift=shift, valid=valid, tiles_a=ta),
        grid=(n_out // tn, ta + tb),
        in_specs=in_specs,
        out_specs=(pl.BlockSpec((tm, tn), lambda j, i: (a_tile(i), j)),
                   pl.BlockSpec((tm, tn), lambda j, i: (b_tile(i), j))),
        out_shape=(jax.ShapeDtypeStruct((ma, n_out), F32), jax.ShapeDtypeStruct((mb, n_out), F32)),
        scratch_shapes=[pltpu.VMEM((tn, k), BF16)],
        compiler_params=_params("parallel", "arbitrary",
                                vmem=4 * tm * k * 2 + 4 * tm * tn * 4 + 2 * tn * k * 4 + tn * k * 2
                                + 2 * shift * k * 4 + VMEM_MARGIN),
        name="inproj2",
    )(*args)


def _outproj_kernel(*refs, n_in, emit_h):
    x_refs = refs[:n_in]
    w_ref, y_ref, g_ref, m_ref = refs[n_in:n_in + 4]
    if emit_h:
        gn_ref, mn_ref, o_ref, h_ref = refs[n_in + 4:]
    else:
        o_ref = refs[n_in + 4]
    m = m_ref[...]
    sub = o_ref.shape[0] // OUT_SUBTILES
    accs = []
    for s in range(OUT_SUBTILES):
        rows = slice(s * sub, (s + 1) * sub)
        acc = None
        off = 0
        for x_ref in x_refs:
            kk = x_ref.shape[1]
            part = jnp.dot(x_ref[rows, :], w_ref[off:off + kk, :], preferred_element_type=F32)
            acc = part if acc is None else acc + part
            off += kk
        accs.append(acc)
    for s in range(OUT_SUBTILES):
        rows = slice(s * sub, (s + 1) * sub)
        y = y_ref[rows, :] + m[:, 2 * D_MODEL:] * (_rms(accs[s]) * g_ref[...])
        if emit_h:
            mn = mn_ref[...]
            h_ref[rows, :] = (_rms(y) * gn_ref[...] * (1.0 + mn[:, D_MODEL:2 * D_MODEL])
                              + mn[:, :D_MODEL]).astype(BF16)
        o_ref[rows, :] = y


def _outproj(xs, w, y, gain, mods, gidx, next_gain=None, gidx_next=None):
    m = y.shape[0]
    emit_h = next_gain is not None
    row_spec = pl.BlockSpec((ROW_TILE, D_MODEL), lambda i: (i, 0))
    vec_spec = pl.BlockSpec((1, D_MODEL), lambda i: (0, 0))
    in_specs = [pl.BlockSpec((ROW_TILE, x.shape[1]), lambda i: (i, 0)) for x in xs]
    in_specs += [pl.BlockSpec(w.shape, lambda i: (0, 0)), row_spec, vec_spec,
                 pl.BlockSpec((None, 1, 3 * D_MODEL), gidx)]
    args = [*xs, w, y, gain.reshape(1, D_MODEL), mods]
    out_specs = row_spec
    out_shape = jax.ShapeDtypeStruct((m, D_MODEL), F32)
    if emit_h:
        in_specs += [vec_spec, pl.BlockSpec((None, 1, 3 * D_MODEL), gidx_next)]
        args += [next_gain.reshape(1, D_MODEL), mods]
        out_specs = (row_spec, row_spec)
        out_shape = (out_shape, jax.ShapeDtypeStruct((m, D_MODEL), BF16))
    return pl.pallas_call(
        functools.partial(_outproj_kernel, n_in=len(xs), emit_h=emit_h),
        grid=(m // ROW_TILE,),
        in_specs=in_specs,
        out_specs=out_specs,
        out_shape=out_shape,
        compiler_params=_params("parallel"),
        name="outproj",
    )(*args)


def _gdn_kernel(*refs, seq_len, heads, has_state, fill_slot):
    t = seq_len
    c = CHUNK_A
    nc = t // c
    (q_ref, k_ref, v_ref, z_ref, gp_ref, cwq_ref, cwk_ref, cwv_ref,
     alog_ref, dtb_ref, gn_ref) = refs[:11]
    qn, kn, vn, bt, gt, wv_s, wkqd_s, qk_s, kd_s, gl_s, oacc, s_s, xpad = refs[-13:]
    if has_state:
        s0_ref, o_ref = refs[11], refs[-14]
    else:
        o_ref, sf_ref = refs[-15:-13]
    head0 = pl.program_id(1) * heads

    def lanes_of(hd):
        return slice(hd * LANES, (hd + 1) * LANES)

    row_t = lax.broadcasted_iota(jnp.int32, (t, LANES), 0)
    xpad[0:8, :] = jnp.zeros((8, LANES), F32)
    xpad[8 + t:, :] = jnp.zeros((8, LANES), F32)

    def conv_silu(x_ref, w_ref, hd):
        w = w_ref[:, lanes_of(hd)]
        xpad[8:8 + t, :] = x_ref[:, lanes_of(hd)]
        acc = None
        for tap in range(CONV_K):
            start = 8 + tap - CONV_K // 2
            term = xpad[start:start + t, :] * w[tap:tap + 1, :]
            acc = term if acc is None else acc + term
        return _silu(acc)

    def l2n(x):
        return x * lax.rsqrt(jnp.sum(x * x, axis=-1, keepdims=True) + EPS)

    for hd in range(heads):
        qn[hd] = l2n(conv_silu(q_ref, cwq_ref, hd))
        kn[hd] = l2n(conv_silu(k_ref, cwk_ref, hd))
        vn[hd] = conv_silu(v_ref, cwv_ref, hd)

    gp = gp_ref[...]
    lane_t = lax.broadcasted_iota(jnp.int32, (t, LANES), 1)
    beta_all = jax.nn.sigmoid(gp)
    g_all = -jnp.exp(alog_ref[...]) * _softplus(gp + dtb_ref[...])

    def pick(x, idx):
        col = jnp.sum(jnp.where(lane_t == idx, x, 0.0), axis=-1, keepdims=True)
        return jnp.broadcast_to(col, (t, LANES))

    pos_t = row_t & (c - 1)
    g_cum = [g_all, g_all]
    sh = 1
    while sh < c:
        g_cum[0] = g_cum[0] + jnp.where(pos_t >= sh, pltpu.roll(g_cum[0], sh, axis=0), 0.0)
        g_cum[1] = g_cum[1] + jnp.where(pos_t < c - sh, pltpu.roll(g_cum[1], t - sh, axis=0), 0.0)
        sh *= 2

    for hd in range(heads):
        for d in range(2):
            bt[hd * 2 + d] = pick(beta_all, d * H_A + head0 + hd)
            gt[hd * 2 + d] = pick(g_cum[d], 2 * H_A + d * H_A + head0 + hd)

    r2 = 2 * c
    row = lax.broadcasted_iota(jnp.int32, (r2, r2), 0)
    col = lax.broadcasted_iota(jnp.int32, (r2, r2), 1)
    rc_xor = row ^ col

    tri_inc, tri_str = [], []
    for rev in (False, True):
        after = (col - row) if rev else (row - col)
        after = jnp.where(rc_xor < c, after, -1)
        tri_inc.append(after >= 0)
        tri_str.append(after > 0)
    eye = jnp.where(row == col, 1.0, 0.0)

    def prep(chains):
        n = range(len(chains))
        rows = [_aligned(gi * r2, r2, r2) for _, gi, _ in chains]
        dirs = [d for _, _, d in chains]
        hdir = [hd * 2 + d for hd, _, d in chains]
        q = [qn[chains[i][0], rows[i], :] * DK_A ** -0.5 for i in n]
        k = [kn[chains[i][0], rows[i], :] for i in n]
        beta = [bt[hdir[i], rows[i], :] for i in n]
        kb = [k[i] * beta[i] for i in n]
        kk = [_mm_nt(kb[i], k[i]) for i in n]
        qk = [_mm_nt(q[i], k[i]) for i in n]
        gcum = [gt[hdir[i], rows[i], :] for i in n]
        decay = [jnp.where(tri_inc[dirs[i]], jnp.exp(jnp.minimum(gcum[i] - gcum[i].T, 0.0)), 0.0) for i in n]
        amat = [jnp.where(tri_str[dirs[i]], kk[i] * decay[i], 0.0) for i in n]
        tinv = [eye - jnp.where(rc_xor < 2, amat[i], 0.0) for i in n]
        b = 2
        while b < c:
            level = (rc_xor >= b) & (rc_xor < 2 * b)
            m1 = [_mm(jnp.where(level, amat[i], 0.0), tinv[i]) for i in n]
            m2 = [_mm(tinv[i], m1[i]) for i in n]
            tinv = [tinv[i] - m2[i] for i in n]
            b *= 2
        eg = [jnp.exp(gcum[i]) for i in n]
        x = [_mm(tinv[i], jnp.concatenate([vn[chains[i][0], rows[i], :] * beta[i], kb[i] * eg[i]], axis=1))
             for i in n]
        for i in n:
            _, gi, d = chains[i]
            hd2 = hdir[i]
            qkd = qk[i] * decay[i]
            g_end = _block_row_bcast(gcum[i], c, 0 if d == 1 else c - 1)
            qd = q[i] * eg[i]
            gl = jnp.exp(g_end)
            wv_s[hd2, rows[i], :] = x[i][:, :DV_A]
            qk_s[hd2, rows[i], :] = jnp.where(row < c, qkd, pltpu.roll(qkd, c, axis=1))[:, :c]
            kd_s[hd2, rows[i], :] = k[i] * jnp.exp(g_end - gcum[i])
            for j in range(2):
                base = gi * 2 * r2 + j * r2
                wkqd_s[hd2, _aligned(base, c, c), :] = x[i][j * c:(j + 1) * c, DV_A:]
                wkqd_s[hd2, _aligned(base + c, c, c), :] = qd[j * c:(j + 1) * c, :]
                gl_s[hd2, _aligned(gi * 16 + j * 8, 8, 8), :] = gl[j * c:j * c + 8, :]

    def rec(chains):
        n = range(len(chains))
        rows = [_aligned(ci * c, c, c) for _, ci, _ in chains]
        hdir = [hd * 2 + d for hd, _, d in chains]
        s = [s_s[hdir[i]] for i in n]
        ws = [_mm(wkqd_s[hdir[i], _aligned(chains[i][1] * 2 * c, 2 * c, 2 * c), :], s[i]) for i in n]
        u = [wv_s[hdir[i], rows[i], :] - ws[i][:c] for i in n]
        o2 = [_mm(qk_s[hdir[i], rows[i], :], u[i]) for i in n]
        ds = [_mm_tn(kd_s[hdir[i], rows[i], :], u[i]) for i in n]
        for i in n:
            s_s[hdir[i]] = s[i] * gl_s[hdir[i], _aligned(chains[i][1] * 8, 1, 8), :] + ds[i]
        for i in n:
            oacc[chains[i][0], rows[i], :] += ws[i][c:] + o2[i]

    for hd in range(heads):
        for d in range(2):
            if has_state:
                s_s[hd * 2 + d] = s0_ref[d, hd]
            else:
                s_s[hd * 2 + d] = jnp.zeros((DK_A, DV_A), F32)
    oacc[...] = jnp.zeros((heads, t, DV_A), F32)

    n_groups = nc // 2
    unroll = max(1, min(n_groups, PREP_CHAINS // (2 * heads)))

    def prep_body(gi, carry):
        prep([(hd, gi * unroll + j, d) for hd in range(heads) for j in range(unroll) for d in range(2)])
        return carry

    if n_groups == unroll:
        prep_body(0, 0)
    else:
        lax.fori_loop(0, n_groups // unroll, prep_body, 0)

    def rec_body(i, carry):
        rec([(hd, i if d == 0 else nc - 1 - i, d) for hd in range(heads) for d in range(2)])
        return carry

    lax.fori_loop(0, nc, rec_body, 0)

    for hd in range(heads):
        o_ref[:, lanes_of(hd)] = (_rms(oacc[hd]) * gn_ref[...] * _silu(z_ref[:, lanes_of(hd)])).astype(BF16)
        if not has_state:
            for d in range(2):
                _store_state(sf_ref, fill_slot, d, hd, s_s[hd * 2 + d])


def _gdn_heads(t):
    for g in (GDN_MAX_HEADS, 2, 1):
        scratch = g * t * LANES * 4 * 17 + 2 * g * DK_A * DV_A * 4
        blocks = 2 * (4 * t * g * LANES * 4 + t * LANES * 4 + t * g * LANES * 2)
        if scratch + blocks <= VMEM_LIMIT - (8 << 20):
            return g
    return 1


def _gdn(proj, gp, conv_w, alog_row, dtb_row, gnorm, state, state_out, layer_e, nseq, seq_len):
    t = seq_len
    nc = t // CHUNK_A
    has_state = state is not None
    aliases = {}
    fill_slot = None

    g = _gdn_heads(t)
    w = g * LANES
    nb = H_A // g

    def col(seg):
        return pl.BlockSpec((t, w), lambda b, h: (b, seg * nb + h))

    def cw(seg):
        return pl.BlockSpec((CONV_K, w), lambda b, h: (0, seg * nb + h))

    row_spec = pl.BlockSpec((1, LANES), lambda b, h: (0, 0))
    in_specs = [col(0), col(1), col(2), col(3),
                pl.BlockSpec((t, LANES), lambda b, h: (b, 0)),
                cw(0), cw(1), cw(2), row_spec, row_spec, row_spec]
    args = [proj, proj, proj, proj, gp, conv_w, conv_w, conv_w, alog_row, dtb_row, gnorm]
    o_shape = jax.ShapeDtypeStruct((nseq * t, H_A * DV_A), BF16)
    o_spec = pl.BlockSpec((t, w), lambda b, h: (b, h))
    if has_state:
        in_specs.append(pl.BlockSpec((None, None, 2, g, DK_A, DV_A),
                                     lambda b, h: (b, layer_e, 0, h, 0, 0)))
        args.append(state)
        out_shape, out_specs = o_shape, o_spec
    else:
        prev, n_slots = state_out
        out_shape = (o_shape, jax.ShapeDtypeStruct((nseq, n_slots, 2, H_A, DK_A, DV_A), F32))
        if prev is None:
            fill_slot = layer_e
            s_spec = pl.BlockSpec((None, n_slots, 2, g, DK_A, DV_A), lambda b, h: (b, 0, 0, h, 0, 0))
        else:
            s_spec = pl.BlockSpec((None, None, 2, g, DK_A, DV_A), lambda b, h: (b, layer_e, 0, h, 0, 0))
            in_specs.append(pl.BlockSpec(memory_space=pl.ANY))
            args.append(prev)
            aliases = {len(args) - 1: 1}
        out_specs = (o_spec, s_spec)
    scratch = [pltpu.VMEM((g, t, LANES), F32)] * 3
    scratch += [pltpu.VMEM((2 * g, t, LANES), F32)] * 2
    scratch += [pltpu.VMEM((2 * g, t, LANES), F32),
                pltpu.VMEM((2 * g, 2 * t, LANES), F32),
                pltpu.VMEM((2 * g, t, CHUNK_A), F32),
                pltpu.VMEM((2 * g, t, LANES), F32),
                pltpu.VMEM((2 * g, nc * 8, LANES), F32),
                pltpu.VMEM((g, t, LANES), F32),
                pltpu.VMEM((2 * g, DK_A, DV_A), F32),
                pltpu.VMEM((t + 16, LANES), F32)]
    return pl.pallas_call(
        functools.partial(_gdn_kernel, seq_len=t, heads=g, has_state=has_state, fill_slot=fill_slot),
        grid=(nseq, nb),
        in_specs=in_specs,
        out_specs=out_specs,
        out_shape=out_shape,
        scratch_shapes=scratch,
        input_output_aliases=aliases,
        compiler_params=_params("parallel", "parallel"),
        name="gdn",
    )(*args)


def _rope(x, cos, sin):
    lane = lax.broadcasted_iota(jnp.int32, x.shape, 1)
    quarter = DQK_B // 4
    rot = jnp.where((lane & quarter) == 0,
                    -pltpu.roll(x, LANES - quarter, axis=1),
                    pltpu.roll(x, quarter, axis=1))
    return x * cos + rot * sin


def _attn_kernel(*refs, seq_len, heads, lam_init, has_ctx):
    t = seq_len
    q_ref, k_ref, v_ref, z_ref, lam_ref, dg_ref = refs[:6]
    if has_ctx:
        cq_ref, sq_ref, ck_ref, sk_ref, kc_ref, vc_ref, o_ref, k_all, v_all = refs[6:]
    else:
        o_ref = refs[6]

    def lanes_of(hd):
        return slice(hd * LANES, (hd + 1) * LANES)

    if has_ctx:
        @pl.when(pl.program_id(2) == 0)
        def _():
            for hd in range(heads):
                k_all[hd, 0:t, :] = _rope(k_ref[:, lanes_of(hd)], ck_ref[...], sk_ref[...])
                k_all[hd, t:, :] = kc_ref[:, lanes_of(hd)]
                v_all[hd, 0:t, :] = v_ref[:, lanes_of(hd)]
                v_all[hd, t:, :] = vc_ref[:, lanes_of(hd)]
    lp = lam_ref[...]
    lam = (jnp.exp(jnp.sum(lp[0:1] * lp[1:2], axis=-1, keepdims=True))
           - jnp.exp(jnp.sum(lp[2:3] * lp[3:4], axis=-1, keepdims=True)) + lam_init)
    lane = lax.broadcasted_iota(jnp.int32, (q_ref.shape[0], LANES), 1)

    chains = [(hd, m) for hd in range(heads) for m in range(2)]
    n = range(len(chains))
    q = []
    for hd in range(heads):
        qh = q_ref[:, lanes_of(hd)]
        if has_ctx:
            qh = _rope(qh, cq_ref[...], sq_ref[...])
        q.append(qh * DQK_B ** -0.5)
    k = [k_all[hd] if has_ctx else k_ref[:, lanes_of(hd)] for hd in range(heads)]
    v = [v_all[hd] if has_ctx else v_ref[:, lanes_of(hd)] for hd in range(heads)]
    qm = [jnp.where(lane < DQK_B, q[hd], 0.0) if m == 0 else jnp.where(lane < DQK_B, 0.0, q[hd])
          for hd, m in chains]
    s = [_mm_nt(qm[i], k[chains[i][0]]) for i in n]
    e = [jnp.exp(s[i] - jnp.max(s[i], axis=-1, keepdims=True)) for i in n]
    den = [jnp.sum(e[i], axis=-1, keepdims=True) for i in n]
    ev = [_mm(e[i], v[chains[i][0]]) for i in n]
    for hd in range(heads):
        o = ev[2 * hd] / den[2 * hd] - lam * (ev[2 * hd + 1] / den[2 * hd + 1])
        o_ref[:, lanes_of(hd)] = (_rms(o) * dg_ref[...] * (1.0 - lam_init)
                                  * _silu(z_ref[:, lanes_of(hd)])).astype(BF16)


def _attn(q, k, v, z, lam_p, dgain, lam_init, ctx, layer_e, nseq, seq_len):
    t = seq_len
    has_ctx = ctx is not None
    g = ATTN_HEADS_CTX if has_ctx else ATTN_HEADS
    tq = min(t, Q_TILE_CTX if has_ctx else Q_TILE)
    nq = t // tq
    w = g * LANES
    nb = H_B // g

    def rows_spec(src, rows):
        arr, seg = src
        if arr.ndim == 4:
            assert rows == t
            return pl.BlockSpec((None, None, t, w), lambda b, h, i: (b, seg, 0, h))
        if rows == t:
            return pl.BlockSpec((t, w), lambda b, h, i: (b, seg * nb + h))
        return pl.BlockSpec((rows, w), lambda b, h, i: (b * nq + i, seg * nb + h))

    in_specs = [rows_spec(q, tq), rows_spec(k, t), rows_spec(v, t), rows_spec(z, tq),
                pl.BlockSpec((4, DQK_B), lambda b, h, i: (0, 0)),
                pl.BlockSpec((1, LANES), lambda b, h, i: (0, 0))]
    args = [q[0], k[0], v[0], z[0], lam_p, dgain]
    scratch = []
    if has_ctx:
        cos, sin, cache_k, cache_v = ctx
        past = cache_k.shape[2]
        in_specs += [pl.BlockSpec((tq, LANES), lambda b, h, i: (i, 0)),
                     pl.BlockSpec((tq, LANES), lambda b, h, i: (i, 0)),
                     pl.BlockSpec((t, LANES), lambda b, h, i: (0, 0)),
                     pl.BlockSpec((t, LANES), lambda b, h, i: (0, 0)),
                     pl.BlockSpec((None, None, past, w), lambda b, h, i: (b, layer_e, 0, h)),
                     pl.BlockSpec((None, None, past, w), lambda b, h, i: (b, layer_e, 0, h))]
        args += [cos, sin, cos, sin, cache_k.reshape(cache_k.shape[:3] + (-1,)),
                 cache_v.reshape(cache_v.shape[:3] + (-1,))]
        scratch = [pltpu.VMEM((g, t + past, LANES), F32)] * 2
    return pl.pallas_call(
        functools.partial(_attn_kernel, seq_len=t, heads=g, lam_init=lam_init, has_ctx=has_ctx),
        grid=(nseq, nb, nq),
        in_specs=in_specs,
        out_specs=pl.BlockSpec((tq, w), lambda b, h, i: (b * nq + i, h)),
        out_shape=jax.ShapeDtypeStruct((nseq * t, H_B * DV_B), BF16),
        scratch_shapes=scratch,
        compiler_params=_params("parallel", "parallel", "arbitrary"),
        name="diffattn",
    )(*args)


def _block_row_bcast(x, blk, r):
    n, w = x.shape
    if blk >= 8:
        x3 = x.reshape(n // blk, blk, w)
        return jnp.broadcast_to(x3[:, r:r + 1, :], x3.shape).reshape(n, w)
    x3 = x.reshape(n // 8, 8, w)
    sub = lax.broadcasted_iota(jnp.int32, x3.shape, 1)
    out = None
    for b in range(8 // blk):
        cand = jnp.broadcast_to(x3[:, b * blk + r:b * blk + r + 1, :], x3.shape)
        out = cand if out is None else jnp.where(sub >= b * blk, cand, out)
    return out.reshape(n, w)


def _gla_kernel(*refs, seq_len, heads, has_state, fill_slot):
    t = seq_len
    c = CHUNK_C
    nc = t // c
    q_ref, k_ref, v_ref, z_ref, glr_ref, wg_ref, bg_ref, gg_ref = refs[:8]
    la_s, oacc, st_s = refs[-3:]
    if has_state:
        s0_ref, o_ref = refs[8], refs[-4]
    else:
        o_ref, sf_ref = refs[-5:-3]

    def kcols(hd):
        return slice(hd * DK_C, (hd + 1) * DK_C)

    def vcols(hd):
        return slice(hd * DV_C, (hd + 1) * DV_C)

    glr = glr_ref[...]
    for hd in range(heads):
        for d in range(2):
            gate = _mm(glr, wg_ref[d, :, kcols(hd)]) + bg_ref[d, :, kcols(hd)]
            la_s[hd * 2 + d] = ((jnp.minimum(gate, 0.0) - jnp.log1p(jnp.exp(-jnp.abs(gate))))
                                * (LOG2E / GLA_NORMALIZER))

    row = lax.broadcasted_iota(jnp.int32, (c, c), 0)
    col = lax.broadcasted_iota(jnp.int32, (c, c), 1)
    top = row ^ col
    sh = 1
    while sh < c:
        top = top | (top >> sh)
        sh *= 2
    top = (top + 1) >> 1
    pair_code = jnp.where(row > col, top, -top)
    tri = [jnp.where(row >= col, 1.0, 0.0).astype(BF16), jnp.where(row <= col, 1.0, 0.0).astype(BF16)]

    def chunks(chains):
        n = range(len(chains))
        rows = [_aligned(ci * c, c, c) for _, ci, _ in chains]
        rev = [d == 1 for _, _, d in chains]
        hdir = [hd * 2 + d for hd, _, d in chains]
        q = [q_ref[rows[i], kcols(chains[i][0])] * DK_C ** -0.5 for i in n]
        k = [k_ref[rows[i], kcols(chains[i][0])] for i in n]
        v = [v_ref[rows[i], vcols(chains[i][0])] for i in n]
        qb = [q[i].astype(BF16) for i in n]
        kb = [k[i].astype(BF16) for i in n]
        attn = [jnp.where(row == col, _dot_nt(qb[i], kb[i]), 0.0) for i in n]
        la = [la_s[hdir[i], rows[i], :] for i in n]
        la_hi = [la[i].astype(BF16) for i in n]
        la_lo = [(la[i] - la_hi[i].astype(F32)).astype(BF16) for i in n]
        bc = [jnp.dot(tri[chains[i][2]], la_hi[i], preferred_element_type=F32)
              + jnp.dot(tri[chains[i][2]], la_lo[i], preferred_element_type=F32) for i in n]
        s = c // 2
        while s >= 1:
            for i in n:
                bref = _block_row_bcast(bc[i], 2 * s, s if rev[i] else s - 1)
                e = jnp.exp2(-jnp.abs(bc[i] - bref)).astype(BF16)
                pairs = pair_code == (-s if rev[i] else s)
                attn[i] = attn[i] + jnp.where(pairs, _dot_nt(qb[i] * e, kb[i] * e), 0.0)
            s //= 2
        b_end = [bc[i][0:1, :] if rev[i] else bc[i][c - 1:c, :] for i in n]
        st = [st_s[hdir[i]] for i in n]
        o = [_mm(attn[i], v[i]) + _mm_nt(q[i] * jnp.exp2(bc[i]), st[i]) for i in n]
        upd = [_mm_tn(v[i], k[i] * jnp.exp2(b_end[i] - bc[i])) for i in n]
        for i in n:
            st_s[hdir[i]] = st[i] * jnp.exp2(b_end[i]) + upd[i]
        for i in n:
            oacc[chains[i][0], rows[i], :] += o[i]

    for hd in range(heads):
        for d in range(2):
            if has_state:
                st_s[hd * 2 + d] = s0_ref[d, hd].T
            else:
                st_s[hd * 2 + d] = jnp.zeros((DV_C, DK_C), F32)
    oacc[...] = jnp.zeros((heads, t, DV_C), F32)

    def body(i, carry):
        chunks([(hd, i if d == 0 else nc - 1 - i, d) for hd in range(heads) for d in range(2)])
        return carry

    lax.fori_loop(0, nc, body, 0)

    for hd in range(heads):
        o_ref[:, vcols(hd)] = (_rms(oacc[hd]) * gg_ref[...] * _silu(z_ref[:, vcols(hd)])).astype(BF16)
        if not has_state:
            for d in range(2):
                _store_state(sf_ref, fill_slot, d, hd, st_s[hd * 2 + d].T)


def _gla_heads(t):
    for g in (GLA_MAX_HEADS, 1):
        scratch = g * (2 * t * DK_C + t * DV_C + 2 * DV_C * DK_C) * 4
        blocks = 2 * (g * t * (2 * DK_C + 2 * DV_C) * 4 + t * LANES * 4 + g * t * DV_C * 2
                      + 2 * g * DK_C * DV_C * 4)
        if scratch + blocks <= VMEM_LIMIT - (12 << 20):
            return g
    return 1


def _gla(proj, glr, wg_pad, bgate, ggain, state, state_out, layer_o, nseq, seq_len):
    t = seq_len
    has_state = state is not None
    aliases = {}
    fill_slot = None
    g = _gla_heads(t)
    nb = H_C // g
    wk, wv = g * DK_C, g * DV_C
    in_specs = [pl.BlockSpec((t, wk), lambda b, h: (b, h)),
                pl.BlockSpec((t, wk), lambda b, h: (b, nb + h)),
                pl.BlockSpec((t, wv), lambda b, h: (b, nb + h)),
                pl.BlockSpec((t, wv), lambda b, h: (b, 2 * nb + h)),
                pl.BlockSpec((t, LANES), lambda b, h: (b, 0)),
                pl.BlockSpec((2, LANES, wk), lambda b, h: (0, 0, h)),
                pl.BlockSpec((2, 1, wk), lambda b, h: (0, 0, h)),
                pl.BlockSpec((1, DV_C), lambda b, h: (0, 0))]
    args = [proj, proj, proj, proj, glr, wg_pad, bgate, ggain]
    o_shape = jax.ShapeDtypeStruct((nseq * t, H_C * DV_C), BF16)
    o_spec = pl.BlockSpec((t, wv), lambda b, h: (b, h))
    if has_state:
        in_specs.append(pl.BlockSpec((None, None, 2, g, DK_C, DV_C),
                                     lambda b, h: (b, layer_o, 0, h, 0, 0)))
        args.append(state)
        out_shape, out_specs = o_shape, o_spec
    else:
        prev, n_slots = state_out
        out_shape = (o_shape, jax.ShapeDtypeStruct((nseq, n_slots, 2, H_C, DK_C, DV_C), F32))
        if prev is None:
            fill_slot = layer_o
            s_spec = pl.BlockSpec((None, n_slots, 2, g, DK_C, DV_C), lambda b, h: (b, 0, 0, h, 0, 0))
        else:
            s_spec = pl.BlockSpec((None, None, 2, g, DK_C, DV_C), lambda b, h: (b, layer_o, 0, h, 0, 0))
            in_specs.append(pl.BlockSpec(memory_space=pl.ANY))
            args.append(prev)
            aliases = {len(args) - 1: 1}
        out_specs = (o_spec, s_spec)
    scratch = [pltpu.VMEM((2 * g, t, DK_C), F32),
               pltpu.VMEM((g, t, DV_C), F32),
               pltpu.VMEM((2 * g, DV_C, DK_C), F32)]
    return pl.pallas_call(
        functools.partial(_gla_kernel, seq_len=t, heads=g, has_state=has_state, fill_slot=fill_slot),
        grid=(nseq, nb),
        in_specs=in_specs,
        out_specs=out_specs,
        out_shape=out_shape,
        scratch_shapes=scratch,
        input_output_aliases=aliases,
        compiler_params=_params("parallel", "parallel"),
        name="gla",
    )(*args)


def _rope_tables(n_tokens):
    rows = n_tokens // GRID_W
    row = jnp.repeat(jnp.arange(rows), GRID_W).astype(F32)
    col = jnp.tile(jnp.arange(GRID_W), rows).astype(F32)
    quarter = DQK_B // 4
    freqs = ROPE_BASE ** (-jnp.arange(quarter, dtype=F32) / quarter)
    ar = row[:, None] * freqs
    ac = col[:, None] * freqs
    ang = jnp.concatenate([ar, ar, ac, ac] * (LANES // DQK_B), axis=-1)
    return jnp.cos(ang), jnp.sin(ang)


def _pad_lanes(x):
    return jnp.pad(x, ((0, 0), (0, LANES - x.shape[1])))


def kernel(x_prompt, x_sample, c, state_gdn, cache_k, cache_v, state_gla, c_ctx, w_ada, b_ada, norm_pre, norm_post, w_in_even, conv_even, a_log_even, dt_bias_even, gdn_norm_even, lam_even, diff_norm_even, w_out_even, w_in_odd, w_gate_odd, b_gate_odd, gla_norm_odd, w_out_odd):
    n_p, t_p, _ = x_prompt.shape
    n_s, t_s, _ = x_sample.shape
    wa = H_A * DK_A

    cvec = jnp.zeros((8, D_MODEL), F32).at[0].set(c_ctx).at[1:1 + n_s].set(c)
    mods = _adaln(cvec, w_ada, b_ada).reshape(DEPTH * 8, 1, 3 * D_MODEL)
    cos, sin = _rope_tables(t_s)

    y_p = x_prompt.reshape(n_p * t_p, D_MODEL)
    y_s = x_sample.reshape(n_s * t_s, D_MODEL)
    n_even, n_odd = w_in_even.shape[0], w_in_odd.shape[0]
    wt_even = jnp.swapaxes(w_in_even, 1, 2)
    wt_odd = jnp.swapaxes(w_in_odd, 1, 2)
    gdn_states = new_k = new_v = gla_states = None

    def gidx(layer):
        return _group_index(layer, 0, None), _group_index(layer, 1, t_s // ROW_TILE)

    gidx_p, gidx_s = gidx(0)
    h_p = _prenorm(y_p, norm_pre[0], mods, gidx_p)
    h_s = _prenorm(y_s, norm_pre[0], mods, gidx_s)

    for layer in range(DEPTH):
        gidx_p, gidx_s = gidx(layer)
        if layer + 1 < DEPTH:
            next_gain = norm_pre[layer + 1]
            gnext_p, gnext_s = gidx(layer + 1)
        else:
            next_gain = gnext_p = gnext_s = None
        if layer % 2 == 0:
            e = layer // 2
            lam_init = 0.8 - 0.6 * math.exp(-0.3 * layer)
            w_out = w_out_even[e].astype(BF16)
            zeros16 = jnp.zeros((2 * H_A,), F32)
            alog_row = _pad_lanes(jnp.concatenate([zeros16, a_log_even[e].reshape(-1)])[None, :])
            dtb_row = _pad_lanes(jnp.concatenate([zeros16, dt_bias_even[e].reshape(-1)])[None, :])
            gnorm = gdn_norm_even[e][None, :]
            dgain = diff_norm_even[e][None, :]
            col_b = 4 * wa
            shift_b = 4 * H_A

            pa_p, pa_s = _inproj2(h_p, h_s, wt_even, e, 0, 4 * wa, IN_TILE, 1024)
            gp_p, gp_s = _inproj2(h_p, h_s, wt_even, e, col_b, LANES, IN_TILE, LANES)
            qb_p, qb_s = _inproj2(h_p, h_s, wt_even, e, col_b, wa, IN_TILE, wa, shift_b)
            zb_p, zb_s = _inproj2(h_p, h_s, wt_even, e, col_b + 3 * wa, wa, IN_TILE, wa, shift_b)
            kv_s = _inproj(h_s, wt_even, e, col_b + wa, 2 * wa, IN_TILE, wa, shift_b)
            kv_tile = IN_TILE if new_k is not None else IN_TILE // n_even
            new_k = _inproj(h_p, wt_even, e, col_b + wa, wa, kv_tile, wa, shift_b, (new_k, e, t_p))
            new_v = _inproj(h_p, wt_even, e, col_b + 2 * wa, wa, kv_tile, wa, shift_b, (new_v, e, t_p))
            oa_p, gdn_states = _gdn(pa_p, gp_p, conv_even[e], alog_row, dtb_row, gnorm, None,
                                    (gdn_states, n_even), e, n_p, t_p)
            ob_p = _attn((qb_p, 0), (new_k, e), (new_v, e), (zb_p, 0), lam_even[e], dgain, lam_init,
                         None, e, n_p, t_p)
            out_p = _outproj([oa_p, ob_p], w_out, y_p, norm_post[layer], mods, gidx_p, next_gain, gnext_p)

            oa_s = _gdn(pa_s, gp_s, conv_even[e], alog_row, dtb_row, gnorm, state_gdn, None, e, n_s, t_s)
            ob_s = _attn((qb_s, 0), (kv_s, 0), (kv_s, 1), (zb_s, 0), lam_even[e], dgain, lam_init,
                         (cos, sin, cache_k, cache_v), e, n_s, t_s)
            out_s = _outproj([oa_s, ob_s], w_out, y_s, norm_post[layer], mods, gidx_s, next_gain, gnext_s)
        else:
            od = layer // 2
            n_main = 2 * H_C * DK_C + 2 * H_C * DV_C
            w_out = w_out_odd[od].astype(BF16)
            wg_pad = jnp.zeros((2, LANES, H_C * DK_C), F32)
            for d in range(2):
                wg_pad = wg_pad.at[d, d * GATE_RANK:(d + 1) * GATE_RANK].set(w_gate_odd[od, d])
            bgate = b_gate_odd[od].reshape(2, 1, H_C * DK_C)
            ggain = gla_norm_odd[od][None, :]

            proj_p, proj_s = _inproj2(h_p, h_s, wt_odd, od, 0, n_main, IN_TILE, 1024)
            glr_p, glr_s = _inproj2(h_p, h_s, wt_odd, od, n_main, LANES, IN_TILE, LANES)
            o_p, gla_states = _gla(proj_p, glr_p, wg_pad, bgate, ggain, None, (gla_states, n_odd),
                                   od, n_p, t_p)
            out_p = _outproj([o_p], w_out, y_p, norm_post[layer], mods, gidx_p, next_gain, gnext_p)

            o_s = _gla(proj_s, glr_s, wg_pad, bgate, ggain, state_gla, None, od, n_s, t_s)
            out_s = _outproj([o_s], w_out, y_s, norm_post[layer], mods, gidx_s, next_gain, gnext_s)
        if next_gain is None:
            y_p, y_s = out_p, out_s
        else:
            (y_p, h_p), (y_s, h_s) = out_p, out_s

    return (y_p.reshape(n_p, t_p, D_MODEL),
            y_s.reshape(n_s, t_s, D_MODEL),
            gdn_states,
            new_k.reshape(n_p, n_even, t_p, H_B, 2 * DQK_B),
            new_v.reshape(n_p, n_even, t_p, H_B, DV_B),
            gla_states)
```

```python
import functools
import math

import jax
import jax.numpy as jnp
from jax import lax
from jax.experimental import pallas as pl
from jax.experimental.pallas import tpu as pltpu

F32 = jnp.float32
BF16 = jnp.bfloat16
HIGHEST = lax.Precision.HIGHEST

D_MODEL = 2048
DEPTH = 4
EPS = 1e-6
GRID_W = 64
H_A = 8
DK_A = 128
DV_A = 128
CONV_K = 5
CHUNK_A = 64
PREP_CHAINS = 16
GDN_MAX_HEADS = 4
H_B = 8
DQK_B = 64
DV_B = 128
ROPE_BASE = 10000.0
Q_TILE = 256
Q_TILE_CTX = 256
ATTN_HEADS = 4
ATTN_HEADS_CTX = 2
H_C = 4
DK_C = 256
DV_C = 512
GATE_RANK = 16
GLA_NORMALIZER = 16.0
CHUNK_C = 128
GLA_MAX_HEADS = 2

LOG2E = math.log2(math.e)
LANES = 128
VMEM_LIMIT = 48 * 1024 * 1024
VMEM_MARGIN = 4 * 1024 * 1024
ROW_TILE = 512
OUT_SUBTILES = 4
IN_TILE = 1024


def _params(*semantics, vmem=VMEM_LIMIT):
    return pltpu.CompilerParams(dimension_semantics=semantics, vmem_limit_bytes=vmem)


def _mm(a, b):
    return jnp.dot(a.astype(BF16), b.astype(BF16), preferred_element_type=F32)


def _mm_nt(a, b):
    return lax.dot_general(a.astype(BF16), b.astype(BF16),
                           (((1,), (1,)), ((), ())), preferred_element_type=F32)


def _dot_nt(a, b):
    return lax.dot_general(a, b, (((1,), (1,)), ((), ())), preferred_element_type=F32)


def _mm_tn(a, b):
    return lax.dot_general(a.astype(BF16), b.astype(BF16),
                           (((0,), (0,)), ((), ())), preferred_element_type=F32)


def _mm_hi(a, b):
    return jnp.dot(a, b, precision=HIGHEST, preferred_element_type=F32)


def _mm_tn_hi(a, b):
    return lax.dot_general(a, b, (((0,), (0,)), ((), ())), precision=HIGHEST,
                           preferred_element_type=F32)


def _aligned(start, size, align):
    if isinstance(start, int):
        return pl.ds(start, size)
    return pl.ds(pl.multiple_of(start, align), size)


def _store_state(sf_ref, fill_slot, d, hd, state):
    if fill_slot is None:
        sf_ref[d, hd] = state
    else:
        for s in range(sf_ref.shape[0]):
            sf_ref[s, d, hd] = state if s == fill_slot else jnp.zeros_like(state)


def _silu(x):
    return x * jax.nn.sigmoid(x)


def _softplus(x):
    return jnp.maximum(x, 0.0) + jnp.log1p(jnp.exp(-jnp.abs(x)))


def _rms(x):
    return x * lax.rsqrt(jnp.mean(x * x, axis=-1, keepdims=True) + EPS)


def _adaln_kernel(c_ref, w_ref, b_ref, o_ref):
    o_ref[...] = _mm(_silu(c_ref[...]), w_ref[...]) + b_ref[...]


def _adaln(cvec, w_ada, b_ada):
    tn = 1024
    n3 = 3 * D_MODEL
    return pl.pallas_call(
        _adaln_kernel,
        grid=(DEPTH, n3 // tn),
        in_specs=[pl.BlockSpec((8, D_MODEL), lambda l, j: (0, 0)),
                  pl.BlockSpec((None, D_MODEL, tn), lambda l, j: (l, 0, j)),
                  pl.BlockSpec((None, 1, tn), lambda l, j: (l, 0, j))],
        out_specs=pl.BlockSpec((None, 8, tn), lambda l, j: (l, 0, j)),
        out_shape=jax.ShapeDtypeStruct((DEPTH, 8, n3), F32),
        compiler_params=_params("parallel", "parallel"),
        name="adaln",
    )(cvec, w_ada, b_ada.reshape(DEPTH, 1, n3))


def _prenorm_kernel(x_ref, g_ref, m_ref, h_ref):
    y = _rms(x_ref[...]) * g_ref[...]
    m = m_ref[...]
    h_ref[...] = (y * (1.0 + m[:, D_MODEL:2 * D_MODEL]) + m[:, :D_MODEL]).astype(BF16)


def _group_index(layer, group0, tiles_per_group):
    if tiles_per_group is None:
        return lambda i: (layer * 8 + group0, 0, 0)
    return lambda i: (layer * 8 + group0 + i // tiles_per_group, 0, 0)


def _prenorm(y, gain, mods, gidx):
    m = y.shape[0]
    return pl.pallas_call(
        _prenorm_kernel,
        grid=(m // ROW_TILE,),
        in_specs=[pl.BlockSpec((ROW_TILE, D_MODEL), lambda i: (i, 0)),
                  pl.BlockSpec((1, D_MODEL), lambda i: (0, 0)),
                  pl.BlockSpec((None, 1, 3 * D_MODEL), gidx)],
        out_specs=pl.BlockSpec((ROW_TILE, D_MODEL), lambda i: (i, 0)),
        out_shape=jax.ShapeDtypeStruct((m, D_MODEL), BF16),
        compiler_params=_params("parallel"),
        name="prenorm",
    )(y, gain.reshape(1, D_MODEL), mods)


def _cast_weight_block(wa_ref, wb_ref, wbf, shift, valid):
    @pl.when(pl.program_id(1) == 0)
    def _():
        w = wa_ref[...]
        if shift:
            w = jnp.concatenate([w[shift:], wb_ref[...]], axis=0)
        if valid is not None:
            w = jnp.where(lax.broadcasted_iota(jnp.int32, w.shape, 0) < valid, w, 0.0)
        wbf[...] = w.astype(BF16)


def _inproj2_kernel(*refs, shift, valid, tiles_a):
    xa_ref, xb_ref, wa_ref = refs[:3]
    wb_ref = refs[3] if shift else None
    oa_ref, ob_ref, wbf = refs[-3:]
    _cast_weight_block(wa_ref, wb_ref, wbf, shift, valid)

    @pl.when(pl.program_id(1) < tiles_a)
    def _():
        oa_ref[...] = _dot_nt(xa_ref[...], wbf[...]).astype(oa_ref.dtype)

    @pl.when(pl.program_id(1) >= tiles_a)
    def _():
        ob_ref[...] = _dot_nt(xb_ref[...], wbf[...]).astype(ob_ref.dtype)


def _inproj_kernel(*refs, shift, valid, fill_slot):
    x_ref, wa_ref = refs[:2]
    wb_ref = refs[2] if shift else None
    o_ref, wbf = refs[-2:]
    _cast_weight_block(wa_ref, wb_ref, wbf, shift, valid)
    res = _dot_nt(x_ref[...], wbf[...])
    if fill_slot is None:
        o_ref[...] = res.reshape(o_ref.shape).astype(o_ref.dtype)
    else:
        for s in range(o_ref.shape[1]):
            blk = (o_ref.shape[0],) + o_ref.shape[2:]
            o_ref[:, s] = res.reshape(blk) if s == fill_slot else jnp.zeros(blk, F32)


def _inproj(x, wt3, layer_idx, row0, n_out, tm, tn, shift=0, cache_slot=None, out_dtype=F32):
    m, k = x.shape
    n_rows = wt3.shape[1]
    valid = n_rows - row0 if row0 + n_out > n_rows else None
    assert valid is None or (n_out == tn and not shift)
    in_specs = [pl.BlockSpec((tm, k), lambda j, i: (i, 0)),
                pl.BlockSpec((None, tn, k), lambda j, i: (layer_idx, row0 // tn + j, 0))]
    args = [x, wt3]
    if shift:
        in_specs.append(pl.BlockSpec((None, shift, k),
                                     lambda j, i: (layer_idx, (row0 + (j + 1) * tn) // shift, 0)))
        args.append(wt3)
    aliases = {}
    fill_slot = None
    if cache_slot is None:
        out_spec = pl.BlockSpec((tm, tn), lambda j, i: (i, j))
        out_shape = jax.ShapeDtypeStruct((m, n_out), out_dtype)
    else:
        prev, slot, t = cache_slot
        n_slots = wt3.shape[0]
        out_shape = jax.ShapeDtypeStruct((m // t, n_slots, t, n_out), F32)
        if prev is None:
            fill_slot = slot
            out_spec = pl.BlockSpec((tm // t, n_slots, t, tn), lambda j, i: (i, 0, 0, 0))
        else:
            out_spec = pl.BlockSpec((tm // t, None, t, tn), lambda j, i: (i, slot, 0, 0))
            in_specs.append(pl.BlockSpec(memory_space=pl.ANY))
            args.append(prev)
            aliases = {len(args) - 1: 0}

    return pl.pallas_call(
        functools.partial(_inproj_kernel, shift=shift, valid=valid, fill_slot=fill_slot),
        grid=(n_out // tn, m // tm),
        in_specs=in_specs,
        out_specs=out_spec,
        out_shape=out_shape,
        scratch_shapes=[pltpu.VMEM((tn, k), BF16)],
        input_output_aliases=aliases,
        compiler_params=_params("parallel", "arbitrary"),
        name="inproj",
    )(*args)


def _inproj2(xa, xb, wt3, layer_idx, row0, n_out, tm, tn, shift=0, out_dtype=F32):
    (ma, k), mb = xa.shape, xb.shape[0]
    ta, tb = ma // tm, mb // tm
    n_rows = wt3.shape[1]
    valid = n_rows - row0 if row0 + n_out > n_rows else None
    assert valid is None or (n_out == tn and not shift)

    def a_tile(i):
        return jnp.minimum(i, ta - 1)

    def b_tile(i):
        return jnp.maximum(i - ta, 0)

    in_specs = [pl.BlockSpec((tm, k), lambda j, i: (a_tile(i), 0)),
                pl.BlockSpec((tm, k), lambda j, i: (b_tile(i), 0)),
                pl.BlockSpec((None, tn, k), lambda j, i: (layer_idx, row0 // tn + j, 0))]
    args = [xa, xb, wt3]
    if shift:
        in_specs.append(pl.BlockSpec((None, shift, k),
                                     lambda j, i: (layer_idx, (row0 + (j + 1) * tn) // shift, 0)))
        args.append(wt3)
    return pl.pallas_call(
        functools.partial(_inproj2_kernel, shift=shift, valid=valid, tiles_a=ta),
        grid=(n_out // tn, ta + tb),
        in_specs=in_specs,
        out_specs=(pl.BlockSpec((tm, tn), lambda j, i: (a_tile(i), j)),
                   pl.BlockSpec((tm, tn), lambda j, i: (b_tile(i), j))),
        out_shape=(jax.ShapeDtypeStruct((ma, n_out), out_dtype),
                   jax.ShapeDtypeStruct((mb, n_out), out_dtype)),
        scratch_shapes=[pltpu.VMEM((tn, k), BF16)],
        compiler_params=_params("parallel", "arbitrary",
                                vmem=4 * tm * k * 2 + 4 * tm * tn * 4 + 2 * tn * k * 4 + tn * k * 2
                                + 2 * shift * k * 4 + VMEM_MARGIN),
        name="inproj2",
    )(*args)


def _outproj_kernel(*refs, n_in, emit_h):
    x_refs = refs[:n_in]
    w_ref, y_ref, g_ref, m_ref = refs[n_in:n_in + 4]
    if emit_h:
        gn_ref, mn_ref, o_ref, h_ref = refs[n_in + 4:]
    else:
        o_ref = refs[n_in + 4]
    m = m_ref[...]
    sub = o_ref.shape[0] // OUT_SUBTILES
    accs = []
    for s in range(OUT_SUBTILES):
        rows = slice(s * sub, (s + 1) * sub)
        acc = None
        off = 0
        for x_ref in x_refs:
            kk = x_ref.shape[1]
            part = jnp.dot(x_ref[rows, :], w_ref[off:off + kk, :], preferred_element_type=F32)
            acc = part if acc is None else acc + part
            off += kk
        accs.append(acc)
    for s in range(OUT_SUBTILES):
        rows = slice(s * sub, (s + 1) * sub)
        y = y_ref[rows, :] + m[:, 2 * D_MODEL:] * (_rms(accs[s]) * g_ref[...])
        if emit_h:
            mn = mn_ref[...]
            h_ref[rows, :] = (_rms(y) * gn_ref[...] * (1.0 + mn[:, D_MODEL:2 * D_MODEL])
                              + mn[:, :D_MODEL]).astype(BF16)
        o_ref[rows, :] = y


def _outproj(xs, w, y, gain, mods, gidx, next_gain=None, gidx_next=None):
    m = y.shape[0]
    emit_h = next_gain is not None
    row_spec = pl.BlockSpec((ROW_TILE, D_MODEL), lambda i: (i, 0))
    vec_spec = pl.BlockSpec((1, D_MODEL), lambda i: (0, 0))
    in_specs = [pl.BlockSpec((ROW_TILE, x.shape[1]), lambda i: (i, 0)) for x in xs]
    in_specs += [pl.BlockSpec(w.shape, lambda i: (0, 0)), row_spec, vec_spec,
                 pl.BlockSpec((None, 1, 3 * D_MODEL), gidx)]
    args = [*xs, w, y, gain.reshape(1, D_MODEL), mods]
    out_specs = row_spec
    out_shape = jax.ShapeDtypeStruct((m, D_MODEL), F32)
    if emit_h:
        in_specs += [vec_spec, pl.BlockSpec((None, 1, 3 * D_MODEL), gidx_next)]
        args += [next_gain.reshape(1, D_MODEL), mods]
        out_specs = (row_spec, row_spec)
        out_shape = (out_shape, jax.ShapeDtypeStruct((m, D_MODEL), BF16))
    return pl.pallas_call(
        functools.partial(_outproj_kernel, n_in=len(xs), emit_h=emit_h),
        grid=(m // ROW_TILE,),
        in_specs=in_specs,
        out_specs=out_specs,
        out_shape=out_shape,
        compiler_params=_params("parallel"),
        name="outproj",
    )(*args)


def _gdn_kernel(*refs, seq_len, heads, has_state, fill_slot):
    t = seq_len
    c = CHUNK_A
    nc = t // c
    (q_ref, k_ref, v_ref, z_ref, gp_ref, cwq_ref, cwk_ref, cwv_ref,
     alog_ref, dtb_ref, gn_ref) = refs[:11]
    qn, kn, vn, bt, gt, wv_s, wkqd_s, qk_s, kd_s, gl_s, oacc, s_s, xpad = refs[-13:]
    if has_state:
        s0_ref, o_ref = refs[11], refs[-14]
    else:
        o_ref, sf_ref = refs[-15:-13]
    head0 = pl.program_id(1) * heads

    def lanes_of(hd):
        return slice(hd * LANES, (hd + 1) * LANES)

    row_t = lax.broadcasted_iota(jnp.int32, (t, LANES), 0)
    xpad[0:8, :] = jnp.zeros((8, LANES), F32)
    xpad[8 + t:, :] = jnp.zeros((8, LANES), F32)

    def conv_silu(x_ref, w_ref, hd):
        w = w_ref[:, lanes_of(hd)]
        xpad[8:8 + t, :] = x_ref[:, lanes_of(hd)]
        acc = None
        for tap in range(CONV_K):
            start = 8 + tap - CONV_K // 2
            term = xpad[start:start + t, :] * w[tap:tap + 1, :]
            acc = term if acc is None else acc + term
        return _silu(acc)

    def l2n(x):
        return x * lax.rsqrt(jnp.sum(x * x, axis=-1, keepdims=True) + EPS)

    for hd in range(heads):
        qn[hd] = l2n(conv_silu(q_ref, cwq_ref, hd))
        kn[hd] = l2n(conv_silu(k_ref, cwk_ref, hd))
        vn[hd] = conv_silu(v_ref, cwv_ref, hd)

    gp = gp_ref[...]
    lane_t = lax.broadcasted_iota(jnp.int32, (t, LANES), 1)
    beta_all = jax.nn.sigmoid(gp)
    g_all = -jnp.exp(alog_ref[...]) * _softplus(gp + dtb_ref[...])

    def pick(x, idx):
        col = jnp.sum(jnp.where(lane_t == idx, x, 0.0), axis=-1, keepdims=True)
        return jnp.broadcast_to(col, (t, LANES))

    pos_t = row_t & (c - 1)
    g_cum = [g_all, g_all]
    sh = 1
    while sh < c:
        g_cum[0] = g_cum[0] + jnp.where(pos_t >= sh, pltpu.roll(g_cum[0], sh, axis=0), 0.0)
        g_cum[1] = g_cum[1] + jnp.where(pos_t < c - sh, pltpu.roll(g_cum[1], t - sh, axis=0), 0.0)
        sh *= 2

    for hd in range(heads):
        for d in range(2):
            bt[hd * 2 + d] = pick(beta_all, d * H_A + head0 + hd)
            gt[hd * 2 + d] = pick(g_cum[d], 2 * H_A + d * H_A + head0 + hd)

    r2 = 2 * c
    row = lax.broadcasted_iota(jnp.int32, (r2, r2), 0)
    col = lax.broadcasted_iota(jnp.int32, (r2, r2), 1)
    rc_xor = row ^ col

    tri_inc, tri_str = [], []
    for rev in (False, True):
        after = (col - row) if rev else (row - col)
        after = jnp.where(rc_xor < c, after, -1)
        tri_inc.append(after >= 0)
        tri_str.append(after > 0)
    eye = jnp.where(row == col, 1.0, 0.0)

    def prep(chains):
        n = range(len(chains))
        rows = [_aligned(gi * r2, r2, r2) for _, gi, _ in chains]
        dirs = [d for _, _, d in chains]
        hdir = [hd * 2 + d for hd, _, d in chains]
        q = [qn[chains[i][0], rows[i], :] * DK_A ** -0.5 for i in n]
        k = [kn[chains[i][0], rows[i], :] for i in n]
        beta = [bt[hdir[i], rows[i], :] for i in n]
        kb = [k[i] * beta[i] for i in n]
        kk = [_mm_nt(kb[i], k[i]) for i in n]
        qk = [_mm_nt(q[i], k[i]) for i in n]
        gcum = [gt[hdir[i], rows[i], :] for i in n]
        decay = [jnp.where(tri_inc[dirs[i]], jnp.exp(jnp.minimum(gcum[i] - gcum[i].T, 0.0)), 0.0) for i in n]
        amat = [jnp.where(tri_str[dirs[i]], kk[i] * decay[i], 0.0) for i in n]
        tinv = [eye - jnp.where(rc_xor < 2, amat[i], 0.0) for i in n]
        b = 2
        while b < c:
            level = (rc_xor >= b) & (rc_xor < 2 * b)
            m1 = [_mm(jnp.where(level, amat[i], 0.0), tinv[i]) for i in n]
            m2 = [_mm(tinv[i], m1[i]) for i in n]
            tinv = [tinv[i] - m2[i] for i in n]
            b *= 2
        eg = [jnp.exp(gcum[i]) for i in n]
        x = [_mm(tinv[i], jnp.concatenate([vn[chains[i][0], rows[i], :] * beta[i], kb[i] * eg[i]], axis=1))
             for i in n]
        for i in n:
            _, gi, d = chains[i]
            hd2 = hdir[i]
            qkd = qk[i] * decay[i]
            g_end = _block_row_bcast(gcum[i], c, 0 if d == 1 else c - 1)
            qd = q[i] * eg[i]
            gl = jnp.exp(g_end)
            wv_s[hd2, rows[i], :] = x[i][:, :DV_A]
            qk_s[hd2, rows[i], :] = jnp.where(row < c, qkd, pltpu.roll(qkd, c, axis=1))[:, :c]
            kd_s[hd2, rows[i], :] = k[i] * jnp.exp(g_end - gcum[i])
            for j in range(2):
                base = gi * 2 * r2 + j * r2
                wkqd_s[hd2, _aligned(base, c, c), :] = x[i][j * c:(j + 1) * c, DV_A:]
                wkqd_s[hd2, _aligned(base + c, c, c), :] = qd[j * c:(j + 1) * c, :]
                gl_s[hd2, _aligned(gi * 16 + j * 8, 8, 8), :] = gl[j * c:j * c + 8, :]

    def rec(chains):
        n = range(len(chains))
        rows = [_aligned(ci * c, c, c) for _, ci, _ in chains]
        hdir = [hd * 2 + d for hd, _, d in chains]
        s = [s_s[hdir[i]] for i in n]
        ws = [_mm(wkqd_s[hdir[i], _aligned(chains[i][1] * 2 * c, 2 * c, 2 * c), :], s[i]) for i in n]
        u = [wv_s[hdir[i], rows[i], :] - ws[i][:c] for i in n]
        o2 = [_mm(qk_s[hdir[i], rows[i], :], u[i]) for i in n]
        ds = [_mm_tn(kd_s[hdir[i], rows[i], :], u[i]) for i in n]
        for i in n:
            s_s[hdir[i]] = s[i] * gl_s[hdir[i], _aligned(chains[i][1] * 8, 1, 8), :] + ds[i]
        for i in n:
            oacc[chains[i][0], rows[i], :] += ws[i][c:] + o2[i]

    for hd in range(heads):
        for d in range(2):
            if has_state:
                s_s[hd * 2 + d] = s0_ref[d, hd]
            else:
                s_s[hd * 2 + d] = jnp.zeros((DK_A, DV_A), F32)
    oacc[...] = jnp.zeros((heads, t, DV_A), F32)

    n_groups = nc // 2
    unroll = max(1, min(n_groups, PREP_CHAINS // (2 * heads)))

    def prep_body(gi, carry):
        prep([(hd, gi * unroll + j, d) for hd in range(heads) for j in range(unroll) for d in range(2)])
        return carry

    if n_groups == unroll:
        prep_body(0, 0)
    else:
        lax.fori_loop(0, n_groups // unroll, prep_body, 0)

    def rec_body(i, carry):
        rec([(hd, i if d == 0 else nc - 1 - i, d) for hd in range(heads) for d in range(2)])
        return carry

    lax.fori_loop(0, nc, rec_body, 0)

    for hd in range(heads):
        o_ref[:, lanes_of(hd)] = (_rms(oacc[hd]) * gn_ref[...] * _silu(z_ref[:, lanes_of(hd)])).astype(BF16)
        if not has_state:
            for d in range(2):
                _store_state(sf_ref, fill_slot, d, hd, s_s[hd * 2 + d])


def _gdn_heads(t):
    for g in (GDN_MAX_HEADS, 2, 1):
        scratch = g * t * LANES * 4 * 17 + 2 * g * DK_A * DV_A * 4
        blocks = 2 * (4 * t * g * LANES * 4 + t * LANES * 4 + t * g * LANES * 2)
        if scratch + blocks <= VMEM_LIMIT - (8 << 20):
            return g
    return 1


def _gdn(proj, gp, conv_w, alog_row, dtb_row, gnorm, state, state_out, layer_e, nseq, seq_len):
    t = seq_len
    nc = t // CHUNK_A
    has_state = state is not None
    aliases = {}
    fill_slot = None

    g = _gdn_heads(t)
    w = g * LANES
    nb = H_A // g

    def col(seg):
        return pl.BlockSpec((t, w), lambda b, h: (b, seg * nb + h))

    def cw(seg):
        return pl.BlockSpec((CONV_K, w), lambda b, h: (0, seg * nb + h))

    row_spec = pl.BlockSpec((1, LANES), lambda b, h: (0, 0))
    in_specs = [col(0), col(1), col(2), col(3),
                pl.BlockSpec((t, LANES), lambda b, h: (b, 0)),
                cw(0), cw(1), cw(2), row_spec, row_spec, row_spec]
    args = [proj, proj, proj, proj, gp, conv_w, conv_w, conv_w, alog_row, dtb_row, gnorm]
    o_shape = jax.ShapeDtypeStruct((nseq * t, H_A * DV_A), BF16)
    o_spec = pl.BlockSpec((t, w), lambda b, h: (b, h))
    if has_state:
        in_specs.append(pl.BlockSpec((None, None, 2, g, DK_A, DV_A),
                                     lambda b, h: (b, layer_e, 0, h, 0, 0)))
        args.append(state)
        out_shape, out_specs = o_shape, o_spec
    else:
        prev, n_slots = state_out
        out_shape = (o_shape, jax.ShapeDtypeStruct((nseq, n_slots, 2, H_A, DK_A, DV_A), F32))
        if prev is None:
            fill_slot = layer_e
            s_spec = pl.BlockSpec((None, n_slots, 2, g, DK_A, DV_A), lambda b, h: (b, 0, 0, h, 0, 0))
        else:
            s_spec = pl.BlockSpec((None, None, 2, g, DK_A, DV_A), lambda b, h: (b, layer_e, 0, h, 0, 0))
            in_specs.append(pl.BlockSpec(memory_space=pl.ANY))
            args.append(prev)
            aliases = {len(args) - 1: 1}
        out_specs = (o_spec, s_spec)
    scratch = [pltpu.VMEM((g, t, LANES), F32)] * 3
    scratch += [pltpu.VMEM((2 * g, t, LANES), F32)] * 2
    scratch += [pltpu.VMEM((2 * g, t, LANES), F32),
                pltpu.VMEM((2 * g, 2 * t, LANES), F32),
                pltpu.VMEM((2 * g, t, CHUNK_A), F32),
                pltpu.VMEM((2 * g, t, LANES), F32),
                pltpu.VMEM((2 * g, nc * 8, LANES), F32),
                pltpu.VMEM((g, t, LANES), F32),
                pltpu.VMEM((2 * g, DK_A, DV_A), F32),
                pltpu.VMEM((t + 16, LANES), F32)]
    return pl.pallas_call(
        functools.partial(_gdn_kernel, seq_len=t, heads=g, has_state=has_state, fill_slot=fill_slot),
        grid=(nseq, nb),
        in_specs=in_specs,
        out_specs=out_specs,
        out_shape=out_shape,
        scratch_shapes=scratch,
        input_output_aliases=aliases,
        compiler_params=_params("parallel", "parallel"),
        name="gdn",
    )(*args)


def _rope(x, cos, sin):
    x = x.astype(F32)
    lane = lax.broadcasted_iota(jnp.int32, x.shape, 1)
    quarter = DQK_B // 4
    rot = jnp.where((lane & quarter) == 0,
                    -pltpu.roll(x, LANES - quarter, axis=1),
                    pltpu.roll(x, quarter, axis=1))
    return x * cos + rot * sin


def _attn_kernel(*refs, seq_len, heads, lam_init, has_ctx):
    t = seq_len
    q_ref, k_ref, v_ref, z_ref, lam_ref, dg_ref = refs[:6]
    if has_ctx:
        cq_ref, sq_ref, ck_ref, sk_ref, kc_ref, vc_ref, o_ref, k_all, v_all = refs[6:]
    else:
        o_ref = refs[6]

    def lanes_of(hd):
        return slice(hd * LANES, (hd + 1) * LANES)

    if has_ctx:
        @pl.when(pl.program_id(2) == 0)
        def _():
            for hd in range(heads):
                k_all[hd, 0:t, :] = _rope(k_ref[:, lanes_of(hd)], ck_ref[...], sk_ref[...])
                k_all[hd, t:, :] = kc_ref[:, lanes_of(hd)]
                v_all[hd, 0:t, :] = v_ref[:, lanes_of(hd)].astype(F32)
                v_all[hd, t:, :] = vc_ref[:, lanes_of(hd)]
    lp = lam_ref[...]
    lam = (jnp.exp(jnp.sum(lp[0:1] * lp[1:2], axis=-1, keepdims=True))
           - jnp.exp(jnp.sum(lp[2:3] * lp[3:4], axis=-1, keepdims=True)) + lam_init)
    lane = lax.broadcasted_iota(jnp.int32, (q_ref.shape[0], LANES), 1)

    chains = [(hd, m) for hd in range(heads) for m in range(2)]
    n = range(len(chains))
    q = []
    for hd in range(heads):
        qh = q_ref[:, lanes_of(hd)]
        if has_ctx:
            qh = _rope(qh, cq_ref[...], sq_ref[...])
        q.append(qh * DQK_B ** -0.5)
    k = [k_all[hd] if has_ctx else k_ref[:, lanes_of(hd)] for hd in range(heads)]
    v = [v_all[hd] if has_ctx else v_ref[:, lanes_of(hd)] for hd in range(heads)]
    qm = [jnp.where(lane < DQK_B, q[hd], 0.0) if m == 0 else jnp.where(lane < DQK_B, 0.0, q[hd])
          for hd, m in chains]
    s = [_mm_nt(qm[i], k[chains[i][0]]) for i in n]
    e = [jnp.exp(s[i] - jnp.max(s[i], axis=-1, keepdims=True)) for i in n]
    den = [jnp.sum(e[i], axis=-1, keepdims=True) for i in n]
    ev = [_mm(e[i], v[chains[i][0]]) for i in n]
    for hd in range(heads):
        o = ev[2 * hd] / den[2 * hd] - lam * (ev[2 * hd + 1] / den[2 * hd + 1])
        o_ref[:, lanes_of(hd)] = (_rms(o) * dg_ref[...] * (1.0 - lam_init)
                                  * _silu(z_ref[:, lanes_of(hd)])).astype(BF16)


def _attn(q, k, v, z, lam_p, dgain, lam_init, ctx, layer_e, nseq, seq_len):
    t = seq_len
    has_ctx = ctx is not None
    g = ATTN_HEADS_CTX if has_ctx else ATTN_HEADS
    tq = min(t, Q_TILE_CTX if has_ctx else Q_TILE)
    nq = t // tq
    w = g * LANES
    nb = H_B // g

    def rows_spec(src, rows):
        arr, seg = src
        if arr.ndim == 4:
            assert rows == t
            return pl.BlockSpec((None, None, t, w), lambda b, h, i: (b, seg, 0, h))
        if rows == t:
            return pl.BlockSpec((t, w), lambda b, h, i: (b, seg * nb + h))
        return pl.BlockSpec((rows, w), lambda b, h, i: (b * nq + i, seg * nb + h))

    in_specs = [rows_spec(q, tq), rows_spec(k, t), rows_spec(v, t), rows_spec(z, tq),
                pl.BlockSpec((4, DQK_B), lambda b, h, i: (0, 0)),
                pl.BlockSpec((1, LANES), lambda b, h, i: (0, 0))]
    args = [q[0], k[0], v[0], z[0], lam_p, dgain]
    scratch = []
    if has_ctx:
        cos, sin, cache_k, cache_v = ctx
        past = cache_k.shape[2]
        in_specs += [pl.BlockSpec((tq, LANES), lambda b, h, i: (i, 0)),
                     pl.BlockSpec((tq, LANES), lambda b, h, i: (i, 0)),
                     pl.BlockSpec((t, LANES), lambda b, h, i: (0, 0)),
                     pl.BlockSpec((t, LANES), lambda b, h, i: (0, 0)),
                     pl.BlockSpec((None, None, past, w), lambda b, h, i: (b, layer_e, 0, h)),
                     pl.BlockSpec((None, None, past, w), lambda b, h, i: (b, layer_e, 0, h))]
        args += [cos, sin, cos, sin, cache_k.reshape(cache_k.shape[:3] + (-1,)),
                 cache_v.reshape(cache_v.shape[:3] + (-1,))]
        scratch = [pltpu.VMEM((g, t + past, LANES), F32)] * 2
    return pl.pallas_call(
        functools.partial(_attn_kernel, seq_len=t, heads=g, lam_init=lam_init, has_ctx=has_ctx),
        grid=(nseq, nb, nq),
        in_specs=in_specs,
        out_specs=pl.BlockSpec((tq, w), lambda b, h, i: (b * nq + i, h)),
        out_shape=jax.ShapeDtypeStruct((nseq * t, H_B * DV_B), BF16),
        scratch_shapes=scratch,
        compiler_params=_params("parallel", "parallel", "arbitrary"),
        name="diffattn",
    )(*args)


def _block_row_bcast(x, blk, r):
    n, w = x.shape
    if blk >= 8:
        x3 = x.reshape(n // blk, blk, w)
        return jnp.broadcast_to(x3[:, r:r + 1, :], x3.shape).reshape(n, w)
    x3 = x.reshape(n // 8, 8, w)
    sub = lax.broadcasted_iota(jnp.int32, x3.shape, 1)
    out = None
    for b in range(8 // blk):
        cand = jnp.broadcast_to(x3[:, b * blk + r:b * blk + r + 1, :], x3.shape)
        out = cand if out is None else jnp.where(sub >= b * blk, cand, out)
    return out.reshape(n, w)


def _gla_kernel(*refs, seq_len, heads, has_state, fill_slot):
    t = seq_len
    c = CHUNK_C
    nc = t // c
    q_ref, k_ref, v_ref, z_ref, glr_ref, wg_ref, bg_ref, gg_ref = refs[:8]
    la_s, oacc, st_s = refs[-3:]
    if has_state:
        s0_ref, o_ref = refs[8], refs[-4]
    else:
        o_ref, sf_ref = refs[-5:-3]

    def kcols(hd):
        return slice(hd * DK_C, (hd + 1) * DK_C)

    def vcols(hd):
        return slice(hd * DV_C, (hd + 1) * DV_C)

    glr = glr_ref[...]
    for hd in range(heads):
        for d in range(2):
            gate = _mm(glr, wg_ref[d, :, kcols(hd)]) + bg_ref[d, :, kcols(hd)]
            la_s[hd * 2 + d] = ((jnp.minimum(gate, 0.0) - jnp.log(1.0 + jnp.exp(-jnp.abs(gate))))
                                * (LOG2E / GLA_NORMALIZER))

    row = lax.broadcasted_iota(jnp.int32, (c, c), 0)
    col = lax.broadcasted_iota(jnp.int32, (c, c), 1)
    top = row ^ col
    sh = 1
    while sh < c:
        top = top | (top >> sh)
        sh *= 2
    top = (top + 1) >> 1
    pair_code = jnp.where(row > col, top, -top)
    tri = [jnp.where(row >= col, 1.0, 0.0).astype(BF16), jnp.where(row <= col, 1.0, 0.0).astype(BF16)]

    def chunks(chains):
        n = range(len(chains))
        rows = [_aligned(ci * c, c, c) for _, ci, _ in chains]
        rev = [d == 1 for _, _, d in chains]
        hdir = [hd * 2 + d for hd, _, d in chains]
        q = [q_ref[rows[i], kcols(chains[i][0])] * DK_C ** -0.5 for i in n]
        k = [k_ref[rows[i], kcols(chains[i][0])] for i in n]
        v = [v_ref[rows[i], vcols(chains[i][0])] for i in n]
        qb = [q[i].astype(BF16) for i in n]
        kb = [k[i].astype(BF16) for i in n]
        attn = [jnp.where(row == col, _dot_nt(qb[i], kb[i]), 0.0) for i in n]
        la = [la_s[hdir[i], rows[i], :] for i in n]
        la_hi = [la[i].astype(BF16) for i in n]
        la_lo = [(la[i] - la_hi[i].astype(F32)).astype(BF16) for i in n]
        bc = [jnp.dot(tri[chains[i][2]], la_hi[i], preferred_element_type=F32)
              + jnp.dot(tri[chains[i][2]], la_lo[i], preferred_element_type=F32) for i in n]
        s = c // 2
        while s >= 1:
            for i in n:
                bref = _block_row_bcast(bc[i], 2 * s, s if rev[i] else s - 1)
                e = jnp.exp2(-jnp.abs(bc[i] - bref)).astype(BF16)
                pairs = pair_code == (-s if rev[i] else s)
                attn[i] = attn[i] + jnp.where(pairs, _dot_nt(qb[i] * e, kb[i] * e), 0.0)
            s //= 2
        b_end = [bc[i][0:1, :] if rev[i] else bc[i][c - 1:c, :] for i in n]
        st = [st_s[hdir[i]] for i in n]
        o = [_mm(attn[i], v[i]) + _mm_nt(q[i] * jnp.exp2(bc[i]), st[i]) for i in n]
        upd = [_mm_tn(v[i], k[i] * jnp.exp2(b_end[i] - bc[i])) for i in n]
        for i in n:
            st_s[hdir[i]] = st[i] * jnp.exp2(b_end[i]) + upd[i]
        for i in n:
            oacc[chains[i][0], rows[i], :] += o[i]

    for hd in range(heads):
        for d in range(2):
            if has_state:
                st_s[hd * 2 + d] = s0_ref[d, hd].T
            else:
                st_s[hd * 2 + d] = jnp.zeros((DV_C, DK_C), F32)
    oacc[...] = jnp.zeros((heads, t, DV_C), F32)

    def body(i, carry):
        chunks([(hd, i if d == 0 else nc - 1 - i, d) for hd in range(heads) for d in range(2)])
        return carry

    lax.fori_loop(0, nc, body, 0)

    for hd in range(heads):
        o_ref[:, vcols(hd)] = (_rms(oacc[hd]) * gg_ref[...] * _silu(z_ref[:, vcols(hd)])).astype(BF16)
        if not has_state:
            for d in range(2):
                _store_state(sf_ref, fill_slot, d, hd, st_s[hd * 2 + d].T)


def _gla_heads(t):
    for g in (GLA_MAX_HEADS, 1):
        scratch = g * (2 * t * DK_C + t * DV_C + 2 * DV_C * DK_C) * 4
        blocks = 2 * (g * t * (2 * DK_C + 2 * DV_C) * 4 + t * LANES * 4 + g * t * DV_C * 2
                      + 2 * g * DK_C * DV_C * 4)
        if scratch + blocks <= VMEM_LIMIT - (12 << 20):
            return g
    return 1


def _gla(qkv, zproj, glr, wg_pad, bgate, ggain, state, state_out, layer_o, nseq, seq_len):
    t = seq_len
    has_state = state is not None
    aliases = {}
    fill_slot = None
    g = _gla_heads(t)
    nb = H_C // g
    wk, wv = g * DK_C, g * DV_C
    in_specs = [pl.BlockSpec((t, wk), lambda b, h: (b, h)),
                pl.BlockSpec((t, wk), lambda b, h: (b, nb + h)),
                pl.BlockSpec((t, wv), lambda b, h: (b, nb + h)),
                pl.BlockSpec((t, wv), lambda b, h: (b, h)),
                pl.BlockSpec((t, LANES), lambda b, h: (b, 0)),
                pl.BlockSpec((2, LANES, wk), lambda b, h: (0, 0, h)),
                pl.BlockSpec((2, 1, wk), lambda b, h: (0, 0, h)),
                pl.BlockSpec((1, DV_C), lambda b, h: (0, 0))]
    args = [qkv, qkv, qkv, zproj, glr, wg_pad, bgate, ggain]
    o_shape = jax.ShapeDtypeStruct((nseq * t, H_C * DV_C), BF16)
    o_spec = pl.BlockSpec((t, wv), lambda b, h: (b, h))
    if has_state:
        in_specs.append(pl.BlockSpec((None, None, 2, g, DK_C, DV_C),
                                     lambda b, h: (b, layer_o, 0, h, 0, 0)))
        args.append(state)
        out_shape, out_specs = o_shape, o_spec
    else:
        prev, n_slots = state_out
        out_shape = (o_shape, jax.ShapeDtypeStruct((nseq, n_slots, 2, H_C, DK_C, DV_C), F32))
        if prev is None:
            fill_slot = layer_o
            s_spec = pl.BlockSpec((None, n_slots, 2, g, DK_C, DV_C), lambda b, h: (b, 0, 0, h, 0, 0))
        else:
            s_spec = pl.BlockSpec((None, None, 2, g, DK_C, DV_C), lambda b, h: (b, layer_o, 0, h, 0, 0))
            in_specs.append(pl.BlockSpec(memory_space=pl.ANY))
            args.append(prev)
            aliases = {len(args) - 1: 1}
        out_specs = (o_spec, s_spec)
    scratch = [pltpu.VMEM((2 * g, t, DK_C), F32),
               pltpu.VMEM((g, t, DV_C), F32),
               pltpu.VMEM((2 * g, DV_C, DK_C), F32)]
    return pl.pallas_call(
        functools.partial(_gla_kernel, seq_len=t, heads=g, has_state=has_state, fill_slot=fill_slot),
        grid=(nseq, nb),
        in_specs=in_specs,
        out_specs=out_specs,
        out_shape=out_shape,
        scratch_shapes=scratch,
        input_output_aliases=aliases,
        compiler_params=_params("parallel", "parallel"),
        name="gla",
    )(*args)


def _rope_tables(n_tokens):
    rows = n_tokens // GRID_W
    row = jnp.repeat(jnp.arange(rows), GRID_W).astype(F32)
    col = jnp.tile(jnp.arange(GRID_W), rows).astype(F32)
    quarter = DQK_B // 4
    freqs = ROPE_BASE ** (-jnp.arange(quarter, dtype=F32) / quarter)
    ar = row[:, None] * freqs
    ac = col[:, None] * freqs
    ang = jnp.concatenate([ar, ar, ac, ac] * (LANES // DQK_B), axis=-1)
    return jnp.cos(ang), jnp.sin(ang)


def _pad_lanes(x):
    return jnp.pad(x, ((0, 0), (0, LANES - x.shape[1])))


def kernel(x_prompt, x_sample, c, state_gdn, cache_k, cache_v, state_gla, c_ctx, w_ada, b_ada, norm_pre, norm_post, w_in_even, conv_even, a_log_even, dt_bias_even, gdn_norm_even, lam_even, diff_norm_even, w_out_even, w_in_odd, w_gate_odd, b_gate_odd, gla_norm_odd, w_out_odd):
    n_p, t_p, _ = x_prompt.shape
    n_s, t_s, _ = x_sample.shape
    wa = H_A * DK_A

    cvec = jnp.zeros((8, D_MODEL), F32).at[0].set(c_ctx).at[1:1 + n_s].set(c)
    mods = _adaln(cvec, w_ada, b_ada).reshape(DEPTH * 8, 1, 3 * D_MODEL)
    cos, sin = _rope_tables(t_s)

    y_p = x_prompt.reshape(n_p * t_p, D_MODEL)
    y_s = x_sample.reshape(n_s * t_s, D_MODEL)
    n_even, n_odd = w_in_even.shape[0], w_in_odd.shape[0]
    wt_even = jnp.swapaxes(w_in_even, 1, 2)
    wt_odd = jnp.swapaxes(w_in_odd, 1, 2)
    gdn_states = new_k = new_v = gla_states = None

    def gidx(layer):
        return _group_index(layer, 0, None), _group_index(layer, 1, t_s // ROW_TILE)

    gidx_p, gidx_s = gidx(0)
    h_p = _prenorm(y_p, norm_pre[0], mods, gidx_p)
    h_s = _prenorm(y_s, norm_pre[0], mods, gidx_s)

    for layer in range(DEPTH):
        gidx_p, gidx_s = gidx(layer)
        if layer + 1 < DEPTH:
            next_gain = norm_pre[layer + 1]
            gnext_p, gnext_s = gidx(layer + 1)
        else:
            next_gain = gnext_p = gnext_s = None
        if layer % 2 == 0:
            e = layer // 2
            lam_init = 0.8 - 0.6 * math.exp(-0.3 * layer)
            w_out = w_out_even[e].astype(BF16)
            zeros16 = jnp.zeros((2 * H_A,), F32)
            alog_row = _pad_lanes(jnp.concatenate([zeros16, a_log_even[e].reshape(-1)])[None, :])
            dtb_row = _pad_lanes(jnp.concatenate([zeros16, dt_bias_even[e].reshape(-1)])[None, :])
            gnorm = gdn_norm_even[e][None, :]
            dgain = diff_norm_even[e][None, :]
            col_b = 4 * wa
            shift_b = 4 * H_A

            pa_p, pa_s = _inproj2(h_p, h_s, wt_even, e, 0, 4 * wa, IN_TILE, 1024)
            gp_p, gp_s = _inproj2(h_p, h_s, wt_even, e, col_b, LANES, IN_TILE, LANES)
            qb_p, qb_s = _inproj2(h_p, h_s, wt_even, e, col_b, wa, IN_TILE, wa, shift_b, out_dtype=BF16)
            zb_p, zb_s = _inproj2(h_p, h_s, wt_even, e, col_b + 3 * wa, wa, IN_TILE, wa, shift_b)
            kv_s = _inproj(h_s, wt_even, e, col_b + wa, 2 * wa, IN_TILE, wa, shift_b, out_dtype=BF16)
            kv_tile = IN_TILE if new_k is not None else IN_TILE // n_even
            new_k = _inproj(h_p, wt_even, e, col_b + wa, wa, kv_tile, wa, shift_b, (new_k, e, t_p))
            new_v = _inproj(h_p, wt_even, e, col_b + 2 * wa, wa, kv_tile, wa, shift_b, (new_v, e, t_p))
            oa_p, gdn_states = _gdn(pa_p, gp_p, conv_even[e], alog_row, dtb_row, gnorm, None,
                                    (gdn_states, n_even), e, n_p, t_p)
            ob_p = _attn((qb_p, 0), (new_k, e), (new_v, e), (zb_p, 0), lam_even[e], dgain, lam_init,
                         None, e, n_p, t_p)
            out_p = _outproj([oa_p, ob_p], w_out, y_p, norm_post[layer], mods, gidx_p, next_gain, gnext_p)

            oa_s = _gdn(pa_s, gp_s, conv_even[e], alog_row, dtb_row, gnorm, state_gdn, None, e, n_s, t_s)
            ob_s = _attn((qb_s, 0), (kv_s, 0), (kv_s, 1), (zb_s, 0), lam_even[e], dgain, lam_init,
                         (cos, sin, cache_k, cache_v), e, n_s, t_s)
            out_s = _outproj([oa_s, ob_s], w_out, y_s, norm_post[layer], mods, gidx_s, next_gain, gnext_s)
        else:
            od = layer // 2
            n_main = 2 * H_C * DK_C + 2 * H_C * DV_C
            w_out = w_out_odd[od].astype(BF16)
            wg_pad = jnp.zeros((2, LANES, H_C * DK_C), F32)
            for d in range(2):
                wg_pad = wg_pad.at[d, d * GATE_RANK:(d + 1) * GATE_RANK].set(w_gate_odd[od, d])
            bgate = b_gate_odd[od].reshape(2, 1, H_C * DK_C)
            ggain = gla_norm_odd[od][None, :]

            n_qkv = 2 * H_C * DK_C + H_C * DV_C
            qkv_p, qkv_s = _inproj2(h_p, h_s, wt_odd, od, 0, n_qkv, IN_TILE, 1024, out_dtype=BF16)
            z_p, z_s = _inproj2(h_p, h_s, wt_odd, od, n_qkv, n_main - n_qkv, IN_TILE, 1024)
            glr_p, glr_s = _inproj2(h_p, h_s, wt_odd, od, n_main, LANES, IN_TILE, LANES)
            o_p, gla_states = _gla(qkv_p, z_p, glr_p, wg_pad, bgate, ggain, None, (gla_states, n_odd),
                                   od, n_p, t_p)
            out_p = _outproj([o_p], w_out, y_p, norm_post[layer], mods, gidx_p, next_gain, gnext_p)

            o_s = _gla(qkv_s, z_s, glr_s, wg_pad, bgate, ggain, state_gla, None, od, n_s, t_s)
            out_s = _outproj([o_s], w_out, y_s, norm_post[layer], mods, gidx_s, next_gain, gnext_s)
        if next_gain is None:
            y_p, y_s = out_p, out_s
        else:
            (y_p, h_p), (y_s, h_s) = out_p, out_s

    return (y_p.reshape(n_p, t_p, D_MODEL),
            y_s.reshape(n_s, t_s, D_MODEL),
            gdn_states,
            new_k.reshape(n_p, n_even, t_p, H_B, 2 * DQK_B),
            new_v.reshape(n_p, n_even, t_p, H_B, DV_B),
            gla_states)
```

```python
import functools
import math

import jax
import jax.numpy as jnp
from jax import lax
from jax.experimental import pallas as pl
from jax.experimental.pallas import tpu as pltpu

F32 = jnp.float32
BF16 = jnp.bfloat16
HIGHEST = lax.Precision.HIGHEST

D_MODEL = 2048
DEPTH = 4
EPS = 1e-6
GRID_W = 64
H_A = 8
DK_A = 128
DV_A = 128
CONV_K = 5
CHUNK_A = 64
PREP_CHAINS = 16
GDN_MAX_HEADS = 4
H_B = 8
DQK_B = 64
DV_B = 128
ROPE_BASE = 10000.0
Q_TILE = 256
Q_TILE_CTX = 256
ATTN_HEADS = 4
ATTN_HEADS_CTX = 2
H_C = 4
DK_C = 256
DV_C = 512
GATE_RANK = 16
GLA_NORMALIZER = 16.0
CHUNK_C = 128
GLA_MAX_HEADS = 2

LOG2E = math.log2(math.e)
LANES = 128
VMEM_LIMIT = 48 * 1024 * 1024
VMEM_MARGIN = 4 * 1024 * 1024
ROW_TILE = 512
OUT_SUBTILES = 4
IN_TILE = 1024


def _params(*semantics, vmem=VMEM_LIMIT):
    return pltpu.CompilerParams(dimension_semantics=semantics, vmem_limit_bytes=vmem)


def _mm(a, b):
    return jnp.dot(a.astype(BF16), b.astype(BF16), preferred_element_type=F32)


def _mm_nt(a, b):
    return lax.dot_general(a.astype(BF16), b.astype(BF16),
                           (((1,), (1,)), ((), ())), preferred_element_type=F32)


def _dot_nt(a, b):
    return lax.dot_general(a, b, (((1,), (1,)), ((), ())), preferred_element_type=F32)


def _mm_tn(a, b):
    return lax.dot_general(a.astype(BF16), b.astype(BF16),
                           (((0,), (0,)), ((), ())), preferred_element_type=F32)


def _mm_hi(a, b):
    return jnp.dot(a, b, precision=HIGHEST, preferred_element_type=F32)


def _mm_tn_hi(a, b):
    return lax.dot_general(a, b, (((0,), (0,)), ((), ())), precision=HIGHEST,
                           preferred_element_type=F32)


def _aligned(start, size, align):
    if isinstance(start, int):
        return pl.ds(start, size)
    return pl.ds(pl.multiple_of(start, align), size)


def _store_state(sf_ref, fill_slot, d, hd, state):
    if fill_slot is None:
        sf_ref[d, hd] = state
    else:
        for s in range(sf_ref.shape[0]):
            sf_ref[s, d, hd] = state if s == fill_slot else jnp.zeros_like(state)


def _silu(x):
    return x * jax.nn.sigmoid(x)


def _softplus(x):
    return jnp.maximum(x, 0.0) + jnp.log1p(jnp.exp(-jnp.abs(x)))


def _rms(x):
    return x * lax.rsqrt(jnp.mean(x * x, axis=-1, keepdims=True) + EPS)


def _adaln_kernel(c_ref, w_ref, b_ref, o_ref):
    o_ref[...] = _mm(_silu(c_ref[...]), w_ref[...]) + b_ref[...]


def _adaln(cvec, w_ada, b_ada):
    tn = 1024
    n3 = 3 * D_MODEL
    return pl.pallas_call(
        _adaln_kernel,
        grid=(DEPTH, n3 // tn),
        in_specs=[pl.BlockSpec((8, D_MODEL), lambda l, j: (0, 0)),
                  pl.BlockSpec((None, D_MODEL, tn), lambda l, j: (l, 0, j)),
                  pl.BlockSpec((None, 1, tn), lambda l, j: (l, 0, j))],
        out_specs=pl.BlockSpec((None, 8, tn), lambda l, j: (l, 0, j)),
        out_shape=jax.ShapeDtypeStruct((DEPTH, 8, n3), F32),
        compiler_params=_params("parallel", "parallel"),
        name="adaln",
    )(cvec, w_ada, b_ada.reshape(DEPTH, 1, n3))


def _prenorm_kernel(x_ref, g_ref, m_ref, h_ref):
    y = _rms(x_ref[...]) * g_ref[...]
    m = m_ref[...]
    h_ref[...] = (y * (1.0 + m[:, D_MODEL:2 * D_MODEL]) + m[:, :D_MODEL]).astype(BF16)


def _group_index(layer, group0, tiles_per_group):
    if tiles_per_group is None:
        return lambda i: (layer * 8 + group0, 0, 0)
    return lambda i: (layer * 8 + group0 + i // tiles_per_group, 0, 0)


def _prenorm(y, gain, mods, gidx):
    m = y.shape[0]
    return pl.pallas_call(
        _prenorm_kernel,
        grid=(m // ROW_TILE,),
        in_specs=[pl.BlockSpec((ROW_TILE, D_MODEL), lambda i: (i, 0)),
                  pl.BlockSpec((1, D_MODEL), lambda i: (0, 0)),
                  pl.BlockSpec((None, 1, 3 * D_MODEL), gidx)],
        out_specs=pl.BlockSpec((ROW_TILE, D_MODEL), lambda i: (i, 0)),
        out_shape=jax.ShapeDtypeStruct((m, D_MODEL), BF16),
        compiler_params=_params("parallel"),
        name="prenorm",
    )(y, gain.reshape(1, D_MODEL), mods)


def _cast_weight_block(wa_ref, wb_ref, wbf, shift, valid):
    @pl.when(pl.program_id(1) == 0)
    def _():
        w = wa_ref[...]
        if shift:
            w = jnp.concatenate([w[shift:], wb_ref[...]], axis=0)
        if valid is not None:
            w = jnp.where(lax.broadcasted_iota(jnp.int32, w.shape, 0) < valid, w, 0.0)
        wbf[...] = w.astype(BF16)


def _inproj2_kernel(*refs, shift, valid, tiles_a):
    xa_ref, xb_ref, wa_ref = refs[:3]
    wb_ref = refs[3] if shift else None
    oa_ref, ob_ref, wbf = refs[-3:]
    _cast_weight_block(wa_ref, wb_ref, wbf, shift, valid)

    @pl.when(pl.program_id(1) < tiles_a)
    def _():
        oa_ref[...] = _dot_nt(xa_ref[...], wbf[...]).astype(oa_ref.dtype)

    @pl.when(pl.program_id(1) >= tiles_a)
    def _():
        ob_ref[...] = _dot_nt(xb_ref[...], wbf[...]).astype(ob_ref.dtype)


def _inproj_kernel(*refs, shift, valid, fill_slot):
    x_ref, wa_ref = refs[:2]
    wb_ref = refs[2] if shift else None
    o_ref, wbf = refs[-2:]
    _cast_weight_block(wa_ref, wb_ref, wbf, shift, valid)
    res = _dot_nt(x_ref[...], wbf[...])
    if fill_slot is None:
        o_ref[...] = res.reshape(o_ref.shape).astype(o_ref.dtype)
    else:
        for s in range(o_ref.shape[1]):
            blk = (o_ref.shape[0],) + o_ref.shape[2:]
            o_ref[:, s] = res.reshape(blk) if s == fill_slot else jnp.zeros(blk, F32)


def _inproj(x, wt3, layer_idx, row0, n_out, tm, tn, shift=0, cache_slot=None, out_dtype=F32):
    m, k = x.shape
    n_rows = wt3.shape[1]
    valid = n_rows - row0 if row0 + n_out > n_rows else None
    assert valid is None or (n_out == tn and not shift)
    in_specs = [pl.BlockSpec((tm, k), lambda j, i: (i, 0)),
                pl.BlockSpec((None, tn, k), lambda j, i: (layer_idx, row0 // tn + j, 0))]
    args = [x, wt3]
    if shift:
        in_specs.append(pl.BlockSpec((None, shift, k),
                                     lambda j, i: (layer_idx, (row0 + (j + 1) * tn) // shift, 0)))
        args.append(wt3)
    aliases = {}
    fill_slot = None
    if cache_slot is None:
        out_spec = pl.BlockSpec((tm, tn), lambda j, i: (i, j))
        out_shape = jax.ShapeDtypeStruct((m, n_out), out_dtype)
    else:
        prev, slot, t = cache_slot
        n_slots = wt3.shape[0]
        out_shape = jax.ShapeDtypeStruct((m // t, n_slots, t, n_out), F32)
        if prev is None:
            fill_slot = slot
            out_spec = pl.BlockSpec((tm // t, n_slots, t, tn), lambda j, i: (i, 0, 0, 0))
        else:
            out_spec = pl.BlockSpec((tm // t, None, t, tn), lambda j, i: (i, slot, 0, 0))
            in_specs.append(pl.BlockSpec(memory_space=pl.ANY))
            args.append(prev)
            aliases = {len(args) - 1: 0}

    return pl.pallas_call(
        functools.partial(_inproj_kernel, shift=shift, valid=valid, fill_slot=fill_slot),
        grid=(n_out // tn, m // tm),
        in_specs=in_specs,
        out_specs=out_spec,
        out_shape=out_shape,
        scratch_shapes=[pltpu.VMEM((tn, k), BF16)],
        input_output_aliases=aliases,
        compiler_params=_params("parallel", "arbitrary"),
        name="inproj",
    )(*args)


def _inproj2(xa, xb, wt3, layer_idx, row0, n_out, tm, tn, shift=0, out_dtype=F32):
    (ma, k), mb = xa.shape, xb.shape[0]
    ta, tb = ma // tm, mb // tm
    n_rows = wt3.shape[1]
    valid = n_rows - row0 if row0 + n_out > n_rows else None
    assert valid is None or (n_out == tn and not shift)

    def a_tile(i):
        return jnp.minimum(i, ta - 1)

    def b_tile(i):
        return jnp.maximum(i - ta, 0)

    in_specs = [pl.BlockSpec((tm, k), lambda j, i: (a_tile(i), 0)),
                pl.BlockSpec((tm, k), lambda j, i: (b_tile(i), 0)),
                pl.BlockSpec((None, tn, k), lambda j, i: (layer_idx, row0 // tn + j, 0))]
    args = [xa, xb, wt3]
    if shift:
        in_specs.append(pl.BlockSpec((None, shift, k),
                                     lambda j, i: (layer_idx, (row0 + (j + 1) * tn) // shift, 0)))
        args.append(wt3)
    return pl.pallas_call(
        functools.partial(_inproj2_kernel, shift=shift, valid=valid, tiles_a=ta),
        grid=(n_out // tn, ta + tb),
        in_specs=in_specs,
        out_specs=(pl.BlockSpec((tm, tn), lambda j, i: (a_tile(i), j)),
                   pl.BlockSpec((tm, tn), lambda j, i: (b_tile(i), j))),
        out_shape=(jax.ShapeDtypeStruct((ma, n_out), out_dtype),
                   jax.ShapeDtypeStruct((mb, n_out), out_dtype)),
        scratch_shapes=[pltpu.VMEM((tn, k), BF16)],
        compiler_params=_params("parallel", "arbitrary",
                                vmem=4 * tm * k * 2 + 4 * tm * tn * 4 + 2 * tn * k * 4 + tn * k * 2
                                + 2 * shift * k * 4 + VMEM_MARGIN),
        name="inproj2",
    )(*args)


def _outproj_kernel(*refs, n_in, emit_h):
    x_refs = refs[:n_in]
    w_ref, y_ref, g_ref, m_ref = refs[n_in:n_in + 4]
    if emit_h:
        gn_ref, mn_ref, o_ref, h_ref = refs[n_in + 4:]
    else:
        o_ref = refs[n_in + 4]
    m = m_ref[...]
    sub = o_ref.shape[0] // OUT_SUBTILES
    accs = []
    for s in range(OUT_SUBTILES):
        rows = slice(s * sub, (s + 1) * sub)
        acc = None
        off = 0
        for x_ref in x_refs:
            kk = x_ref.shape[1]
            part = jnp.dot(x_ref[rows, :], w_ref[off:off + kk, :], preferred_element_type=F32)
            acc = part if acc is None else acc + part
            off += kk
        accs.append(acc)
    for s in range(OUT_SUBTILES):
        rows = slice(s * sub, (s + 1) * sub)
        y = y_ref[rows, :] + m[:, 2 * D_MODEL:] * (_rms(accs[s]) * g_ref[...])
        if emit_h:
            mn = mn_ref[...]
            h_ref[rows, :] = (_rms(y) * gn_ref[...] * (1.0 + mn[:, D_MODEL:2 * D_MODEL])
                              + mn[:, :D_MODEL]).astype(BF16)
        o_ref[rows, :] = y


def _outproj(xs, w, y, gain, mods, gidx, next_gain=None, gidx_next=None):
    m = y.shape[0]
    emit_h = next_gain is not None
    row_spec = pl.BlockSpec((ROW_TILE, D_MODEL), lambda i: (i, 0))
    vec_spec = pl.BlockSpec((1, D_MODEL), lambda i: (0, 0))
    in_specs = [pl.BlockSpec((ROW_TILE, x.shape[1]), lambda i: (i, 0)) for x in xs]
    in_specs += [pl.BlockSpec(w.shape, lambda i: (0, 0)), row_spec, vec_spec,
                 pl.BlockSpec((None, 1, 3 * D_MODEL), gidx)]
    args = [*xs, w, y, gain.reshape(1, D_MODEL), mods]
    out_specs = row_spec
    out_shape = jax.ShapeDtypeStruct((m, D_MODEL), F32)
    if emit_h:
        in_specs += [vec_spec, pl.BlockSpec((None, 1, 3 * D_MODEL), gidx_next)]
        args += [next_gain.reshape(1, D_MODEL), mods]
        out_specs = (row_spec, row_spec)
        out_shape = (out_shape, jax.ShapeDtypeStruct((m, D_MODEL), BF16))
    return pl.pallas_call(
        functools.partial(_outproj_kernel, n_in=len(xs), emit_h=emit_h),
        grid=(m // ROW_TILE,),
        in_specs=in_specs,
        out_specs=out_specs,
        out_shape=out_shape,
        compiler_params=_params("parallel"),
        name="outproj",
    )(*args)


def _gdn_kernel(*refs, seq_len, heads, has_state, fill_slot):
    t = seq_len
    c = CHUNK_A
    nc = t // c
    (q_ref, k_ref, v_ref, z_ref, gp_ref, cwq_ref, cwk_ref, cwv_ref,
     alog_ref, dtb_ref, gn_ref) = refs[:11]
    qn, kn, vn, bt, gt, wv_s, wkqd_s, qk_s, kd_s, gl_s, oacc, s_s, xpad = refs[-13:]
    if has_state:
        s0_ref, o_ref = refs[11], refs[-14]
    else:
        o_ref, sf_ref = refs[-15:-13]
    head0 = pl.program_id(1) * heads

    def lanes_of(hd):
        return slice(hd * LANES, (hd + 1) * LANES)

    row_t = lax.broadcasted_iota(jnp.int32, (t, LANES), 0)
    xpad[0:8, :] = jnp.zeros((8, LANES), F32)
    xpad[8 + t:, :] = jnp.zeros((8, LANES), F32)

    def conv_silu(x_ref, w_ref, hd):
        w = w_ref[:, lanes_of(hd)]
        xpad[8:8 + t, :] = x_ref[:, lanes_of(hd)]
        acc = None
        for tap in range(CONV_K):
            start = 8 + tap - CONV_K // 2
            term = xpad[start:start + t, :] * w[tap:tap + 1, :]
            acc = term if acc is None else acc + term
        return _silu(acc)

    def l2n(x):
        return x * lax.rsqrt(jnp.sum(x * x, axis=-1, keepdims=True) + EPS)

    for hd in range(heads):
        qn[hd] = l2n(conv_silu(q_ref, cwq_ref, hd))
        kn[hd] = l2n(conv_silu(k_ref, cwk_ref, hd))
        vn[hd] = conv_silu(v_ref, cwv_ref, hd)

    gp = gp_ref[...]
    lane_t = lax.broadcasted_iota(jnp.int32, (t, LANES), 1)
    beta_all = jax.nn.sigmoid(gp)
    g_all = -jnp.exp(alog_ref[...]) * _softplus(gp + dtb_ref[...])

    def pick(x, idx):
        col = jnp.sum(jnp.where(lane_t == idx, x, 0.0), axis=-1, keepdims=True)
        return jnp.broadcast_to(col, (t, LANES))

    pos_t = row_t & (c - 1)
    g_cum = [g_all, g_all]
    sh = 1
    while sh < c:
        g_cum[0] = g_cum[0] + jnp.where(pos_t >= sh, pltpu.roll(g_cum[0], sh, axis=0), 0.0)
        g_cum[1] = g_cum[1] + jnp.where(pos_t < c - sh, pltpu.roll(g_cum[1], t - sh, axis=0), 0.0)
        sh *= 2

    for hd in range(heads):
        for d in range(2):
            bt[hd * 2 + d] = pick(beta_all, d * H_A + head0 + hd)
            gt[hd * 2 + d] = pick(g_cum[d], 2 * H_A + d * H_A + head0 + hd)

    r2 = 2 * c
    row = lax.broadcasted_iota(jnp.int32, (r2, r2), 0)
    col = lax.broadcasted_iota(jnp.int32, (r2, r2), 1)
    rc_xor = row ^ col

    tri_inc, tri_str = [], []
    for rev in (False, True):
        after = (col - row) if rev else (row - col)
        after = jnp.where(rc_xor < c, after, -1)
        tri_inc.append(after >= 0)
        tri_str.append(after > 0)
    eye = jnp.where(row == col, 1.0, 0.0)

    def prep(chains):
        n = range(len(chains))
        rows = [_aligned(gi * r2, r2, r2) for _, gi, _ in chains]
        dirs = [d for _, _, d in chains]
        hdir = [hd * 2 + d for hd, _, d in chains]
        q = [qn[chains[i][0], rows[i], :] * DK_A ** -0.5 for i in n]
        k = [kn[chains[i][0], rows[i], :] for i in n]
        beta = [bt[hdir[i], rows[i], :] for i in n]
        kb = [k[i] * beta[i] for i in n]
        kk = [_mm_nt(kb[i], k[i]) for i in n]
        qk = [_mm_nt(q[i], k[i]) for i in n]
        gcum = [gt[hdir[i], rows[i], :] for i in n]
        decay = [jnp.where(tri_inc[dirs[i]], jnp.exp(jnp.minimum(gcum[i] - gcum[i].T, 0.0)), 0.0) for i in n]
        amat = [jnp.where(tri_str[dirs[i]], kk[i] * decay[i], 0.0) for i in n]
        tinv = [eye - jnp.where(rc_xor < 2, amat[i], 0.0) for i in n]
        b = 2
        while b < c:
            level = (rc_xor >= b) & (rc_xor < 2 * b)
            m1 = [_mm(jnp.where(level, amat[i], 0.0), tinv[i]) for i in n]
            m2 = [_mm(tinv[i], m1[i]) for i in n]
            tinv = [tinv[i] - m2[i] for i in n]
            b *= 2
        eg = [jnp.exp(gcum[i]) for i in n]
        x = [_mm(tinv[i], jnp.concatenate([vn[chains[i][0], rows[i], :] * beta[i], kb[i] * eg[i]], axis=1))
             for i in n]
        for i in n:
            _, gi, d = chains[i]
            hd2 = hdir[i]
            qkd = qk[i] * decay[i]
            g_end = _block_row_bcast(gcum[i], c, 0 if d == 1 else c - 1)
            qd = q[i] * eg[i]
            gl = jnp.exp(g_end)
            wv_s[hd2, rows[i], :] = x[i][:, :DV_A]
            qk_s[hd2, rows[i], :] = jnp.where(row < c, qkd, pltpu.roll(qkd, c, axis=1))[:, :c]
            kd_s[hd2, rows[i], :] = k[i] * jnp.exp(g_end - gcum[i])
            for j in range(2):
                base = gi * 2 * r2 + j * r2
                wkqd_s[hd2, _aligned(base, c, c), :] = x[i][j * c:(j + 1) * c, DV_A:]
                wkqd_s[hd2, _aligned(base + c, c, c), :] = qd[j * c:(j + 1) * c, :]
                gl_s[hd2, _aligned(gi * 16 + j * 8, 8, 8), :] = gl[j * c:j * c + 8, :]

    def rec(chains):
        n = range(len(chains))
        rows = [_aligned(ci * c, c, c) for _, ci, _ in chains]
        hdir = [hd * 2 + d for hd, _, d in chains]
        s = [s_s[hdir[i]] for i in n]
        ws = [_mm(wkqd_s[hdir[i], _aligned(chains[i][1] * 2 * c, 2 * c, 2 * c), :], s[i]) for i in n]
        u = [wv_s[hdir[i], rows[i], :] - ws[i][:c] for i in n]
        o2 = [_mm(qk_s[hdir[i], rows[i], :], u[i]) for i in n]
        ds = [_mm_tn(kd_s[hdir[i], rows[i], :], u[i]) for i in n]
        for i in n:
            s_s[hdir[i]] = s[i] * gl_s[hdir[i], _aligned(chains[i][1] * 8, 1, 8), :] + ds[i]
        for i in n:
            oacc[chains[i][0], rows[i], :] += ws[i][c:] + o2[i]

    for hd in range(heads):
        for d in range(2):
            if has_state:
                s_s[hd * 2 + d] = s0_ref[d, hd]
            else:
                s_s[hd * 2 + d] = jnp.zeros((DK_A, DV_A), F32)
    oacc[...] = jnp.zeros((heads, t, DV_A), F32)

    n_groups = nc // 2
    unroll = max(1, min(n_groups, PREP_CHAINS // (2 * heads)))

    def prep_body(gi, carry):
        prep([(hd, gi * unroll + j, d) for hd in range(heads) for j in range(unroll) for d in range(2)])
        return carry

    if n_groups == unroll:
        prep_body(0, 0)
    else:
        lax.fori_loop(0, n_groups // unroll, prep_body, 0)

    def rec_body(i, carry):
        rec([(hd, i if d == 0 else nc - 1 - i, d) for hd in range(heads) for d in range(2)])
        return carry

    lax.fori_loop(0, nc, rec_body, 0)

    for hd in range(heads):
        o_ref[:, lanes_of(hd)] = (_rms(oacc[hd]) * gn_ref[...] * _silu(z_ref[:, lanes_of(hd)])).astype(BF16)
        if not has_state:
            for d in range(2):
                _store_state(sf_ref, fill_slot, d, hd, s_s[hd * 2 + d])


def _gdn_heads(t):
    for g in (GDN_MAX_HEADS, 2, 1):
        scratch = g * t * LANES * 4 * 17 + 2 * g * DK_A * DV_A * 4
        blocks = 2 * (4 * t * g * LANES * 4 + t * LANES * 4 + t * g * LANES * 2)
        if scratch + blocks <= VMEM_LIMIT - (8 << 20):
            return g
    return 1


def _gdn(proj, gp, conv_w, alog_row, dtb_row, gnorm, state, state_out, layer_e, nseq, seq_len):
    t = seq_len
    nc = t // CHUNK_A
    has_state = state is not None
    aliases = {}
    fill_slot = None

    g = _gdn_heads(t)
    w = g * LANES
    nb = H_A // g

    def col(seg):
        return pl.BlockSpec((t, w), lambda b, h: (b, seg * nb + h))

    def cw(seg):
        return pl.BlockSpec((CONV_K, w), lambda b, h: (0, seg * nb + h))

    row_spec = pl.BlockSpec((1, LANES), lambda b, h: (0, 0))
    in_specs = [col(0), col(1), col(2), col(3),
                pl.BlockSpec((t, LANES), lambda b, h: (b, 0)),
                cw(0), cw(1), cw(2), row_spec, row_spec, row_spec]
    args = [proj, proj, proj, proj, gp, conv_w, conv_w, conv_w, alog_row, dtb_row, gnorm]
    o_shape = jax.ShapeDtypeStruct((nseq * t, H_A * DV_A), BF16)
    o_spec = pl.BlockSpec((t, w), lambda b, h: (b, h))
    if has_state:
        in_specs.append(pl.BlockSpec((None, None, 2, g, DK_A, DV_A),
                                     lambda b, h: (b, layer_e, 0, h, 0, 0)))
        args.append(state)
        out_shape, out_specs = o_shape, o_spec
    else:
        prev, n_slots = state_out
        out_shape = (o_shape, jax.ShapeDtypeStruct((nseq, n_slots, 2, H_A, DK_A, DV_A), F32))
        if prev is None:
            fill_slot = layer_e
            s_spec = pl.BlockSpec((None, n_slots, 2, g, DK_A, DV_A), lambda b, h: (b, 0, 0, h, 0, 0))
        else:
            s_spec = pl.BlockSpec((None, None, 2, g, DK_A, DV_A), lambda b, h: (b, layer_e, 0, h, 0, 0))
            in_specs.append(pl.BlockSpec(memory_space=pl.ANY))
            args.append(prev)
            aliases = {len(args) - 1: 1}
        out_specs = (o_spec, s_spec)
    scratch = [pltpu.VMEM((g, t, LANES), F32)] * 3
    scratch += [pltpu.VMEM((2 * g, t, LANES), F32)] * 2
    scratch += [pltpu.VMEM((2 * g, t, LANES), F32),
                pltpu.VMEM((2 * g, 2 * t, LANES), F32),
                pltpu.VMEM((2 * g, t, CHUNK_A), F32),
                pltpu.VMEM((2 * g, t, LANES), F32),
                pltpu.VMEM((2 * g, nc * 8, LANES), F32),
                pltpu.VMEM((g, t, LANES), F32),
                pltpu.VMEM((2 * g, DK_A, DV_A), F32),
                pltpu.VMEM((t + 16, LANES), F32)]
    return pl.pallas_call(
        functools.partial(_gdn_kernel, seq_len=t, heads=g, has_state=has_state, fill_slot=fill_slot),
        grid=(nseq, nb),
        in_specs=in_specs,
        out_specs=out_specs,
        out_shape=out_shape,
        scratch_shapes=scratch,
        input_output_aliases=aliases,
        compiler_params=_params("parallel", "parallel"),
        name="gdn",
    )(*args)


def _rope(x, cos, sin):
    x = x.astype(F32)
    lane = lax.broadcasted_iota(jnp.int32, x.shape, 1)
    quarter = DQK_B // 4
    rot = jnp.where((lane & quarter) == 0,
                    -pltpu.roll(x, LANES - quarter, axis=1),
                    pltpu.roll(x, quarter, axis=1))
    return x * cos + rot * sin


def _attn_kernel(*refs, seq_len, heads, lam_init, has_ctx):
    t = seq_len
    q_ref, k_ref, v_ref, z_ref, lam_ref, dg_ref = refs[:6]
    if has_ctx:
        cq_ref, sq_ref, ck_ref, sk_ref, kc_ref, vc_ref, o_ref, k_all, v_all = refs[6:]
    else:
        o_ref = refs[6]

    def lanes_of(hd):
        return slice(hd * LANES, (hd + 1) * LANES)

    if has_ctx:
        @pl.when(pl.program_id(2) == 0)
        def _():
            for hd in range(heads):
                k_all[hd, 0:t, :] = _rope(k_ref[:, lanes_of(hd)], ck_ref[...], sk_ref[...])
                k_all[hd, t:, :] = kc_ref[:, lanes_of(hd)]
                v_all[hd, 0:t, :] = v_ref[:, lanes_of(hd)].astype(F32)
                v_all[hd, t:, :] = vc_ref[:, lanes_of(hd)]
    lp = lam_ref[...]
    lam = (jnp.exp(jnp.sum(lp[0:1] * lp[1:2], axis=-1, keepdims=True))
           - jnp.exp(jnp.sum(lp[2:3] * lp[3:4], axis=-1, keepdims=True)) + lam_init)
    lane = lax.broadcasted_iota(jnp.int32, (q_ref.shape[0], LANES), 1)

    chains = [(hd, m) for hd in range(heads) for m in range(2)]
    n = range(len(chains))
    q = []
    for hd in range(heads):
        qh = q_ref[:, lanes_of(hd)]
        if has_ctx:
            qh = _rope(qh, cq_ref[...], sq_ref[...])
        q.append(qh * DQK_B ** -0.5)
    k = [k_all[hd] if has_ctx else k_ref[:, lanes_of(hd)] for hd in range(heads)]
    v = [v_all[hd] if has_ctx else v_ref[:, lanes_of(hd)] for hd in range(heads)]
    qm = [jnp.where(lane < DQK_B, q[hd], 0.0) if m == 0 else jnp.where(lane < DQK_B, 0.0, q[hd])
          for hd, m in chains]
    s = [_mm_nt(qm[i], k[chains[i][0]]) for i in n]
    e = [jnp.exp(s[i] - jnp.max(s[i], axis=-1, keepdims=True)) for i in n]
    den = [jnp.sum(e[i], axis=-1, keepdims=True) for i in n]
    ev = [_mm(e[i], v[chains[i][0]]) for i in n]
    for hd in range(heads):
        o = ev[2 * hd] / den[2 * hd] - lam * (ev[2 * hd + 1] / den[2 * hd + 1])
        o_ref[:, lanes_of(hd)] = (_rms(o) * dg_ref[...] * (1.0 - lam_init)
                                  * _silu(z_ref[:, lanes_of(hd)])).astype(BF16)


def _attn(q, k, v, z, lam_p, dgain, lam_init, ctx, layer_e, nseq, seq_len):
    t = seq_len
    has_ctx = ctx is not None
    g = ATTN_HEADS_CTX if has_ctx else ATTN_HEADS
    tq = min(t, Q_TILE_CTX if has_ctx else Q_TILE)
    nq = t // tq
    w = g * LANES
    nb = H_B // g

    def rows_spec(src, rows):
        arr, seg = src
        if arr.ndim == 4:
            assert rows == t
            return pl.BlockSpec((None, None, t, w), lambda b, h, i: (b, seg, 0, h))
        if rows == t:
            return pl.BlockSpec((t, w), lambda b, h, i: (b, seg * nb + h))
        return pl.BlockSpec((rows, w), lambda b, h, i: (b * nq + i, seg * nb + h))

    in_specs = [rows_spec(q, tq), rows_spec(k, t), rows_spec(v, t), rows_spec(z, tq),
                pl.BlockSpec((4, DQK_B), lambda b, h, i: (0, 0)),
                pl.BlockSpec((1, LANES), lambda b, h, i: (0, 0))]
    args = [q[0], k[0], v[0], z[0], lam_p, dgain]
    scratch = []
    if has_ctx:
        cos, sin, cache_k, cache_v = ctx
        past = cache_k.shape[2]
        in_specs += [pl.BlockSpec((tq, LANES), lambda b, h, i: (i, 0)),
                     pl.BlockSpec((tq, LANES), lambda b, h, i: (i, 0)),
                     pl.BlockSpec((t, LANES), lambda b, h, i: (0, 0)),
                     pl.BlockSpec((t, LANES), lambda b, h, i: (0, 0)),
                     pl.BlockSpec((None, None, past, w), lambda b, h, i: (b, layer_e, 0, h)),
                     pl.BlockSpec((None, None, past, w), lambda b, h, i: (b, layer_e, 0, h))]
        args += [cos, sin, cos, sin, cache_k.reshape(cache_k.shape[:3] + (-1,)),
                 cache_v.reshape(cache_v.shape[:3] + (-1,))]
        scratch = [pltpu.VMEM((g, t + past, LANES), F32)] * 2
    return pl.pallas_call(
        functools.partial(_attn_kernel, seq_len=t, heads=g, lam_init=lam_init, has_ctx=has_ctx),
        grid=(nseq, nb, nq),
        in_specs=in_specs,
        out_specs=pl.BlockSpec((tq, w), lambda b, h, i: (b * nq + i, h)),
        out_shape=jax.ShapeDtypeStruct((nseq * t, H_B * DV_B), BF16),
        scratch_shapes=scratch,
        compiler_params=_params("parallel", "parallel", "arbitrary"),
        name="diffattn",
    )(*args)


def _block_row_bcast(x, blk, r):
    n, w = x.shape
    if blk >= 8:
        x3 = x.reshape(n // blk, blk, w)
        return jnp.broadcast_to(x3[:, r:r + 1, :], x3.shape).reshape(n, w)
    x3 = x.reshape(n // 8, 8, w)
    sub = lax.broadcasted_iota(jnp.int32, x3.shape, 1)
    out = None
    for b in range(8 // blk):
        cand = jnp.broadcast_to(x3[:, b * blk + r:b * blk + r + 1, :], x3.shape)
        out = cand if out is None else jnp.where(sub >= b * blk, cand, out)
    return out.reshape(n, w)


def _gla_kernel(*refs, seq_len, heads, has_state, fill_slot):
    t = seq_len
    c = CHUNK_C
    nc = t // c
    q_ref, k_ref, v_ref, z_ref, glr_ref, wg_ref, bg_ref, gg_ref = refs[:8]
    la_s, oacc, st_s = refs[-3:]
    if has_state:
        s0_ref, o_ref = refs[8], refs[-4]
    else:
        o_ref, sf_ref = refs[-5:-3]

    def kcols(hd):
        return slice(hd * DK_C, (hd + 1) * DK_C)

    def vcols(hd):
        return slice(hd * DV_C, (hd + 1) * DV_C)

    glr = glr_ref[...]
    for hd in range(heads):
        for d in range(2):
            gate = _mm(glr, wg_ref[d, :, kcols(hd)]) + bg_ref[d, :, kcols(hd)]
            la_s[hd * 2 + d] = ((jnp.minimum(gate, 0.0) - jnp.log(1.0 + jnp.exp(-jnp.abs(gate))))
                                * (LOG2E / GLA_NORMALIZER))

    row = lax.broadcasted_iota(jnp.int32, (c, c), 0)
    col = lax.broadcasted_iota(jnp.int32, (c, c), 1)
    top = row ^ col
    sh = 1
    while sh < c:
        top = top | (top >> sh)
        sh *= 2
    top = (top + 1) >> 1
    pair_code = jnp.where(row > col, top, -top)
    tri = [jnp.where(row >= col, 1.0, 0.0).astype(BF16), jnp.where(row <= col, 1.0, 0.0).astype(BF16)]

    def chunks(chains):
        n = range(len(chains))
        rows = [_aligned(ci * c, c, c) for _, ci, _ in chains]
        rev = [d == 1 for _, _, d in chains]
        hdir = [hd * 2 + d for hd, _, d in chains]
        q = [q_ref[rows[i], kcols(chains[i][0])] * DK_C ** -0.5 for i in n]
        k = [k_ref[rows[i], kcols(chains[i][0])] for i in n]
        v = [v_ref[rows[i], vcols(chains[i][0])] for i in n]
        qb = [q[i].astype(BF16) for i in n]
        kb = [k[i].astype(BF16) for i in n]
        attn = [jnp.where(row == col, _dot_nt(qb[i], kb[i]), 0.0) for i in n]
        la = [la_s[hdir[i], rows[i], :] for i in n]
        la_hi = [la[i].astype(BF16) for i in n]
        la_lo = [(la[i] - la_hi[i].astype(F32)).astype(BF16) for i in n]
        bc = [jnp.dot(tri[chains[i][2]], la_hi[i], preferred_element_type=F32)
              + jnp.dot(tri[chains[i][2]], la_lo[i], preferred_element_type=F32) for i in n]
        s = c // 2
        while s >= 1:
            for i in n:
                bref = _block_row_bcast(bc[i], 2 * s, s if rev[i] else s - 1)
                e = jnp.exp2(-jnp.abs(bc[i] - bref)).astype(BF16)
                pairs = pair_code == (-s if rev[i] else s)
                attn[i] = attn[i] + jnp.where(pairs, _dot_nt(qb[i] * e, kb[i] * e), 0.0)
            s //= 2
        b_end = [bc[i][0:1, :] if rev[i] else bc[i][c - 1:c, :] for i in n]
        st = [st_s[hdir[i]] for i in n]
        o = [_mm(attn[i], v[i]) + _mm_nt(q[i] * jnp.exp2(bc[i]), st[i]) for i in n]
        upd = [_mm_tn(v[i], k[i] * jnp.exp2(b_end[i] - bc[i])) for i in n]
        for i in n:
            st_s[hdir[i]] = st[i] * jnp.exp2(b_end[i]) + upd[i]
        for i in n:
            oacc[chains[i][0], rows[i], :] += o[i]

    for hd in range(heads):
        for d in range(2):
            if has_state:
                st_s[hd * 2 + d] = s0_ref[d, hd].T
            else:
                st_s[hd * 2 + d] = jnp.zeros((DV_C, DK_C), F32)
    oacc[...] = jnp.zeros((heads, t, DV_C), F32)

    def body(i, carry):
        chunks([(hd, i if d == 0 else nc - 1 - i, d) for hd in range(heads) for d in range(2)])
        return carry

    lax.fori_loop(0, nc, body, 0)

    for hd in range(heads):
        o_ref[:, vcols(hd)] = (_rms(oacc[hd]) * gg_ref[...] * _silu(z_ref[:, vcols(hd)])).astype(BF16)
        if not has_state:
            for d in range(2):
                _store_state(sf_ref, fill_slot, d, hd, st_s[hd * 2 + d].T)


def _gla_heads(t):
    for g in (GLA_MAX_HEADS, 1):
        scratch = g * (2 * t * DK_C + t * DV_C + 2 * DV_C * DK_C) * 4
        blocks = 2 * (g * t * ((2 * DK_C + DV_C) * 2 + DV_C * 4) + t * LANES * 4 + g * t * DV_C * 2
                      + 2 * g * DK_C * DV_C * 4)
        if scratch + blocks <= VMEM_LIMIT - (12 << 20):
            return g
    return 1


def _gla(qkv, zproj, glr, wg_pad, bgate, ggain, state, state_out, layer_o, nseq, seq_len):
    t = seq_len
    has_state = state is not None
    aliases = {}
    fill_slot = None
    g = _gla_heads(t)
    nb = H_C // g
    wk, wv = g * DK_C, g * DV_C
    in_specs = [pl.BlockSpec((t, wk), lambda b, h: (b, h)),
                pl.BlockSpec((t, wk), lambda b, h: (b, nb + h)),
                pl.BlockSpec((t, wv), lambda b, h: (b, nb + h)),
                pl.BlockSpec((t, wv), lambda b, h: (b, h)),
                pl.BlockSpec((t, LANES), lambda b, h: (b, 0)),
                pl.BlockSpec((2, LANES, wk), lambda b, h: (0, 0, h)),
                pl.BlockSpec((2, 1, wk), lambda b, h: (0, 0, h)),
                pl.BlockSpec((1, DV_C), lambda b, h: (0, 0))]
    args = [qkv, qkv, qkv, zproj, glr, wg_pad, bgate, ggain]
    o_shape = jax.ShapeDtypeStruct((nseq * t, H_C * DV_C), BF16)
    o_spec = pl.BlockSpec((t, wv), lambda b, h: (b, h))
    if has_state:
        in_specs.append(pl.BlockSpec((None, None, 2, g, DK_C, DV_C),
                                     lambda b, h: (b, layer_o, 0, h, 0, 0)))
        args.append(state)
        out_shape, out_specs = o_shape, o_spec
    else:
        prev, n_slots = state_out
        out_shape = (o_shape, jax.ShapeDtypeStruct((nseq, n_slots, 2, H_C, DK_C, DV_C), F32))
        if prev is None:
            fill_slot = layer_o
            s_spec = pl.BlockSpec((None, n_slots, 2, g, DK_C, DV_C), lambda b, h: (b, 0, 0, h, 0, 0))
        else:
            s_spec = pl.BlockSpec((None, None, 2, g, DK_C, DV_C), lambda b, h: (b, layer_o, 0, h, 0, 0))
            in_specs.append(pl.BlockSpec(memory_space=pl.ANY))
            args.append(prev)
            aliases = {len(args) - 1: 1}
        out_specs = (o_spec, s_spec)
    scratch = [pltpu.VMEM((2 * g, t, DK_C), F32),
               pltpu.VMEM((g, t, DV_C), F32),
               pltpu.VMEM((2 * g, DV_C, DK_C), F32)]
    return pl.pallas_call(
        functools.partial(_gla_kernel, seq_len=t, heads=g, has_state=has_state, fill_slot=fill_slot),
        grid=(nseq, nb),
        in_specs=in_specs,
        out_specs=out_specs,
        out_shape=out_shape,
        scratch_shapes=scratch,
        input_output_aliases=aliases,
        compiler_params=_params("parallel", "parallel"),
        name="gla",
    )(*args)


def _rope_tables(n_tokens):
    rows = n_tokens // GRID_W
    row = jnp.repeat(jnp.arange(rows), GRID_W).astype(F32)
    col = jnp.tile(jnp.arange(GRID_W), rows).astype(F32)
    quarter = DQK_B // 4
    freqs = ROPE_BASE ** (-jnp.arange(quarter, dtype=F32) / quarter)
    ar = row[:, None] * freqs
    ac = col[:, None] * freqs
    ang = jnp.concatenate([ar, ar, ac, ac] * (LANES // DQK_B), axis=-1)
    return jnp.cos(ang), jnp.sin(ang)


def _pad_lanes(x):
    return jnp.pad(x, ((0, 0), (0, LANES - x.shape[1])))


def kernel(x_prompt, x_sample, c, state_gdn, cache_k, cache_v, state_gla, c_ctx, w_ada, b_ada, norm_pre, norm_post, w_in_even, conv_even, a_log_even, dt_bias_even, gdn_norm_even, lam_even, diff_norm_even, w_out_even, w_in_odd, w_gate_odd, b_gate_odd, gla_norm_odd, w_out_odd):
    n_p, t_p, _ = x_prompt.shape
    n_s, t_s, _ = x_sample.shape
    wa = H_A * DK_A

    cvec = jnp.zeros((8, D_MODEL), F32).at[0].set(c_ctx).at[1:1 + n_s].set(c)
    mods = _adaln(cvec, w_ada, b_ada).reshape(DEPTH * 8, 1, 3 * D_MODEL)
    cos, sin = _rope_tables(t_s)

    y_p = x_prompt.reshape(n_p * t_p, D_MODEL)
    y_s = x_sample.reshape(n_s * t_s, D_MODEL)
    n_even, n_odd = w_in_even.shape[0], w_in_odd.shape[0]
    wt_even = jnp.swapaxes(w_in_even, 1, 2)
    wt_odd = jnp.swapaxes(w_in_odd, 1, 2)
    gdn_states = new_k = new_v = gla_states = None

    def gidx(layer):
        return _group_index(layer, 0, None), _group_index(layer, 1, t_s // ROW_TILE)

    gidx_p, gidx_s = gidx(0)
    h_p = _prenorm(y_p, norm_pre[0], mods, gidx_p)
    h_s = _prenorm(y_s, norm_pre[0], mods, gidx_s)

    for layer in range(DEPTH):
        gidx_p, gidx_s = gidx(layer)
        if layer + 1 < DEPTH:
            next_gain = norm_pre[layer + 1]
            gnext_p, gnext_s = gidx(layer + 1)
        else:
            next_gain = gnext_p = gnext_s = None
        if layer % 2 == 0:
            e = layer // 2
            lam_init = 0.8 - 0.6 * math.exp(-0.3 * layer)
            w_out = w_out_even[e].astype(BF16)
            zeros16 = jnp.zeros((2 * H_A,), F32)
            alog_row = _pad_lanes(jnp.concatenate([zeros16, a_log_even[e].reshape(-1)])[None, :])
            dtb_row = _pad_lanes(jnp.concatenate([zeros16, dt_bias_even[e].reshape(-1)])[None, :])
            gnorm = gdn_norm_even[e][None, :]
            dgain = diff_norm_even[e][None, :]
            col_b = 4 * wa
            shift_b = 4 * H_A

            pa_p, pa_s = _inproj2(h_p, h_s, wt_even, e, 0, 4 * wa, IN_TILE, 1024)
            gp_p, gp_s = _inproj2(h_p, h_s, wt_even, e, col_b, LANES, IN_TILE, LANES)
            qb_p, qb_s = _inproj2(h_p, h_s, wt_even, e, col_b, wa, IN_TILE, wa, shift_b, out_dtype=BF16)
            zb_p, zb_s = _inproj2(h_p, h_s, wt_even, e, col_b + 3 * wa, wa, IN_TILE, wa, shift_b)
            kv_s = _inproj(h_s, wt_even, e, col_b + wa, 2 * wa, IN_TILE, wa, shift_b, out_dtype=BF16)
            kv_tile = IN_TILE if new_k is not None else IN_TILE // n_even
            new_k = _inproj(h_p, wt_even, e, col_b + wa, wa, kv_tile, wa, shift_b, (new_k, e, t_p))
            new_v = _inproj(h_p, wt_even, e, col_b + 2 * wa, wa, kv_tile, wa, shift_b, (new_v, e, t_p))
            oa_p, gdn_states = _gdn(pa_p, gp_p, conv_even[e], alog_row, dtb_row, gnorm, None,
                                    (gdn_states, n_even), e, n_p, t_p)
            ob_p = _attn((qb_p, 0), (new_k, e), (new_v, e), (zb_p, 0), lam_even[e], dgain, lam_init,
                         None, e, n_p, t_p)
            out_p = _outproj([oa_p, ob_p], w_out, y_p, norm_post[layer], mods, gidx_p, next_gain, gnext_p)

            oa_s = _gdn(pa_s, gp_s, conv_even[e], alog_row, dtb_row, gnorm, state_gdn, None, e, n_s, t_s)
            ob_s = _attn((qb_s, 0), (kv_s, 0), (kv_s, 1), (zb_s, 0), lam_even[e], dgain, lam_init,
                         (cos, sin, cache_k, cache_v), e, n_s, t_s)
            out_s = _outproj([oa_s, ob_s], w_out, y_s, norm_post[layer], mods, gidx_s, next_gain, gnext_s)
        else:
            od = layer // 2
            n_main = 2 * H_C * DK_C + 2 * H_C * DV_C
            w_out = w_out_odd[od].astype(BF16)
            wg_pad = jnp.zeros((2, LANES, H_C * DK_C), F32)
            for d in range(2):
                wg_pad = wg_pad.at[d, d * GATE_RANK:(d + 1) * GATE_RANK].set(w_gate_odd[od, d])
            bgate = b_gate_odd[od].reshape(2, 1, H_C * DK_C)
            ggain = gla_norm_odd[od][None, :]

            n_qkv = 2 * H_C * DK_C + H_C * DV_C
            qkv_p, qkv_s = _inproj2(h_p, h_s, wt_odd, od, 0, n_qkv, IN_TILE, 1024, out_dtype=BF16)
            z_p, z_s = _inproj2(h_p, h_s, wt_odd, od, n_qkv, n_main - n_qkv, IN_TILE, 1024)
            glr_p, glr_s = _inproj2(h_p, h_s, wt_odd, od, n_main, LANES, IN_TILE, LANES)
            o_p, gla_states = _gla(qkv_p, z_p, glr_p, wg_pad, bgate, ggain, None, (gla_states, n_odd),
                                   od, n_p, t_p)
            out_p = _outproj([o_p], w_out, y_p, norm_post[layer], mods, gidx_p, next_gain, gnext_p)

            o_s = _gla(qkv_s, z_s, glr_s, wg_pad, bgate, ggain, state_gla, None, od, n_s, t_s)
            out_s = _outproj([o_s], w_out, y_s, norm_post[layer], mods, gidx_s, next_gain, gnext_s)
        if next_gain is None:
            y_p, y_s = out_p, out_s
        else:
            (y_p, h_p), (y_s, h_s) = out_p, out_s

    return (y_p.reshape(n_p, t_p, D_MODEL),
            y_s.reshape(n_s, t_s, D_MODEL),
            gdn_states,
            new_k.reshape(n_p, n_even, t_p, H_B, 2 * DQK_B),
            new_v.reshape(n_p, n_even, t_p, H_B, DV_B),
            gla_states)
```

```python
import functools
import math

import jax
import jax.numpy as jnp
from jax import lax
from jax.experimental import pallas as pl
from jax.experimental.pallas import tpu as pltpu

F32 = jnp.float32
BF16 = jnp.bfloat16
HIGHEST = lax.Precision.HIGHEST

D_MODEL = 2048
DEPTH = 4
EPS = 1e-6
GRID_W = 64
H_A = 8
DK_A = 128
DV_A = 128
CONV_K = 5
CHUNK_A = 64
PREP_CHAINS = 16
GDN_MAX_HEADS = 4
H_B = 8
DQK_B = 64
DV_B = 128
ROPE_BASE = 10000.0
Q_TILE = 256
Q_TILE_CTX = 256
ATTN_HEADS = 4
ATTN_HEADS_CTX = 2
H_C = 4
DK_C = 256
DV_C = 512
GATE_RANK = 16
GLA_NORMALIZER = 16.0
CHUNK_C = 128
GLA_MAX_HEADS = 4

LOG2E = math.log2(math.e)
LANES = 128
VMEM_LIMIT = 48 * 1024 * 1024
VMEM_MARGIN = 4 * 1024 * 1024
ROW_TILE = 512
OUT_SUBTILES = 4
IN_TILE = 1024


def _params(*semantics, vmem=VMEM_LIMIT):
    return pltpu.CompilerParams(dimension_semantics=semantics, vmem_limit_bytes=vmem)


def _mm(a, b):
    return jnp.dot(a.astype(BF16), b.astype(BF16), preferred_element_type=F32)


def _mm_nt(a, b):
    return lax.dot_general(a.astype(BF16), b.astype(BF16),
                           (((1,), (1,)), ((), ())), preferred_element_type=F32)


def _dot_nt(a, b):
    return lax.dot_general(a, b, (((1,), (1,)), ((), ())), preferred_element_type=F32)


def _mm_tn(a, b):
    return lax.dot_general(a.astype(BF16), b.astype(BF16),
                           (((0,), (0,)), ((), ())), preferred_element_type=F32)


def _mm_hi(a, b):
    return jnp.dot(a, b, precision=HIGHEST, preferred_element_type=F32)


def _mm_tn_hi(a, b):
    return lax.dot_general(a, b, (((0,), (0,)), ((), ())), precision=HIGHEST,
                           preferred_element_type=F32)


def _aligned(start, size, align):
    if isinstance(start, int):
        return pl.ds(start, size)
    return pl.ds(pl.multiple_of(start, align), size)


def _store_state(sf_ref, fill_slot, d, hd, state):
    if fill_slot is None:
        sf_ref[d, hd] = state
    else:
        for s in range(sf_ref.shape[0]):
            sf_ref[s, d, hd] = state if s == fill_slot else jnp.zeros_like(state)


def _silu(x):
    return x * jax.nn.sigmoid(x)


def _softplus(x):
    return jnp.maximum(x, 0.0) + jnp.log1p(jnp.exp(-jnp.abs(x)))


def _rms(x):
    return x * lax.rsqrt(jnp.mean(x * x, axis=-1, keepdims=True) + EPS)


def _adaln_kernel(c_ref, w_ref, b_ref, o_ref):
    o_ref[...] = _mm(_silu(c_ref[...]), w_ref[...]) + b_ref[...]


def _adaln(cvec, w_ada, b_ada):
    tn = 1024
    n3 = 3 * D_MODEL
    return pl.pallas_call(
        _adaln_kernel,
        grid=(DEPTH, n3 // tn),
        in_specs=[pl.BlockSpec((8, D_MODEL), lambda l, j: (0, 0)),
                  pl.BlockSpec((None, D_MODEL, tn), lambda l, j: (l, 0, j)),
                  pl.BlockSpec((None, 1, tn), lambda l, j: (l, 0, j))],
        out_specs=pl.BlockSpec((None, 8, tn), lambda l, j: (l, 0, j)),
        out_shape=jax.ShapeDtypeStruct((DEPTH, 8, n3), F32),
        compiler_params=_params("parallel", "parallel"),
        name="adaln",
    )(cvec, w_ada, b_ada.reshape(DEPTH, 1, n3))


def _prenorm_kernel(x_ref, g_ref, m_ref, h_ref):
    y = _rms(x_ref[...]) * g_ref[...]
    m = m_ref[...]
    h_ref[...] = (y * (1.0 + m[:, D_MODEL:2 * D_MODEL]) + m[:, :D_MODEL]).astype(BF16)


def _group_index(layer, group0, tiles_per_group):
    if tiles_per_group is None:
        return lambda i: (layer * 8 + group0, 0, 0)
    return lambda i: (layer * 8 + group0 + i // tiles_per_group, 0, 0)


def _prenorm(y, gain, mods, gidx):
    m = y.shape[0]
    return pl.pallas_call(
        _prenorm_kernel,
        grid=(m // ROW_TILE,),
        in_specs=[pl.BlockSpec((ROW_TILE, D_MODEL), lambda i: (i, 0)),
                  pl.BlockSpec((1, D_MODEL), lambda i: (0, 0)),
                  pl.BlockSpec((None, 1, 3 * D_MODEL), gidx)],
        out_specs=pl.BlockSpec((ROW_TILE, D_MODEL), lambda i: (i, 0)),
        out_shape=jax.ShapeDtypeStruct((m, D_MODEL), BF16),
        compiler_params=_params("parallel"),
        name="prenorm",
    )(y, gain.reshape(1, D_MODEL), mods)


def _cast_weight_block(wa_ref, wb_ref, wbf, shift, valid):
    @pl.when(pl.program_id(1) == 0)
    def _():
        w = wa_ref[...]
        if shift:
            w = jnp.concatenate([w[shift:], wb_ref[...]], axis=0)
        if valid is not None:
            w = jnp.where(lax.broadcasted_iota(jnp.int32, w.shape, 0) < valid, w, 0.0)
        wbf[...] = w.astype(BF16)


def _inproj2_kernel(*refs, shift, valid, tiles_a):
    xa_ref, xb_ref, wa_ref = refs[:3]
    wb_ref = refs[3] if shift else None
    oa_ref, ob_ref, wbf = refs[-3:]
    _cast_weight_block(wa_ref, wb_ref, wbf, shift, valid)

    @pl.when(pl.program_id(1) < tiles_a)
    def _():
        oa_ref[...] = _dot_nt(xa_ref[...], wbf[...]).astype(oa_ref.dtype)

    @pl.when(pl.program_id(1) >= tiles_a)
    def _():
        ob_ref[...] = _dot_nt(xb_ref[...], wbf[...]).astype(ob_ref.dtype)


def _inproj_kernel(*refs, shift, valid, fill_slot):
    x_ref, wa_ref = refs[:2]
    wb_ref = refs[2] if shift else None
    o_ref, wbf = refs[-2:]
    _cast_weight_block(wa_ref, wb_ref, wbf, shift, valid)
    res = _dot_nt(x_ref[...], wbf[...])
    if fill_slot is None:
        o_ref[...] = res.reshape(o_ref.shape).astype(o_ref.dtype)
    else:
        for s in range(o_ref.shape[1]):
            blk = (o_ref.shape[0],) + o_ref.shape[2:]
            o_ref[:, s] = res.reshape(blk) if s == fill_slot else jnp.zeros(blk, F32)


def _inproj(x, wt3, layer_idx, row0, n_out, tm, tn, shift=0, cache_slot=None, out_dtype=F32):
    m, k = x.shape
    n_rows = wt3.shape[1]
    valid = n_rows - row0 if row0 + n_out > n_rows else None
    assert valid is None or (n_out == tn and not shift)
    in_specs = [pl.BlockSpec((tm, k), lambda j, i: (i, 0)),
                pl.BlockSpec((None, tn, k), lambda j, i: (layer_idx, row0 // tn + j, 0))]
    args = [x, wt3]
    if shift:
        in_specs.append(pl.BlockSpec((None, shift, k),
                                     lambda j, i: (layer_idx, (row0 + (j + 1) * tn) // shift, 0)))
        args.append(wt3)
    aliases = {}
    fill_slot = None
    if cache_slot is None:
        out_spec = pl.BlockSpec((tm, tn), lambda j, i: (i, j))
        out_shape = jax.ShapeDtypeStruct((m, n_out), out_dtype)
    else:
        prev, slot, t = cache_slot
        n_slots = wt3.shape[0]
        out_shape = jax.ShapeDtypeStruct((m // t, n_slots, t, n_out), F32)
        if prev is None:
            fill_slot = slot
            out_spec = pl.BlockSpec((tm // t, n_slots, t, tn), lambda j, i: (i, 0, 0, 0))
        else:
            out_spec = pl.BlockSpec((tm // t, None, t, tn), lambda j, i: (i, slot, 0, 0))
            in_specs.append(pl.BlockSpec(memory_space=pl.ANY))
            args.append(prev)
            aliases = {len(args) - 1: 0}

    return pl.pallas_call(
        functools.partial(_inproj_kernel, shift=shift, valid=valid, fill_slot=fill_slot),
        grid=(n_out // tn, m // tm),
        in_specs=in_specs,
        out_specs=out_spec,
        out_shape=out_shape,
        scratch_shapes=[pltpu.VMEM((tn, k), BF16)],
        input_output_aliases=aliases,
        compiler_params=_params("parallel", "arbitrary"),
        name="inproj",
    )(*args)


def _inproj2(xa, xb, wt3, layer_idx, row0, n_out, tm, tn, shift=0, out_dtype=F32):
    (ma, k), mb = xa.shape, xb.shape[0]
    ta, tb = ma // tm, mb // tm
    n_rows = wt3.shape[1]
    valid = n_rows - row0 if row0 + n_out > n_rows else None
    assert valid is None or (n_out == tn and not shift)

    def a_tile(i):
        return jnp.minimum(i, ta - 1)

    def b_tile(i):
        return jnp.maximum(i - ta, 0)

    in_specs = [pl.BlockSpec((tm, k), lambda j, i: (a_tile(i), 0)),
                pl.BlockSpec((tm, k), lambda j, i: (b_tile(i), 0)),
                pl.BlockSpec((None, tn, k), lambda j, i: (layer_idx, row0 // tn + j, 0))]
    args = [xa, xb, wt3]
    if shift:
        in_specs.append(pl.BlockSpec((None, shift, k),
                                     lambda j, i: (layer_idx, (row0 + (j + 1) * tn) // shift, 0)))
        args.append(wt3)
    return pl.pallas_call(
        functools.partial(_inproj2_kernel, shift=shift, valid=valid, tiles_a=ta),
        grid=(n_out // tn, ta + tb),
        in_specs=in_specs,
        out_specs=(pl.BlockSpec((tm, tn), lambda j, i: (a_tile(i), j)),
                   pl.BlockSpec((tm, tn), lambda j, i: (b_tile(i), j))),
        out_shape=(jax.ShapeDtypeStruct((ma, n_out), out_dtype),
                   jax.ShapeDtypeStruct((mb, n_out), out_dtype)),
        scratch_shapes=[pltpu.VMEM((tn, k), BF16)],
        compiler_params=_params("parallel", "arbitrary",
                                vmem=4 * tm * k * 2 + 4 * tm * tn * 4 + 2 * tn * k * 4 + tn * k * 2
                                + 2 * shift * k * 4 + VMEM_MARGIN),
        name="inproj2",
    )(*args)


def _outproj_kernel(*refs, n_in, emit_h):
    x_refs = refs[:n_in]
    w_ref, y_ref, g_ref, m_ref = refs[n_in:n_in + 4]
    if emit_h:
        gn_ref, mn_ref, o_ref, h_ref = refs[n_in + 4:]
    else:
        o_ref = refs[n_in + 4]
    m = m_ref[...]
    sub = o_ref.shape[0] // OUT_SUBTILES
    accs = []
    for s in range(OUT_SUBTILES):
        rows = slice(s * sub, (s + 1) * sub)
        acc = None
        off = 0
        for x_ref in x_refs:
            kk = x_ref.shape[1]
            part = jnp.dot(x_ref[rows, :], w_ref[off:off + kk, :], preferred_element_type=F32)
            acc = part if acc is None else acc + part
            off += kk
        accs.append(acc)
    for s in range(OUT_SUBTILES):
        rows = slice(s * sub, (s + 1) * sub)
        y = y_ref[rows, :] + m[:, 2 * D_MODEL:] * (_rms(accs[s]) * g_ref[...])
        if emit_h:
            mn = mn_ref[...]
            h_ref[rows, :] = (_rms(y) * gn_ref[...] * (1.0 + mn[:, D_MODEL:2 * D_MODEL])
                              + mn[:, :D_MODEL]).astype(BF16)
        o_ref[rows, :] = y


def _outproj(xs, w, y, gain, mods, gidx, next_gain=None, gidx_next=None):
    m = y.shape[0]
    emit_h = next_gain is not None
    row_spec = pl.BlockSpec((ROW_TILE, D_MODEL), lambda i: (i, 0))
    vec_spec = pl.BlockSpec((1, D_MODEL), lambda i: (0, 0))
    in_specs = [pl.BlockSpec((ROW_TILE, x.shape[1]), lambda i: (i, 0)) for x in xs]
    in_specs += [pl.BlockSpec(w.shape, lambda i: (0, 0)), row_spec, vec_spec,
                 pl.BlockSpec((None, 1, 3 * D_MODEL), gidx)]
    args = [*xs, w, y, gain.reshape(1, D_MODEL), mods]
    out_specs = row_spec
    out_shape = jax.ShapeDtypeStruct((m, D_MODEL), F32)
    if emit_h:
        in_specs += [vec_spec, pl.BlockSpec((None, 1, 3 * D_MODEL), gidx_next)]
        args += [next_gain.reshape(1, D_MODEL), mods]
        out_specs = (row_spec, row_spec)
        out_shape = (out_shape, jax.ShapeDtypeStruct((m, D_MODEL), BF16))
    return pl.pallas_call(
        functools.partial(_outproj_kernel, n_in=len(xs), emit_h=emit_h),
        grid=(m // ROW_TILE,),
        in_specs=in_specs,
        out_specs=out_specs,
        out_shape=out_shape,
        compiler_params=_params("parallel"),
        name="outproj",
    )(*args)


def _gdn_kernel(*refs, seq_len, heads, has_state, fill_slot):
    t = seq_len
    c = CHUNK_A
    nc = t // c
    (q_ref, k_ref, v_ref, z_ref, gp_ref, cwq_ref, cwk_ref, cwv_ref,
     alog_ref, dtb_ref, gn_ref) = refs[:11]
    qn, kn, vn, bt, gt, wv_s, wkqd_s, qk_s, kd_s, gl_s, oacc, s_s, xpad = refs[-13:]
    if has_state:
        s0_ref, o_ref = refs[11], refs[-14]
    else:
        o_ref, sf_ref = refs[-15:-13]
    head0 = pl.program_id(1) * heads

    def lanes_of(hd):
        return slice(hd * LANES, (hd + 1) * LANES)

    row_t = lax.broadcasted_iota(jnp.int32, (t, LANES), 0)
    xpad[0:8, :] = jnp.zeros((8, LANES), F32)
    xpad[8 + t:, :] = jnp.zeros((8, LANES), F32)

    def conv_silu(x_ref, w_ref, hd):
        w = w_ref[:, lanes_of(hd)]
        xpad[8:8 + t, :] = x_ref[:, lanes_of(hd)]
        acc = None
        for tap in range(CONV_K):
            start = 8 + tap - CONV_K // 2
            term = xpad[start:start + t, :] * w[tap:tap + 1, :]
            acc = term if acc is None else acc + term
        return _silu(acc)

    def l2n(x):
        return x * lax.rsqrt(jnp.sum(x * x, axis=-1, keepdims=True) + EPS)

    for hd in range(heads):
        qn[hd] = l2n(conv_silu(q_ref, cwq_ref, hd))
        kn[hd] = l2n(conv_silu(k_ref, cwk_ref, hd))
        vn[hd] = conv_silu(v_ref, cwv_ref, hd)

    gp = gp_ref[...]
    lane_t = lax.broadcasted_iota(jnp.int32, (t, LANES), 1)
    beta_all = jax.nn.sigmoid(gp)
    g_all = -jnp.exp(alog_ref[...]) * _softplus(gp + dtb_ref[...])

    def pick(x, idx):
        col = jnp.sum(jnp.where(lane_t == idx, x, 0.0), axis=-1, keepdims=True)
        return jnp.broadcast_to(col, (t, LANES))

    pos_t = row_t & (c - 1)
    g_cum = [g_all, g_all]
    sh = 1
    while sh < c:
        g_cum[0] = g_cum[0] + jnp.where(pos_t >= sh, pltpu.roll(g_cum[0], sh, axis=0), 0.0)
        g_cum[1] = g_cum[1] + jnp.where(pos_t < c - sh, pltpu.roll(g_cum[1], t - sh, axis=0), 0.0)
        sh *= 2

    for hd in range(heads):
        for d in range(2):
            bt[hd * 2 + d] = pick(beta_all, d * H_A + head0 + hd)
            gt[hd * 2 + d] = pick(g_cum[d], 2 * H_A + d * H_A + head0 + hd)

    r2 = 2 * c
    row = lax.broadcasted_iota(jnp.int32, (r2, r2), 0)
    col = lax.broadcasted_iota(jnp.int32, (r2, r2), 1)
    rc_xor = row ^ col

    tri_inc, tri_str = [], []
    for rev in (False, True):
        after = (col - row) if rev else (row - col)
        after = jnp.where(rc_xor < c, after, -1)
        tri_inc.append(after >= 0)
        tri_str.append(after > 0)
    eye = jnp.where(row == col, 1.0, 0.0)

    def prep(chains):
        n = range(len(chains))
        rows = [_aligned(gi * r2, r2, r2) for _, gi, _ in chains]
        dirs = [d for _, _, d in chains]
        hdir = [hd * 2 + d for hd, _, d in chains]
        q = [qn[chains[i][0], rows[i], :] * DK_A ** -0.5 for i in n]
        k = [kn[chains[i][0], rows[i], :] for i in n]
        beta = [bt[hdir[i], rows[i], :] for i in n]
        kb = [k[i] * beta[i] for i in n]
        kk = [_mm_nt(kb[i], k[i]) for i in n]
        qk = [_mm_nt(q[i], k[i]) for i in n]
        gcum = [gt[hdir[i], rows[i], :] for i in n]
        decay = [jnp.where(tri_inc[dirs[i]], jnp.exp(jnp.minimum(gcum[i] - gcum[i].T, 0.0)), 0.0) for i in n]
        amat = [jnp.where(tri_str[dirs[i]], kk[i] * decay[i], 0.0) for i in n]
        tinv = [eye - jnp.where(rc_xor < 2, amat[i], 0.0) for i in n]
        b = 2
        while b < c:
            level = (rc_xor >= b) & (rc_xor < 2 * b)
            m1 = [_mm(jnp.where(level, amat[i], 0.0), tinv[i]) for i in n]
            m2 = [_mm(tinv[i], m1[i]) for i in n]
            tinv = [tinv[i] - m2[i] for i in n]
            b *= 2
        eg = [jnp.exp(gcum[i]) for i in n]
        x = [_mm(tinv[i], jnp.concatenate([vn[chains[i][0], rows[i], :] * beta[i], kb[i] * eg[i]], axis=1))
             for i in n]
        for i in n:
            _, gi, d = chains[i]
            hd2 = hdir[i]
            qkd = qk[i] * decay[i]
            g_end = _block_row_bcast(gcum[i], c, 0 if d == 1 else c - 1)
            qd = q[i] * eg[i]
            gl = jnp.exp(g_end)
            wv_s[hd2, rows[i], :] = x[i][:, :DV_A]
            qk_s[hd2, rows[i], :] = jnp.where(row < c, qkd, pltpu.roll(qkd, c, axis=1))[:, :c]
            kd_s[hd2, rows[i], :] = k[i] * jnp.exp(g_end - gcum[i])
            for j in range(2):
                base = gi * 2 * r2 + j * r2
                wkqd_s[hd2, _aligned(base, c, c), :] = x[i][j * c:(j + 1) * c, DV_A:]
                wkqd_s[hd2, _aligned(base + c, c, c), :] = qd[j * c:(j + 1) * c, :]
                gl_s[hd2, _aligned(gi * 16 + j * 8, 8, 8), :] = gl[j * c:j * c + 8, :]

    def rec(chains):
        n = range(len(chains))
        rows = [_aligned(ci * c, c, c) for _, ci, _ in chains]
        hdir = [hd * 2 + d for hd, _, d in chains]
        s = [s_s[hdir[i]] for i in n]
        ws = [_mm(wkqd_s[hdir[i], _aligned(chains[i][1] * 2 * c, 2 * c, 2 * c), :], s[i]) for i in n]
        u = [wv_s[hdir[i], rows[i], :] - ws[i][:c] for i in n]
        o2 = [_mm(qk_s[hdir[i], rows[i], :], u[i]) for i in n]
        ds = [_mm_tn(kd_s[hdir[i], rows[i], :], u[i]) for i in n]
        for i in n:
            s_s[hdir[i]] = s[i] * gl_s[hdir[i], _aligned(chains[i][1] * 8, 1, 8), :] + ds[i]
        for i in n:
            oacc[chains[i][0], rows[i], :] += ws[i][c:] + o2[i]

    for hd in range(heads):
        for d in range(2):
            if has_state:
                s_s[hd * 2 + d] = s0_ref[d, hd]
            else:
                s_s[hd * 2 + d] = jnp.zeros((DK_A, DV_A), F32)
    oacc[...] = jnp.zeros((heads, t, DV_A), F32)

    n_groups = nc // 2
    unroll = max(1, min(n_groups, PREP_CHAINS // (2 * heads)))

    def prep_body(gi, carry):
        prep([(hd, gi * unroll + j, d) for hd in range(heads) for j in range(unroll) for d in range(2)])
        return carry

    if n_groups == unroll:
        prep_body(0, 0)
    else:
        lax.fori_loop(0, n_groups // unroll, prep_body, 0)

    def rec_body(i, carry):
        rec([(hd, i if d == 0 else nc - 1 - i, d) for hd in range(heads) for d in range(2)])
        return carry

    lax.fori_loop(0, nc, rec_body, 0)

    for hd in range(heads):
        o_ref[:, lanes_of(hd)] = (_rms(oacc[hd]) * gn_ref[...] * _silu(z_ref[:, lanes_of(hd)])).astype(BF16)
        if not has_state:
            for d in range(2):
                _store_state(sf_ref, fill_slot, d, hd, s_s[hd * 2 + d])


def _gdn_heads(t):
    for g in (GDN_MAX_HEADS, 2, 1):
        scratch = g * t * LANES * 4 * 17 + 2 * g * DK_A * DV_A * 4
        blocks = 2 * (4 * t * g * LANES * 4 + t * LANES * 4 + t * g * LANES * 2)
        if scratch + blocks <= VMEM_LIMIT - (8 << 20):
            return g
    return 1


def _gdn(proj, gp, conv_w, alog_row, dtb_row, gnorm, state, state_out, layer_e, nseq, seq_len):
    t = seq_len
    nc = t // CHUNK_A
    has_state = state is not None
    aliases = {}
    fill_slot = None

    g = _gdn_heads(t)
    w = g * LANES
    nb = H_A // g

    def col(seg):
        return pl.BlockSpec((t, w), lambda b, h: (b, seg * nb + h))

    def cw(seg):
        return pl.BlockSpec((CONV_K, w), lambda b, h: (0, seg * nb + h))

    row_spec = pl.BlockSpec((1, LANES), lambda b, h: (0, 0))
    in_specs = [col(0), col(1), col(2), col(3),
                pl.BlockSpec((t, LANES), lambda b, h: (b, 0)),
                cw(0), cw(1), cw(2), row_spec, row_spec, row_spec]
    args = [proj, proj, proj, proj, gp, conv_w, conv_w, conv_w, alog_row, dtb_row, gnorm]
    o_shape = jax.ShapeDtypeStruct((nseq * t, H_A * DV_A), BF16)
    o_spec = pl.BlockSpec((t, w), lambda b, h: (b, h))
    if has_state:
        in_specs.append(pl.BlockSpec((None, None, 2, g, DK_A, DV_A),
                                     lambda b, h: (b, layer_e, 0, h, 0, 0)))
        args.append(state)
        out_shape, out_specs = o_shape, o_spec
    else:
        prev, n_slots = state_out
        out_shape = (o_shape, jax.ShapeDtypeStruct((nseq, n_slots, 2, H_A, DK_A, DV_A), F32))
        if prev is None:
            fill_slot = layer_e
            s_spec = pl.BlockSpec((None, n_slots, 2, g, DK_A, DV_A), lambda b, h: (b, 0, 0, h, 0, 0))
        else:
            s_spec = pl.BlockSpec((None, None, 2, g, DK_A, DV_A), lambda b, h: (b, layer_e, 0, h, 0, 0))
            in_specs.append(pl.BlockSpec(memory_space=pl.ANY))
            args.append(prev)
            aliases = {len(args) - 1: 1}
        out_specs = (o_spec, s_spec)
    scratch = [pltpu.VMEM((g, t, LANES), F32)] * 3
    scratch += [pltpu.VMEM((2 * g, t, LANES), F32)] * 2
    scratch += [pltpu.VMEM((2 * g, t, LANES), F32),
                pltpu.VMEM((2 * g, 2 * t, LANES), F32),
                pltpu.VMEM((2 * g, t, CHUNK_A), F32),
                pltpu.VMEM((2 * g, t, LANES), F32),
                pltpu.VMEM((2 * g, nc * 8, LANES), F32),
                pltpu.VMEM((g, t, LANES), F32),
                pltpu.VMEM((2 * g, DK_A, DV_A), F32),
                pltpu.VMEM((t + 16, LANES), F32)]
    return pl.pallas_call(
        functools.partial(_gdn_kernel, seq_len=t, heads=g, has_state=has_state, fill_slot=fill_slot),
        grid=(nseq, nb),
        in_specs=in_specs,
        out_specs=out_specs,
        out_shape=out_shape,
        scratch_shapes=scratch,
        input_output_aliases=aliases,
        compiler_params=_params("parallel", "parallel"),
        name="gdn",
    )(*args)


def _rope(x, cos, sin):
    x = x.astype(F32)
    lane = lax.broadcasted_iota(jnp.int32, x.shape, 1)
    quarter = DQK_B // 4
    rot = jnp.where((lane & quarter) == 0,
                    -pltpu.roll(x, LANES - quarter, axis=1),
                    pltpu.roll(x, quarter, axis=1))
    return x * cos + rot * sin


def _attn_kernel(*refs, seq_len, heads, lam_init, has_ctx):
    t = seq_len
    q_ref, k_ref, v_ref, z_ref, lam_ref, dg_ref = refs[:6]
    if has_ctx:
        cq_ref, sq_ref, ck_ref, sk_ref, kc_ref, vc_ref, o_ref, k_all, v_all = refs[6:]
    else:
        o_ref = refs[6]

    def lanes_of(hd):
        return slice(hd * LANES, (hd + 1) * LANES)

    if has_ctx:
        @pl.when(pl.program_id(2) == 0)
        def _():
            for hd in range(heads):
                k_all[hd, 0:t, :] = _rope(k_ref[:, lanes_of(hd)], ck_ref[...], sk_ref[...])
                k_all[hd, t:, :] = kc_ref[:, lanes_of(hd)]
                v_all[hd, 0:t, :] = v_ref[:, lanes_of(hd)].astype(F32)
                v_all[hd, t:, :] = vc_ref[:, lanes_of(hd)]
    lp = lam_ref[...]
    lam = (jnp.exp(jnp.sum(lp[0:1] * lp[1:2], axis=-1, keepdims=True))
           - jnp.exp(jnp.sum(lp[2:3] * lp[3:4], axis=-1, keepdims=True)) + lam_init)
    lane = lax.broadcasted_iota(jnp.int32, (q_ref.shape[0], LANES), 1)

    chains = [(hd, m) for hd in range(heads) for m in range(2)]
    n = range(len(chains))
    q = []
    for hd in range(heads):
        qh = q_ref[:, lanes_of(hd)]
        if has_ctx:
            qh = _rope(qh, cq_ref[...], sq_ref[...])
        q.append(qh * DQK_B ** -0.5)
    k = [k_all[hd] if has_ctx else k_ref[:, lanes_of(hd)] for hd in range(heads)]
    v = [v_all[hd] if has_ctx else v_ref[:, lanes_of(hd)] for hd in range(heads)]
    qm = [jnp.where(lane < DQK_B, q[hd], 0.0) if m == 0 else jnp.where(lane < DQK_B, 0.0, q[hd])
          for hd, m in chains]
    s = [_mm_nt(qm[i], k[chains[i][0]]) for i in n]
    e = [jnp.exp(s[i] - jnp.max(s[i], axis=-1, keepdims=True)) for i in n]
    den = [jnp.sum(e[i], axis=-1, keepdims=True) for i in n]
    ev = [_mm(e[i], v[chains[i][0]]) for i in n]
    for hd in range(heads):
        o = ev[2 * hd] / den[2 * hd] - lam * (ev[2 * hd + 1] / den[2 * hd + 1])
        o_ref[:, lanes_of(hd)] = (_rms(o) * dg_ref[...] * (1.0 - lam_init)
                                  * _silu(z_ref[:, lanes_of(hd)])).astype(BF16)


def _attn(q, k, v, z, lam_p, dgain, lam_init, ctx, layer_e, nseq, seq_len):
    t = seq_len
    has_ctx = ctx is not None
    g = ATTN_HEADS_CTX if has_ctx else ATTN_HEADS
    tq = min(t, Q_TILE_CTX if has_ctx else Q_TILE)
    nq = t // tq
    w = g * LANES
    nb = H_B // g

    def rows_spec(src, rows):
        arr, seg = src
        if arr.ndim == 4:
            assert rows == t
            return pl.BlockSpec((None, None, t, w), lambda b, h, i: (b, seg, 0, h))
        if rows == t:
            return pl.BlockSpec((t, w), lambda b, h, i: (b, seg * nb + h))
        return pl.BlockSpec((rows, w), lambda b, h, i: (b * nq + i, seg * nb + h))

    in_specs = [rows_spec(q, tq), rows_spec(k, t), rows_spec(v, t), rows_spec(z, tq),
                pl.BlockSpec((4, DQK_B), lambda b, h, i: (0, 0)),
                pl.BlockSpec((1, LANES), lambda b, h, i: (0, 0))]
    args = [q[0], k[0], v[0], z[0], lam_p, dgain]
    scratch = []
    if has_ctx:
        cos, sin, cache_k, cache_v = ctx
        past = cache_k.shape[2]
        in_specs += [pl.BlockSpec((tq, LANES), lambda b, h, i: (i, 0)),
                     pl.BlockSpec((tq, LANES), lambda b, h, i: (i, 0)),
                     pl.BlockSpec((t, LANES), lambda b, h, i: (0, 0)),
                     pl.BlockSpec((t, LANES), lambda b, h, i: (0, 0)),
                     pl.BlockSpec((None, None, past, w), lambda b, h, i: (b, layer_e, 0, h)),
                     pl.BlockSpec((None, None, past, w), lambda b, h, i: (b, layer_e, 0, h))]
        args += [cos, sin, cos, sin, cache_k.reshape(cache_k.shape[:3] + (-1,)),
                 cache_v.reshape(cache_v.shape[:3] + (-1,))]
        scratch = [pltpu.VMEM((g, t + past, LANES), F32)] * 2
    return pl.pallas_call(
        functools.partial(_attn_kernel, seq_len=t, heads=g, lam_init=lam_init, has_ctx=has_ctx),
        grid=(nseq, nb, nq),
        in_specs=in_specs,
        out_specs=pl.BlockSpec((tq, w), lambda b, h, i: (b * nq + i, h)),
        out_shape=jax.ShapeDtypeStruct((nseq * t, H_B * DV_B), BF16),
        scratch_shapes=scratch,
        compiler_params=_params("parallel", "parallel", "arbitrary"),
        name="diffattn",
    )(*args)


def _block_row_bcast(x, blk, r):
    n, w = x.shape
    if blk >= 8:
        x3 = x.reshape(n // blk, blk, w)
        return jnp.broadcast_to(x3[:, r:r + 1, :], x3.shape).reshape(n, w)
    x3 = x.reshape(n // 8, 8, w)
    sub = lax.broadcasted_iota(jnp.int32, x3.shape, 1)
    out = None
    for b in range(8 // blk):
        cand = jnp.broadcast_to(x3[:, b * blk + r:b * blk + r + 1, :], x3.shape)
        out = cand if out is None else jnp.where(sub >= b * blk, cand, out)
    return out.reshape(n, w)


def _gla_kernel(*refs, seq_len, heads, has_state, fill_slot):
    t = seq_len
    c = CHUNK_C
    nc = t // c
    q_ref, k_ref, v_ref, z_ref, glr_ref, wg_ref, bg_ref, gg_ref = refs[:8]
    la_s, oacc, st_s = refs[-3:]
    if has_state:
        s0_ref, o_ref = refs[8], refs[-4]
    else:
        o_ref, sf_ref = refs[-5:-3]

    def kcols(hd):
        return slice(hd * DK_C, (hd + 1) * DK_C)

    def vcols(hd):
        return slice(hd * DV_C, (hd + 1) * DV_C)

    glr = glr_ref[...]
    for hd in range(heads):
        for d in range(2):
            gate = _mm(glr, wg_ref[d, :, kcols(hd)]) + bg_ref[d, :, kcols(hd)]
            la_s[hd * 2 + d] = ((jnp.minimum(gate, 0.0) - jnp.log(1.0 + jnp.exp(-jnp.abs(gate))))
                                * (LOG2E / GLA_NORMALIZER))

    row = lax.broadcasted_iota(jnp.int32, (c, c), 0)
    col = lax.broadcasted_iota(jnp.int32, (c, c), 1)
    top = row ^ col
    sh = 1
    while sh < c:
        top = top | (top >> sh)
        sh *= 2
    top = (top + 1) >> 1
    pair_code = jnp.where(row > col, top, -top)
    tri = [jnp.where(row >= col, 1.0, 0.0).astype(BF16), jnp.where(row <= col, 1.0, 0.0).astype(BF16)]

    def chunks(chains):
        n = range(len(chains))
        rows = [_aligned(ci * c, c, c) for _, ci, _ in chains]
        rev = [d == 1 for _, _, d in chains]
        hdir = [hd * 2 + d for hd, _, d in chains]
        q = [q_ref[rows[i], kcols(chains[i][0])] * DK_C ** -0.5 for i in n]
        k = [k_ref[rows[i], kcols(chains[i][0])] for i in n]
        v = [v_ref[rows[i], vcols(chains[i][0])] for i in n]
        qb = [q[i].astype(BF16) for i in n]
        kb = [k[i].astype(BF16) for i in n]
        attn = [jnp.where(row == col, _dot_nt(qb[i], kb[i]), 0.0) for i in n]
        la = [la_s[hdir[i], rows[i], :] for i in n]
        la_hi = [la[i].astype(BF16) for i in n]
        la_lo = [(la[i] - la_hi[i].astype(F32)).astype(BF16) for i in n]
        bc = [jnp.dot(tri[chains[i][2]], la_hi[i], preferred_element_type=F32)
              + jnp.dot(tri[chains[i][2]], la_lo[i], preferred_element_type=F32) for i in n]
        s = c // 2
        while s >= 1:
            for i in n:
                bref = _block_row_bcast(bc[i], 2 * s, s if rev[i] else s - 1)
                e = jnp.exp2(-jnp.abs(bc[i] - bref)).astype(BF16)
                pairs = pair_code == (-s if rev[i] else s)
                attn[i] = attn[i] + jnp.where(pairs, _dot_nt(qb[i] * e, kb[i] * e), 0.0)
            s //= 2
        b_end = [bc[i][0:1, :] if rev[i] else bc[i][c - 1:c, :] for i in n]
        st = [st_s[hdir[i]] for i in n]
        o = [_mm(attn[i], v[i]) + _mm_nt(q[i] * jnp.exp2(bc[i]), st[i]) for i in n]
        upd = [_mm_tn(v[i], k[i] * jnp.exp2(b_end[i] - bc[i])) for i in n]
        for i in n:
            st_s[hdir[i]] = st[i] * jnp.exp2(b_end[i]) + upd[i]
        for i in n:
            oacc[chains[i][0], rows[i], :] += o[i]

    for hd in range(heads):
        for d in range(2):
            if has_state:
                st_s[hd * 2 + d] = s0_ref[d, hd].T
            else:
                st_s[hd * 2 + d] = jnp.zeros((DV_C, DK_C), F32)
    oacc[...] = jnp.zeros((heads, t, DV_C), F32)

    def body(i, carry):
        chunks([(hd, i if d == 0 else nc - 1 - i, d) for hd in range(heads) for d in range(2)])
        return carry

    lax.fori_loop(0, nc, body, 0)

    for hd in range(heads):
        o_ref[:, vcols(hd)] = (_rms(oacc[hd]) * gg_ref[...] * _silu(z_ref[:, vcols(hd)])).astype(BF16)
        if not has_state:
            for d in range(2):
                _store_state(sf_ref, fill_slot, d, hd, st_s[hd * 2 + d].T)


def _gla_heads(t):
    for g in (GLA_MAX_HEADS, 2, 1):
        scratch = g * (2 * t * DK_C + t * DV_C + 2 * DV_C * DK_C) * 4
        blocks = 2 * (g * t * ((2 * DK_C + DV_C) * 2 + DV_C * 4) + t * LANES * 4 + g * t * DV_C * 2
                      + 2 * g * DK_C * DV_C * 4)
        if scratch + blocks <= VMEM_LIMIT - (12 << 20):
            return g
    return 1


def _gla(qkv, zproj, glr, wg_pad, bgate, ggain, state, state_out, layer_o, nseq, seq_len):
    t = seq_len
    has_state = state is not None
    aliases = {}
    fill_slot = None
    g = _gla_heads(t)
    nb = H_C // g
    wk, wv = g * DK_C, g * DV_C
    in_specs = [pl.BlockSpec((t, wk), lambda b, h: (b, h)),
                pl.BlockSpec((t, wk), lambda b, h: (b, nb + h)),
                pl.BlockSpec((t, wv), lambda b, h: (b, nb + h)),
                pl.BlockSpec((t, wv), lambda b, h: (b, h)),
                pl.BlockSpec((t, LANES), lambda b, h: (b, 0)),
                pl.BlockSpec((2, LANES, wk), lambda b, h: (0, 0, h)),
                pl.BlockSpec((2, 1, wk), lambda b, h: (0, 0, h)),
                pl.BlockSpec((1, DV_C), lambda b, h: (0, 0))]
    args = [qkv, qkv, qkv, zproj, glr, wg_pad, bgate, ggain]
    o_shape = jax.ShapeDtypeStruct((nseq * t, H_C * DV_C), BF16)
    o_spec = pl.BlockSpec((t, wv), lambda b, h: (b, h))
    if has_state:
        in_specs.append(pl.BlockSpec((None, None, 2, g, DK_C, DV_C),
                                     lambda b, h: (b, layer_o, 0, h, 0, 0)))
        args.append(state)
        out_shape, out_specs = o_shape, o_spec
    else:
        prev, n_slots = state_out
        out_shape = (o_shape, jax.ShapeDtypeStruct((nseq, n_slots, 2, H_C, DK_C, DV_C), F32))
        if prev is None:
            fill_slot = layer_o
            s_spec = pl.BlockSpec((None, n_slots, 2, g, DK_C, DV_C), lambda b, h: (b, 0, 0, h, 0, 0))
        else:
            s_spec = pl.BlockSpec((None, None, 2, g, DK_C, DV_C), lambda b, h: (b, layer_o, 0, h, 0, 0))
            in_specs.append(pl.BlockSpec(memory_space=pl.ANY))
            args.append(prev)
            aliases = {len(args) - 1: 1}
        out_specs = (o_spec, s_spec)
    scratch = [pltpu.VMEM((2 * g, t, DK_C), F32),
               pltpu.VMEM((g, t, DV_C), F32),
               pltpu.VMEM((2 * g, DV_C, DK_C), F32)]
    return pl.pallas_call(
        functools.partial(_gla_kernel, seq_len=t, heads=g, has_state=has_state, fill_slot=fill_slot),
        grid=(nseq, nb),
        in_specs=in_specs,
        out_specs=out_specs,
        out_shape=out_shape,
        scratch_shapes=scratch,
        input_output_aliases=aliases,
        compiler_params=_params("parallel", "parallel"),
        name="gla",
    )(*args)


def _rope_tables(n_tokens):
    rows = n_tokens // GRID_W
    row = jnp.repeat(jnp.arange(rows), GRID_W).astype(F32)
    col = jnp.tile(jnp.arange(GRID_W), rows).astype(F32)
    quarter = DQK_B // 4
    freqs = ROPE_BASE ** (-jnp.arange(quarter, dtype=F32) / quarter)
    ar = row[:, None] * freqs
    ac = col[:, None] * freqs
    ang = jnp.concatenate([ar, ar, ac, ac] * (LANES // DQK_B), axis=-1)
    return jnp.cos(ang), jnp.sin(ang)


def _pad_lanes(x):
    return jnp.pad(x, ((0, 0), (0, LANES - x.shape[1])))


def kernel(x_prompt, x_sample, c, state_gdn, cache_k, cache_v, state_gla, c_ctx, w_ada, b_ada, norm_pre, norm_post, w_in_even, conv_even, a_log_even, dt_bias_even, gdn_norm_even, lam_even, diff_norm_even, w_out_even, w_in_odd, w_gate_odd, b_gate_odd, gla_norm_odd, w_out_odd):
    n_p, t_p, _ = x_prompt.shape
    n_s, t_s, _ = x_sample.shape
    wa = H_A * DK_A

    cvec = jnp.zeros((8, D_MODEL), F32).at[0].set(c_ctx).at[1:1 + n_s].set(c)
    mods = _adaln(cvec, w_ada, b_ada).reshape(DEPTH * 8, 1, 3 * D_MODEL)
    cos, sin = _rope_tables(t_s)

    y_p = x_prompt.reshape(n_p * t_p, D_MODEL)
    y_s = x_sample.reshape(n_s * t_s, D_MODEL)
    n_even, n_odd = w_in_even.shape[0], w_in_odd.shape[0]
    wt_even = jnp.swapaxes(w_in_even, 1, 2)
    wt_odd = jnp.swapaxes(w_in_odd, 1, 2)
    gdn_states = new_k = new_v = gla_states = None

    def gidx(layer):
        return _group_index(layer, 0, None), _group_index(layer, 1, t_s // ROW_TILE)

    gidx_p, gidx_s = gidx(0)
    h_p = _prenorm(y_p, norm_pre[0], mods, gidx_p)
    h_s = _prenorm(y_s, norm_pre[0], mods, gidx_s)

    for layer in range(DEPTH):
        gidx_p, gidx_s = gidx(layer)
        if layer + 1 < DEPTH:
            next_gain = norm_pre[layer + 1]
            gnext_p, gnext_s = gidx(layer + 1)
        else:
            next_gain = gnext_p = gnext_s = None
        if layer % 2 == 0:
            e = layer // 2
            lam_init = 0.8 - 0.6 * math.exp(-0.3 * layer)
            w_out = w_out_even[e].astype(BF16)
            zeros16 = jnp.zeros((2 * H_A,), F32)
            alog_row = _pad_lanes(jnp.concatenate([zeros16, a_log_even[e].reshape(-1)])[None, :])
            dtb_row = _pad_lanes(jnp.concatenate([zeros16, dt_bias_even[e].reshape(-1)])[None, :])
            gnorm = gdn_norm_even[e][None, :]
            dgain = diff_norm_even[e][None, :]
            col_b = 4 * wa
            shift_b = 4 * H_A

            pa_p, pa_s = _inproj2(h_p, h_s, wt_even, e, 0, 4 * wa, IN_TILE, 1024)
            gp_p, gp_s = _inproj2(h_p, h_s, wt_even, e, col_b, LANES, IN_TILE, LANES)
            qb_p, qb_s = _inproj2(h_p, h_s, wt_even, e, col_b, wa, IN_TILE, wa, shift_b, out_dtype=BF16)
            zb_p, zb_s = _inproj2(h_p, h_s, wt_even, e, col_b + 3 * wa, wa, IN_TILE, wa, shift_b)
            kv_s = _inproj(h_s, wt_even, e, col_b + wa, 2 * wa, IN_TILE, wa, shift_b, out_dtype=BF16)
            kv_tile = IN_TILE if new_k is not None else IN_TILE // n_even
            new_k = _inproj(h_p, wt_even, e, col_b + wa, wa, kv_tile, wa, shift_b, (new_k, e, t_p))
            new_v = _inproj(h_p, wt_even, e, col_b + 2 * wa, wa, kv_tile, wa, shift_b, (new_v, e, t_p))
            oa_p, gdn_states = _gdn(pa_p, gp_p, conv_even[e], alog_row, dtb_row, gnorm, None,
                                    (gdn_states, n_even), e, n_p, t_p)
            ob_p = _attn((qb_p, 0), (new_k, e), (new_v, e), (zb_p, 0), lam_even[e], dgain, lam_init,
                         None, e, n_p, t_p)
            out_p = _outproj([oa_p, ob_p], w_out, y_p, norm_post[layer], mods, gidx_p, next_gain, gnext_p)

            oa_s = _gdn(pa_s, gp_s, conv_even[e], alog_row, dtb_row, gnorm, state_gdn, None, e, n_s, t_s)
            ob_s = _attn((qb_s, 0), (kv_s, 0), (kv_s, 1), (zb_s, 0), lam_even[e], dgain, lam_init,
                         (cos, sin, cache_k, cache_v), e, n_s, t_s)
            out_s = _outproj([oa_s, ob_s], w_out, y_s, norm_post[layer], mods, gidx_s, next_gain, gnext_s)
        else:
            od = layer // 2
            n_main = 2 * H_C * DK_C + 2 * H_C * DV_C
            w_out = w_out_odd[od].astype(BF16)
            wg_pad = jnp.zeros((2, LANES, H_C * DK_C), F32)
            for d in range(2):
                wg_pad = wg_pad.at[d, d * GATE_RANK:(d + 1) * GATE_RANK].set(w_gate_odd[od, d])
            bgate = b_gate_odd[od].reshape(2, 1, H_C * DK_C)
            ggain = gla_norm_odd[od][None, :]

            n_qkv = 2 * H_C * DK_C + H_C * DV_C
            qkv_p, qkv_s = _inproj2(h_p, h_s, wt_odd, od, 0, n_qkv, IN_TILE, 1024, out_dtype=BF16)
            z_p, z_s = _inproj2(h_p, h_s, wt_odd, od, n_qkv, n_main - n_qkv, IN_TILE, 1024)
            glr_p, glr_s = _inproj2(h_p, h_s, wt_odd, od, n_main, LANES, IN_TILE, LANES)
            o_p, gla_states = _gla(qkv_p, z_p, glr_p, wg_pad, bgate, ggain, None, (gla_states, n_odd),
                                   od, n_p, t_p)
            out_p = _outproj([o_p], w_out, y_p, norm_post[layer], mods, gidx_p, next_gain, gnext_p)

            o_s = _gla(qkv_s, z_s, glr_s, wg_pad, bgate, ggain, state_gla, None, od, n_s, t_s)
            out_s = _outproj([o_s], w_out, y_s, norm_post[layer], mods, gidx_s, next_gain, gnext_s)
        if next_gain is None:
            y_p, y_s = out_p, out_s
        else:
            (y_p, h_p), (y_s, h_s) = out_p, out_s

    return (y_p.reshape(n_p, t_p, D_MODEL),
            y_s.reshape(n_s, t_s, D_MODEL),
            gdn_states,
            new_k.reshape(n_p, n_even, t_p, H_B, 2 * DQK_B),
            new_v.reshape(n_p, n_even, t_p, H_B, DV_B),
            gla_states)
```

```python
import functools
import math

import jax
import jax.numpy as jnp
from jax import lax
from jax.experimental import pallas as pl
from jax.experimental.pallas import tpu as pltpu

F32 = jnp.float32
BF16 = jnp.bfloat16
HIGHEST = lax.Precision.HIGHEST

D_MODEL = 2048
DEPTH = 4
EPS = 1e-6
GRID_W = 64
H_A = 8
DK_A = 128
DV_A = 128
CONV_K = 5
CHUNK_A = 64
PREP_CHAINS = 16
GDN_MAX_HEADS = 4
H_B = 8
DQK_B = 64
DV_B = 128
ROPE_BASE = 10000.0
Q_TILE = 256
Q_TILE_CTX = 256
ATTN_HEADS = 4
ATTN_HEADS_CTX = 2
H_C = 4
DK_C = 256
DV_C = 512
GATE_RANK = 16
GLA_NORMALIZER = 16.0
CHUNK_C = 128
CHUNK_C_SHORT = 256
GLA_MAX_HEADS = 4

LOG2E = math.log2(math.e)
LANES = 128
VMEM_LIMIT = 48 * 1024 * 1024
VMEM_MARGIN = 4 * 1024 * 1024
ROW_TILE = 512
OUT_SUBTILES = 4
IN_TILE = 1024


def _params(*semantics, vmem=VMEM_LIMIT):
    return pltpu.CompilerParams(dimension_semantics=semantics, vmem_limit_bytes=vmem)


def _mm(a, b):
    return jnp.dot(a.astype(BF16), b.astype(BF16), preferred_element_type=F32)


def _mm_nt(a, b):
    return lax.dot_general(a.astype(BF16), b.astype(BF16),
                           (((1,), (1,)), ((), ())), preferred_element_type=F32)


def _dot_nt(a, b):
    return lax.dot_general(a, b, (((1,), (1,)), ((), ())), preferred_element_type=F32)


def _mm_tn(a, b):
    return lax.dot_general(a.astype(BF16), b.astype(BF16),
                           (((0,), (0,)), ((), ())), preferred_element_type=F32)


def _mm_hi(a, b):
    return jnp.dot(a, b, precision=HIGHEST, preferred_element_type=F32)


def _mm_tn_hi(a, b):
    return lax.dot_general(a, b, (((0,), (0,)), ((), ())), precision=HIGHEST,
                           preferred_element_type=F32)


def _aligned(start, size, align):
    if isinstance(start, int):
        return pl.ds(start, size)
    return pl.ds(pl.multiple_of(start, align), size)


def _store_state(sf_ref, fill_slot, d, hd, state):
    if fill_slot is None:
        sf_ref[d, hd] = state
    else:
        for s in range(sf_ref.shape[0]):
            sf_ref[s, d, hd] = state if s == fill_slot else jnp.zeros_like(state)


def _silu(x):
    return x * jax.nn.sigmoid(x)


def _softplus(x):
    return jnp.maximum(x, 0.0) + jnp.log1p(jnp.exp(-jnp.abs(x)))


def _rms(x):
    return x * lax.rsqrt(jnp.mean(x * x, axis=-1, keepdims=True) + EPS)


def _adaln_kernel(c_ref, w_ref, b_ref, o_ref):
    o_ref[...] = _mm(_silu(c_ref[...]), w_ref[...]) + b_ref[...]


def _adaln(cvec, w_ada, b_ada):
    tn = 1024
    n3 = 3 * D_MODEL
    return pl.pallas_call(
        _adaln_kernel,
        grid=(DEPTH, n3 // tn),
        in_specs=[pl.BlockSpec((8, D_MODEL), lambda l, j: (0, 0)),
                  pl.BlockSpec((None, D_MODEL, tn), lambda l, j: (l, 0, j)),
                  pl.BlockSpec((None, 1, tn), lambda l, j: (l, 0, j))],
        out_specs=pl.BlockSpec((None, 8, tn), lambda l, j: (l, 0, j)),
        out_shape=jax.ShapeDtypeStruct((DEPTH, 8, n3), F32),
        compiler_params=_params("parallel", "parallel"),
        name="adaln",
    )(cvec, w_ada, b_ada.reshape(DEPTH, 1, n3))


def _prenorm_kernel(x_ref, g_ref, m_ref, h_ref):
    y = _rms(x_ref[...]) * g_ref[...]
    m = m_ref[...]
    h_ref[...] = (y * (1.0 + m[:, D_MODEL:2 * D_MODEL]) + m[:, :D_MODEL]).astype(BF16)


def _group_index(layer, group0, tiles_per_group):
    if tiles_per_group is None:
        return lambda i: (layer * 8 + group0, 0, 0)
    return lambda i: (layer * 8 + group0 + i // tiles_per_group, 0, 0)


def _prenorm(y, gain, mods, gidx):
    m = y.shape[0]
    return pl.pallas_call(
        _prenorm_kernel,
        grid=(m // ROW_TILE,),
        in_specs=[pl.BlockSpec((ROW_TILE, D_MODEL), lambda i: (i, 0)),
                  pl.BlockSpec((1, D_MODEL), lambda i: (0, 0)),
                  pl.BlockSpec((None, 1, 3 * D_MODEL), gidx)],
        out_specs=pl.BlockSpec((ROW_TILE, D_MODEL), lambda i: (i, 0)),
        out_shape=jax.ShapeDtypeStruct((m, D_MODEL), BF16),
        compiler_params=_params("parallel"),
        name="prenorm",
    )(y, gain.reshape(1, D_MODEL), mods)


def _cast_weight_block(wa_ref, wb_ref, wbf, shift, valid):
    @pl.when(pl.program_id(1) == 0)
    def _():
        w = wa_ref[...]
        if shift:
            w = jnp.concatenate([w[shift:], wb_ref[...]], axis=0)
        if valid is not None:
            w = jnp.where(lax.broadcasted_iota(jnp.int32, w.shape, 0) < valid, w, 0.0)
        wbf[...] = w.astype(BF16)


def _inproj2_kernel(*refs, shift, valid, tiles_a):
    xa_ref, xb_ref, wa_ref = refs[:3]
    wb_ref = refs[3] if shift else None
    oa_ref, ob_ref, wbf = refs[-3:]
    _cast_weight_block(wa_ref, wb_ref, wbf, shift, valid)

    @pl.when(pl.program_id(1) < tiles_a)
    def _():
        oa_ref[...] = _dot_nt(xa_ref[...], wbf[...]).astype(oa_ref.dtype)

    @pl.when(pl.program_id(1) >= tiles_a)
    def _():
        ob_ref[...] = _dot_nt(xb_ref[...], wbf[...]).astype(ob_ref.dtype)


def _inproj_kernel(*refs, shift, valid, fill_slot):
    x_ref, wa_ref = refs[:2]
    wb_ref = refs[2] if shift else None
    o_ref, wbf = refs[-2:]
    _cast_weight_block(wa_ref, wb_ref, wbf, shift, valid)
    res = _dot_nt(x_ref[...], wbf[...])
    if fill_slot is None:
        o_ref[...] = res.reshape(o_ref.shape).astype(o_ref.dtype)
    else:
        for s in range(o_ref.shape[1]):
            blk = (o_ref.shape[0],) + o_ref.shape[2:]
            o_ref[:, s] = res.reshape(blk) if s == fill_slot else jnp.zeros(blk, F32)


def _inproj(x, wt3, layer_idx, row0, n_out, tm, tn, shift=0, cache_slot=None, out_dtype=F32):
    m, k = x.shape
    n_rows = wt3.shape[1]
    valid = n_rows - row0 if row0 + n_out > n_rows else None
    assert valid is None or (n_out == tn and not shift)
    in_specs = [pl.BlockSpec((tm, k), lambda j, i: (i, 0)),
                pl.BlockSpec((None, tn, k), lambda j, i: (layer_idx, row0 // tn + j, 0))]
    args = [x, wt3]
    if shift:
        in_specs.append(pl.BlockSpec((None, shift, k),
                                     lambda j, i: (layer_idx, (row0 + (j + 1) * tn) // shift, 0)))
        args.append(wt3)
    aliases = {}
    fill_slot = None
    if cache_slot is None:
        out_spec = pl.BlockSpec((tm, tn), lambda j, i: (i, j))
        out_shape = jax.ShapeDtypeStruct((m, n_out), out_dtype)
    else:
        prev, slot, t = cache_slot
        n_slots = wt3.shape[0]
        out_shape = jax.ShapeDtypeStruct((m // t, n_slots, t, n_out), F32)
        if prev is None:
            fill_slot = slot
            out_spec = pl.BlockSpec((tm // t, n_slots, t, tn), lambda j, i: (i, 0, 0, 0))
        else:
            out_spec = pl.BlockSpec((tm // t, None, t, tn), lambda j, i: (i, slot, 0, 0))
            in_specs.append(pl.BlockSpec(memory_space=pl.ANY))
            args.append(prev)
            aliases = {len(args) - 1: 0}

    return pl.pallas_call(
        functools.partial(_inproj_kernel, shift=shift, valid=valid, fill_slot=fill_slot),
        grid=(n_out // tn, m // tm),
        in_specs=in_specs,
        out_specs=out_spec,
        out_shape=out_shape,
        scratch_shapes=[pltpu.VMEM((tn, k), BF16)],
        input_output_aliases=aliases,
        compiler_params=_params("parallel", "arbitrary"),
        name="inproj",
    )(*args)


def _inproj2(xa, xb, wt3, layer_idx, row0, n_out, tm, tn, shift=0, out_dtype=F32):
    (ma, k), mb = xa.shape, xb.shape[0]
    ta, tb = ma // tm, mb // tm
    n_rows = wt3.shape[1]
    valid = n_rows - row0 if row0 + n_out > n_rows else None
    assert valid is None or (n_out == tn and not shift)

    def a_tile(i):
        return jnp.minimum(i, ta - 1)

    def b_tile(i):
        return jnp.maximum(i - ta, 0)

    in_specs = [pl.BlockSpec((tm, k), lambda j, i: (a_tile(i), 0)),
                pl.BlockSpec((tm, k), lambda j, i: (b_tile(i), 0)),
                pl.BlockSpec((None, tn, k), lambda j, i: (layer_idx, row0 // tn + j, 0))]
    args = [xa, xb, wt3]
    if shift:
        in_specs.append(pl.BlockSpec((None, shift, k),
                                     lambda j, i: (layer_idx, (row0 + (j + 1) * tn) // shift, 0)))
        args.append(wt3)
    return pl.pallas_call(
        functools.partial(_inproj2_kernel, shift=shift, valid=valid, tiles_a=ta),
        grid=(n_out // tn, ta + tb),
        in_specs=in_specs,
        out_specs=(pl.BlockSpec((tm, tn), lambda j, i: (a_tile(i), j)),
                   pl.BlockSpec((tm, tn), lambda j, i: (b_tile(i), j))),
        out_shape=(jax.ShapeDtypeStruct((ma, n_out), out_dtype),
                   jax.ShapeDtypeStruct((mb, n_out), out_dtype)),
        scratch_shapes=[pltpu.VMEM((tn, k), BF16)],
        compiler_params=_params("parallel", "arbitrary",
                                vmem=4 * tm * k * 2 + 4 * tm * tn * 4 + 2 * tn * k * 4 + tn * k * 2
                                + 2 * shift * k * 4 + VMEM_MARGIN),
        name="inproj2",
    )(*args)


def _outproj_kernel(*refs, n_in, emit_h):
    x_refs = refs[:n_in]
    w_ref, y_ref, g_ref, m_ref = refs[n_in:n_in + 4]
    if emit_h:
        gn_ref, mn_ref, o_ref, h_ref = refs[n_in + 4:]
    else:
        o_ref = refs[n_in + 4]
    m = m_ref[...]
    sub = o_ref.shape[0] // OUT_SUBTILES
    accs = []
    for s in range(OUT_SUBTILES):
        rows = slice(s * sub, (s + 1) * sub)
        acc = None
        off = 0
        for x_ref in x_refs:
            kk = x_ref.shape[1]
            part = jnp.dot(x_ref[rows, :], w_ref[off:off + kk, :], preferred_element_type=F32)
            acc = part if acc is None else acc + part
            off += kk
        accs.append(acc)
    for s in range(OUT_SUBTILES):
        rows = slice(s * sub, (s + 1) * sub)
        y = y_ref[rows, :] + m[:, 2 * D_MODEL:] * (_rms(accs[s]) * g_ref[...])
        if emit_h:
            mn = mn_ref[...]
            h_ref[rows, :] = (_rms(y) * gn_ref[...] * (1.0 + mn[:, D_MODEL:2 * D_MODEL])
                              + mn[:, :D_MODEL]).astype(BF16)
        o_ref[rows, :] = y


def _outproj(xs, w, y, gain, mods, gidx, next_gain=None, gidx_next=None):
    m = y.shape[0]
    emit_h = next_gain is not None
    row_spec = pl.BlockSpec((ROW_TILE, D_MODEL), lambda i: (i, 0))
    vec_spec = pl.BlockSpec((1, D_MODEL), lambda i: (0, 0))
    in_specs = [pl.BlockSpec((ROW_TILE, x.shape[1]), lambda i: (i, 0)) for x in xs]
    in_specs += [pl.BlockSpec(w.shape, lambda i: (0, 0)), row_spec, vec_spec,
                 pl.BlockSpec((None, 1, 3 * D_MODEL), gidx)]
    args = [*xs, w, y, gain.reshape(1, D_MODEL), mods]
    out_specs = row_spec
    out_shape = jax.ShapeDtypeStruct((m, D_MODEL), F32)
    if emit_h:
        in_specs += [vec_spec, pl.BlockSpec((None, 1, 3 * D_MODEL), gidx_next)]
        args += [next_gain.reshape(1, D_MODEL), mods]
        out_specs = (row_spec, row_spec)
        out_shape = (out_shape, jax.ShapeDtypeStruct((m, D_MODEL), BF16))
    return pl.pallas_call(
        functools.partial(_outproj_kernel, n_in=len(xs), emit_h=emit_h),
        grid=(m // ROW_TILE,),
        in_specs=in_specs,
        out_specs=out_specs,
        out_shape=out_shape,
        compiler_params=_params("parallel"),
        name="outproj",
    )(*args)


def _gdn_kernel(*refs, seq_len, heads, has_state, fill_slot):
    t = seq_len
    c = CHUNK_A
    nc = t // c
    (q_ref, k_ref, v_ref, z_ref, gp_ref, cwq_ref, cwk_ref, cwv_ref,
     alog_ref, dtb_ref, gn_ref) = refs[:11]
    qn, kn, vn, bt, gt, wv_s, wkqd_s, qk_s, kd_s, gl_s, oacc, s_s, xpad = refs[-13:]
    if has_state:
        s0_ref, o_ref = refs[11], refs[-14]
    else:
        o_ref, sf_ref = refs[-15:-13]
    head0 = pl.program_id(1) * heads

    def lanes_of(hd):
        return slice(hd * LANES, (hd + 1) * LANES)

    row_t = lax.broadcasted_iota(jnp.int32, (t, LANES), 0)
    xpad[0:8, :] = jnp.zeros((8, LANES), F32)
    xpad[8 + t:, :] = jnp.zeros((8, LANES), F32)

    def conv_silu(x_ref, w_ref, hd):
        w = w_ref[:, lanes_of(hd)]
        xpad[8:8 + t, :] = x_ref[:, lanes_of(hd)]
        acc = None
        for tap in range(CONV_K):
            start = 8 + tap - CONV_K // 2
            term = xpad[start:start + t, :] * w[tap:tap + 1, :]
            acc = term if acc is None else acc + term
        return _silu(acc)

    def l2n(x):
        return x * lax.rsqrt(jnp.sum(x * x, axis=-1, keepdims=True) + EPS)

    for hd in range(heads):
        qn[hd] = l2n(conv_silu(q_ref, cwq_ref, hd))
        kn[hd] = l2n(conv_silu(k_ref, cwk_ref, hd))
        vn[hd] = conv_silu(v_ref, cwv_ref, hd)

    gp = gp_ref[...]
    lane_t = lax.broadcasted_iota(jnp.int32, (t, LANES), 1)
    beta_all = jax.nn.sigmoid(gp)
    g_all = -jnp.exp(alog_ref[...]) * _softplus(gp + dtb_ref[...])

    def pick(x, idx):
        col = jnp.sum(jnp.where(lane_t == idx, x, 0.0), axis=-1, keepdims=True)
        return jnp.broadcast_to(col, (t, LANES))

    pos_t = row_t & (c - 1)
    g_cum = [g_all, g_all]
    sh = 1
    while sh < c:
        g_cum[0] = g_cum[0] + jnp.where(pos_t >= sh, pltpu.roll(g_cum[0], sh, axis=0), 0.0)
        g_cum[1] = g_cum[1] + jnp.where(pos_t < c - sh, pltpu.roll(g_cum[1], t - sh, axis=0), 0.0)
        sh *= 2

    for hd in range(heads):
        for d in range(2):
            bt[hd * 2 + d] = pick(beta_all, d * H_A + head0 + hd)
            gt[hd * 2 + d] = pick(g_cum[d], 2 * H_A + d * H_A + head0 + hd)

    r2 = 2 * c
    row = lax.broadcasted_iota(jnp.int32, (r2, r2), 0)
    col = lax.broadcasted_iota(jnp.int32, (r2, r2), 1)
    rc_xor = row ^ col

    tri_inc, tri_str = [], []
    for rev in (False, True):
        after = (col - row) if rev else (row - col)
        after = jnp.where(rc_xor < c, after, -1)
        tri_inc.append(after >= 0)
        tri_str.append(after > 0)
    eye = jnp.where(row == col, 1.0, 0.0)

    def prep(chains):
        n = range(len(chains))
        rows = [_aligned(gi * r2, r2, r2) for _, gi, _ in chains]
        dirs = [d for _, _, d in chains]
        hdir = [hd * 2 + d for hd, _, d in chains]
        q = [qn[chains[i][0], rows[i], :] * DK_A ** -0.5 for i in n]
        k = [kn[chains[i][0], rows[i], :] for i in n]
        beta = [bt[hdir[i], rows[i], :] for i in n]
        kb = [k[i] * beta[i] for i in n]
        kk = [_mm_nt(kb[i], k[i]) for i in n]
        qk = [_mm_nt(q[i], k[i]) for i in n]
        gcum = [gt[hdir[i], rows[i], :] for i in n]
        decay = [jnp.where(tri_inc[dirs[i]], jnp.exp(jnp.minimum(gcum[i] - gcum[i].T, 0.0)), 0.0) for i in n]
        amat = [jnp.where(tri_str[dirs[i]], kk[i] * decay[i], 0.0) for i in n]
        tinv = [eye - jnp.where(rc_xor < 2, amat[i], 0.0) for i in n]
        b = 2
        while b < c:
            level = (rc_xor >= b) & (rc_xor < 2 * b)
            m1 = [_mm(jnp.where(level, amat[i], 0.0), tinv[i]) for i in n]
            m2 = [_mm(tinv[i], m1[i]) for i in n]
            tinv = [tinv[i] - m2[i] for i in n]
            b *= 2
        eg = [jnp.exp(gcum[i]) for i in n]
        x = [_mm(tinv[i], jnp.concatenate([vn[chains[i][0], rows[i], :] * beta[i], kb[i] * eg[i]], axis=1))
             for i in n]
        for i in n:
            _, gi, d = chains[i]
            hd2 = hdir[i]
            qkd = qk[i] * decay[i]
            g_end = _block_row_bcast(gcum[i], c, 0 if d == 1 else c - 1)
            qd = q[i] * eg[i]
            gl = jnp.exp(g_end)
            wv_s[hd2, rows[i], :] = x[i][:, :DV_A]
            qk_s[hd2, rows[i], :] = jnp.where(row < c, qkd, pltpu.roll(qkd, c, axis=1))[:, :c]
            kd_s[hd2, rows[i], :] = k[i] * jnp.exp(g_end - gcum[i])
            for j in range(2):
                base = gi * 2 * r2 + j * r2
                wkqd_s[hd2, _aligned(base, c, c), :] = x[i][j * c:(j + 1) * c, DV_A:]
                wkqd_s[hd2, _aligned(base + c, c, c), :] = qd[j * c:(j + 1) * c, :]
                gl_s[hd2, _aligned(gi * 16 + j * 8, 8, 8), :] = gl[j * c:j * c + 8, :]

    def rec(chains):
        n = range(len(chains))
        rows = [_aligned(ci * c, c, c) for _, ci, _ in chains]
        hdir = [hd * 2 + d for hd, _, d in chains]
        s = [s_s[hdir[i]] for i in n]
        ws = [_mm(wkqd_s[hdir[i], _aligned(chains[i][1] * 2 * c, 2 * c, 2 * c), :], s[i]) for i in n]
        u = [wv_s[hdir[i], rows[i], :] - ws[i][:c] for i in n]
        o2 = [_mm(qk_s[hdir[i], rows[i], :], u[i]) for i in n]
        ds = [_mm_tn(kd_s[hdir[i], rows[i], :], u[i]) for i in n]
        for i in n:
            s_s[hdir[i]] = s[i] * gl_s[hdir[i], _aligned(chains[i][1] * 8, 1, 8), :] + ds[i]
        for i in n:
            oacc[chains[i][0], rows[i], :] += ws[i][c:] + o2[i]

    for hd in range(heads):
        for d in range(2):
            if has_state:
                s_s[hd * 2 + d] = s0_ref[d, hd]
            else:
                s_s[hd * 2 + d] = jnp.zeros((DK_A, DV_A), F32)
    oacc[...] = jnp.zeros((heads, t, DV_A), F32)

    n_groups = nc // 2
    unroll = max(1, min(n_groups, PREP_CHAINS // (2 * heads)))

    def prep_body(gi, carry):
        prep([(hd, gi * unroll + j, d) for hd in range(heads) for j in range(unroll) for d in range(2)])
        return carry

    if n_groups == unroll:
        prep_body(0, 0)
    else:
        lax.fori_loop(0, n_groups // unroll, prep_body, 0)

    def rec_body(i, carry):
        rec([(hd, i if d == 0 else nc - 1 - i, d) for hd in range(heads) for d in range(2)])
        return carry

    lax.fori_loop(0, nc, rec_body, 0)

    for hd in range(heads):
        o_ref[:, lanes_of(hd)] = (_rms(oacc[hd]) * gn_ref[...] * _silu(z_ref[:, lanes_of(hd)])).astype(BF16)
        if not has_state:
            for d in range(2):
                _store_state(sf_ref, fill_slot, d, hd, s_s[hd * 2 + d])


def _gdn_heads(t):
    for g in (GDN_MAX_HEADS, 2, 1):
        scratch = g * t * LANES * 4 * 17 + 2 * g * DK_A * DV_A * 4
        blocks = 2 * (4 * t * g * LANES * 4 + t * LANES * 4 + t * g * LANES * 2)
        if scratch + blocks <= VMEM_LIMIT - (8 << 20):
            return g
    return 1


def _gdn(proj, gp, conv_w, alog_row, dtb_row, gnorm, state, state_out, layer_e, nseq, seq_len):
    t = seq_len
    nc = t // CHUNK_A
    has_state = state is not None
    aliases = {}
    fill_slot = None

    g = _gdn_heads(t)
    w = g * LANES
    nb = H_A // g

    def col(seg):
        return pl.BlockSpec((t, w), lambda b, h: (b, seg * nb + h))

    def cw(seg):
        return pl.BlockSpec((CONV_K, w), lambda b, h: (0, seg * nb + h))

    row_spec = pl.BlockSpec((1, LANES), lambda b, h: (0, 0))
    in_specs = [col(0), col(1), col(2), col(3),
                pl.BlockSpec((t, LANES), lambda b, h: (b, 0)),
                cw(0), cw(1), cw(2), row_spec, row_spec, row_spec]
    args = [proj, proj, proj, proj, gp, conv_w, conv_w, conv_w, alog_row, dtb_row, gnorm]
    o_shape = jax.ShapeDtypeStruct((nseq * t, H_A * DV_A), BF16)
    o_spec = pl.BlockSpec((t, w), lambda b, h: (b, h))
    if has_state:
        in_specs.append(pl.BlockSpec((None, None, 2, g, DK_A, DV_A),
                                     lambda b, h: (b, layer_e, 0, h, 0, 0)))
        args.append(state)
        out_shape, out_specs = o_shape, o_spec
    else:
        prev, n_slots = state_out
        out_shape = (o_shape, jax.ShapeDtypeStruct((nseq, n_slots, 2, H_A, DK_A, DV_A), F32))
        if prev is None:
            fill_slot = layer_e
            s_spec = pl.BlockSpec((None, n_slots, 2, g, DK_A, DV_A), lambda b, h: (b, 0, 0, h, 0, 0))
        else:
            s_spec = pl.BlockSpec((None, None, 2, g, DK_A, DV_A), lambda b, h: (b, layer_e, 0, h, 0, 0))
            in_specs.append(pl.BlockSpec(memory_space=pl.ANY))
            args.append(prev)
            aliases = {len(args) - 1: 1}
        out_specs = (o_spec, s_spec)
    scratch = [pltpu.VMEM((g, t, LANES), F32)] * 3
    scratch += [pltpu.VMEM((2 * g, t, LANES), F32)] * 2
    scratch += [pltpu.VMEM((2 * g, t, LANES), F32),
                pltpu.VMEM((2 * g, 2 * t, LANES), F32),
                pltpu.VMEM((2 * g, t, CHUNK_A), F32),
                pltpu.VMEM((2 * g, t, LANES), F32),
                pltpu.VMEM((2 * g, nc * 8, LANES), F32),
                pltpu.VMEM((g, t, LANES), F32),
                pltpu.VMEM((2 * g, DK_A, DV_A), F32),
                pltpu.VMEM((t + 16, LANES), F32)]
    return pl.pallas_call(
        functools.partial(_gdn_kernel, seq_len=t, heads=g, has_state=has_state, fill_slot=fill_slot),
        grid=(nseq, nb),
        in_specs=in_specs,
        out_specs=out_specs,
        out_shape=out_shape,
        scratch_shapes=scratch,
        input_output_aliases=aliases,
        compiler_params=_params("parallel", "parallel"),
        name="gdn",
    )(*args)


def _rope(x, cos, sin):
    x = x.astype(F32)
    lane = lax.broadcasted_iota(jnp.int32, x.shape, 1)
    quarter = DQK_B // 4
    rot = jnp.where((lane & quarter) == 0,
                    -pltpu.roll(x, LANES - quarter, axis=1),
                    pltpu.roll(x, quarter, axis=1))
    return x * cos + rot * sin


def _attn_kernel(*refs, seq_len, heads, lam_init, has_ctx):
    t = seq_len
    q_ref, k_ref, v_ref, z_ref, lam_ref, dg_ref = refs[:6]
    if has_ctx:
        cq_ref, sq_ref, ck_ref, sk_ref, kc_ref, vc_ref, o_ref, k_all, v_all = refs[6:]
    else:
        o_ref = refs[6]

    def lanes_of(hd):
        return slice(hd * LANES, (hd + 1) * LANES)

    if has_ctx:
        @pl.when(pl.program_id(2) == 0)
        def _():
            for hd in range(heads):
                k_all[hd, 0:t, :] = _rope(k_ref[:, lanes_of(hd)], ck_ref[...], sk_ref[...])
                k_all[hd, t:, :] = kc_ref[:, lanes_of(hd)]
                v_all[hd, 0:t, :] = v_ref[:, lanes_of(hd)].astype(F32)
                v_all[hd, t:, :] = vc_ref[:, lanes_of(hd)]
    lp = lam_ref[...]
    lam = (jnp.exp(jnp.sum(lp[0:1] * lp[1:2], axis=-1, keepdims=True))
           - jnp.exp(jnp.sum(lp[2:3] * lp[3:4], axis=-1, keepdims=True)) + lam_init)
    lane = lax.broadcasted_iota(jnp.int32, (q_ref.shape[0], LANES), 1)

    chains = [(hd, m) for hd in range(heads) for m in range(2)]
    n = range(len(chains))
    q = []
    for hd in range(heads):
        qh = q_ref[:, lanes_of(hd)]
        if has_ctx:
            qh = _rope(qh, cq_ref[...], sq_ref[...])
        q.append(qh * DQK_B ** -0.5)
    k = [k_all[hd] if has_ctx else k_ref[:, lanes_of(hd)] for hd in range(heads)]
    v = [v_all[hd] if has_ctx else v_ref[:, lanes_of(hd)] for hd in range(heads)]
    qm = [jnp.where(lane < DQK_B, q[hd], 0.0) if m == 0 else jnp.where(lane < DQK_B, 0.0, q[hd])
          for hd, m in chains]
    s = [_mm_nt(qm[i], k[chains[i][0]]) for i in n]
    e = [jnp.exp(s[i] - jnp.max(s[i], axis=-1, keepdims=True)) for i in n]
    den = [jnp.sum(e[i], axis=-1, keepdims=True) for i in n]
    ev = [_mm(e[i], v[chains[i][0]]) for i in n]
    for hd in range(heads):
        o = ev[2 * hd] / den[2 * hd] - lam * (ev[2 * hd + 1] / den[2 * hd + 1])
        o_ref[:, lanes_of(hd)] = (_rms(o) * dg_ref[...] * (1.0 - lam_init)
                                  * _silu(z_ref[:, lanes_of(hd)])).astype(BF16)


def _attn(q, k, v, z, lam_p, dgain, lam_init, ctx, layer_e, nseq, seq_len):
    t = seq_len
    has_ctx = ctx is not None
    g = ATTN_HEADS_CTX if has_ctx else ATTN_HEADS
    tq = min(t, Q_TILE_CTX if has_ctx else Q_TILE)
    nq = t // tq
    w = g * LANES
    nb = H_B // g

    def rows_spec(src, rows):
        arr, seg = src
        if arr.ndim == 4:
            assert rows == t
            return pl.BlockSpec((None, None, t, w), lambda b, h, i: (b, seg, 0, h))
        if rows == t:
            return pl.BlockSpec((t, w), lambda b, h, i: (b, seg * nb + h))
        return pl.BlockSpec((rows, w), lambda b, h, i: (b * nq + i, seg * nb + h))

    in_specs = [rows_spec(q, tq), rows_spec(k, t), rows_spec(v, t), rows_spec(z, tq),
                pl.BlockSpec((4, DQK_B), lambda b, h, i: (0, 0)),
                pl.BlockSpec((1, LANES), lambda b, h, i: (0, 0))]
    args = [q[0], k[0], v[0], z[0], lam_p, dgain]
    scratch = []
    if has_ctx:
        cos, sin, cache_k, cache_v = ctx
        past = cache_k.shape[2]
        in_specs += [pl.BlockSpec((tq, LANES), lambda b, h, i: (i, 0)),
                     pl.BlockSpec((tq, LANES), lambda b, h, i: (i, 0)),
                     pl.BlockSpec((t, LANES), lambda b, h, i: (0, 0)),
                     pl.BlockSpec((t, LANES), lambda b, h, i: (0, 0)),
                     pl.BlockSpec((None, None, past, w), lambda b, h, i: (b, layer_e, 0, h)),
                     pl.BlockSpec((None, None, past, w), lambda b, h, i: (b, layer_e, 0, h))]
        args += [cos, sin, cos, sin, cache_k.reshape(cache_k.shape[:3] + (-1,)),
                 cache_v.reshape(cache_v.shape[:3] + (-1,))]
        scratch = [pltpu.VMEM((g, t + past, LANES), F32)] * 2
    return pl.pallas_call(
        functools.partial(_attn_kernel, seq_len=t, heads=g, lam_init=lam_init, has_ctx=has_ctx),
        grid=(nseq, nb, nq),
        in_specs=in_specs,
        out_specs=pl.BlockSpec((tq, w), lambda b, h, i: (b * nq + i, h)),
        out_shape=jax.ShapeDtypeStruct((nseq * t, H_B * DV_B), BF16),
        scratch_shapes=scratch,
        compiler_params=_params("parallel", "parallel", "arbitrary"),
        name="diffattn",
    )(*args)


def _block_row_bcast(x, blk, r):
    n, w = x.shape
    if blk >= 8:
        x3 = x.reshape(n // blk, blk, w)
        return jnp.broadcast_to(x3[:, r:r + 1, :], x3.shape).reshape(n, w)
    x3 = x.reshape(n // 8, 8, w)
    sub = lax.broadcasted_iota(jnp.int32, x3.shape, 1)
    out = None
    for b in range(8 // blk):
        cand = jnp.broadcast_to(x3[:, b * blk + r:b * blk + r + 1, :], x3.shape)
        out = cand if out is None else jnp.where(sub >= b * blk, cand, out)
    return out.reshape(n, w)


def _gla_kernel(*refs, seq_len, heads, has_state, fill_slot):
    t = seq_len
    c = t if t <= CHUNK_C_SHORT else CHUNK_C
    nc = t // c
    q_ref, k_ref, v_ref, z_ref, glr_ref, wg_ref, bg_ref, gg_ref = refs[:8]
    la_s, oacc, st_s = refs[-3:]
    if has_state:
        s0_ref, o_ref = refs[8], refs[-4]
    else:
        o_ref, sf_ref = refs[-5:-3]

    def kcols(hd):
        return slice(hd * DK_C, (hd + 1) * DK_C)

    def vcols(hd):
        return slice(hd * DV_C, (hd + 1) * DV_C)

    glr = glr_ref[...]
    for hd in range(heads):
        for d in range(2):
            gate = _mm(glr, wg_ref[d, :, kcols(hd)]) + bg_ref[d, :, kcols(hd)]
            la_s[hd * 2 + d] = ((jnp.minimum(gate, 0.0) - jnp.log(1.0 + jnp.exp(-jnp.abs(gate))))
                                * (LOG2E / GLA_NORMALIZER))

    row = lax.broadcasted_iota(jnp.int32, (c, c), 0)
    col = lax.broadcasted_iota(jnp.int32, (c, c), 1)
    top = row ^ col
    sh = 1
    while sh < c:
        top = top | (top >> sh)
        sh *= 2
    top = (top + 1) >> 1
    pair_code = jnp.where(row > col, top, -top)
    tri = [jnp.where(row >= col, 1.0, 0.0).astype(BF16), jnp.where(row <= col, 1.0, 0.0).astype(BF16)]

    def chunks(chains):
        n = range(len(chains))
        rows = [_aligned(ci * c, c, c) for _, ci, _ in chains]
        rev = [d == 1 for _, _, d in chains]
        hdir = [hd * 2 + d for hd, _, d in chains]
        q = [q_ref[rows[i], kcols(chains[i][0])] * DK_C ** -0.5 for i in n]
        k = [k_ref[rows[i], kcols(chains[i][0])] for i in n]
        v = [v_ref[rows[i], vcols(chains[i][0])] for i in n]
        qb = [q[i].astype(BF16) for i in n]
        kb = [k[i].astype(BF16) for i in n]
        attn = [jnp.where(row == col, _dot_nt(qb[i], kb[i]), 0.0) for i in n]
        la = [la_s[hdir[i], rows[i], :] for i in n]
        la_hi = [la[i].astype(BF16) for i in n]
        la_lo = [(la[i] - la_hi[i].astype(F32)).astype(BF16) for i in n]
        bc = [jnp.dot(tri[chains[i][2]], la_hi[i], preferred_element_type=F32)
              + jnp.dot(tri[chains[i][2]], la_lo[i], preferred_element_type=F32) for i in n]
        s = c // 2
        while s >= 1:
            for i in n:
                bref = _block_row_bcast(bc[i], 2 * s, s if rev[i] else s - 1)
                e = jnp.exp2(-jnp.abs(bc[i] - bref)).astype(BF16)
                pairs = pair_code == (-s if rev[i] else s)
                attn[i] = attn[i] + jnp.where(pairs, _dot_nt(qb[i] * e, kb[i] * e), 0.0)
            s //= 2
        b_end = [bc[i][0:1, :] if rev[i] else bc[i][c - 1:c, :] for i in n]
        st = [st_s[hdir[i]] for i in n]
        o = [_mm(attn[i], v[i]) + _mm_nt(q[i] * jnp.exp2(bc[i]), st[i]) for i in n]
        upd = [_mm_tn(v[i], k[i] * jnp.exp2(b_end[i] - bc[i])) for i in n]
        for i in n:
            st_s[hdir[i]] = st[i] * jnp.exp2(b_end[i]) + upd[i]
        for i in n:
            oacc[chains[i][0], rows[i], :] += o[i]

    for hd in range(heads):
        for d in range(2):
            if has_state:
                st_s[hd * 2 + d] = s0_ref[d, hd].T
            else:
                st_s[hd * 2 + d] = jnp.zeros((DV_C, DK_C), F32)
    oacc[...] = jnp.zeros((heads, t, DV_C), F32)

    def body(i, carry):
        chunks([(hd, i if d == 0 else nc - 1 - i, d) for hd in range(heads) for d in range(2)])
        return carry

    lax.fori_loop(0, nc, body, 0)

    for hd in range(heads):
        o_ref[:, vcols(hd)] = (_rms(oacc[hd]) * gg_ref[...] * _silu(z_ref[:, vcols(hd)])).astype(BF16)
        if not has_state:
            for d in range(2):
                _store_state(sf_ref, fill_slot, d, hd, st_s[hd * 2 + d].T)


def _gla_heads(t):
    for g in (GLA_MAX_HEADS, 2, 1):
        scratch = g * (2 * t * DK_C + t * DV_C + 2 * DV_C * DK_C) * 4
        blocks = 2 * (g * t * ((2 * DK_C + DV_C) * 2 + DV_C * 4) + t * LANES * 4 + g * t * DV_C * 2
                      + 2 * g * DK_C * DV_C * 4)
        if scratch + blocks <= VMEM_LIMIT - (12 << 20):
            return g
    return 1


def _gla(qkv, zproj, glr, wg_pad, bgate, ggain, state, state_out, layer_o, nseq, seq_len):
    t = seq_len
    has_state = state is not None
    aliases = {}
    fill_slot = None
    g = _gla_heads(t)
    nb = H_C // g
    wk, wv = g * DK_C, g * DV_C
    in_specs = [pl.BlockSpec((t, wk), lambda b, h: (b, h)),
                pl.BlockSpec((t, wk), lambda b, h: (b, nb + h)),
                pl.BlockSpec((t, wv), lambda b, h: (b, nb + h)),
                pl.BlockSpec((t, wv), lambda b, h: (b, h)),
                pl.BlockSpec((t, LANES), lambda b, h: (b, 0)),
                pl.BlockSpec((2, LANES, wk), lambda b, h: (0, 0, h)),
                pl.BlockSpec((2, 1, wk), lambda b, h: (0, 0, h)),
                pl.BlockSpec((1, DV_C), lambda b, h: (0, 0))]
    args = [qkv, qkv, qkv, zproj, glr, wg_pad, bgate, ggain]
    o_shape = jax.ShapeDtypeStruct((nseq * t, H_C * DV_C), BF16)
    o_spec = pl.BlockSpec((t, wv), lambda b, h: (b, h))
    if has_state:
        in_specs.append(pl.BlockSpec((None, None, 2, g, DK_C, DV_C),
                                     lambda b, h: (b, layer_o, 0, h, 0, 0)))
        args.append(state)
        out_shape, out_specs = o_shape, o_spec
    else:
        prev, n_slots = state_out
        out_shape = (o_shape, jax.ShapeDtypeStruct((nseq, n_slots, 2, H_C, DK_C, DV_C), F32))
        if prev is None:
            fill_slot = layer_o
            s_spec = pl.BlockSpec((None, n_slots, 2, g, DK_C, DV_C), lambda b, h: (b, 0, 0, h, 0, 0))
        else:
            s_spec = pl.BlockSpec((None, None, 2, g, DK_C, DV_C), lambda b, h: (b, layer_o, 0, h, 0, 0))
            in_specs.append(pl.BlockSpec(memory_space=pl.ANY))
            args.append(prev)
            aliases = {len(args) - 1: 1}
        out_specs = (o_spec, s_spec)
    scratch = [pltpu.VMEM((2 * g, t, DK_C), F32),
               pltpu.VMEM((g, t, DV_C), F32),
               pltpu.VMEM((2 * g, DV_C, DK_C), F32)]
    return pl.pallas_call(
        functools.partial(_gla_kernel, seq_len=t, heads=g, has_state=has_state, fill_slot=fill_slot),
        grid=(nseq, nb),
        in_specs=in_specs,
        out_specs=out_specs,
        out_shape=out_shape,
        scratch_shapes=scratch,
        input_output_aliases=aliases,
        compiler_params=_params("parallel", "parallel"),
        name="gla",
    )(*args)


def _rope_tables(n_tokens):
    rows = n_tokens // GRID_W
    row = jnp.repeat(jnp.arange(rows), GRID_W).astype(F32)
    col = jnp.tile(jnp.arange(GRID_W), rows).astype(F32)
    quarter = DQK_B // 4
    freqs = ROPE_BASE ** (-jnp.arange(quarter, dtype=F32) / quarter)
    ar = row[:, None] * freqs
    ac = col[:, None] * freqs
    ang = jnp.concatenate([ar, ar, ac, ac] * (LANES // DQK_B), axis=-1)
    return jnp.cos(ang), jnp.sin(ang)


def _pad_lanes(x):
    return jnp.pad(x, ((0, 0), (0, LANES - x.shape[1])))


def kernel(x_prompt, x_sample, c, state_gdn, cache_k, cache_v, state_gla, c_ctx, w_ada, b_ada, norm_pre, norm_post, w_in_even, conv_even, a_log_even, dt_bias_even, gdn_norm_even, lam_even, diff_norm_even, w_out_even, w_in_odd, w_gate_odd, b_gate_odd, gla_norm_odd, w_out_odd):
    n_p, t_p, _ = x_prompt.shape
    n_s, t_s, _ = x_sample.shape
    wa = H_A * DK_A

    cvec = jnp.zeros((8, D_MODEL), F32).at[0].set(c_ctx).at[1:1 + n_s].set(c)
    mods = _adaln(cvec, w_ada, b_ada).reshape(DEPTH * 8, 1, 3 * D_MODEL)
    cos, sin = _rope_tables(t_s)

    y_p = x_prompt.reshape(n_p * t_p, D_MODEL)
    y_s = x_sample.reshape(n_s * t_s, D_MODEL)
    n_even, n_odd = w_in_even.shape[0], w_in_odd.shape[0]
    wt_even = jnp.swapaxes(w_in_even, 1, 2)
    wt_odd = jnp.swapaxes(w_in_odd, 1, 2)
    gdn_states = new_k = new_v = gla_states = None

    def gidx(layer):
        return _group_index(layer, 0, None), _group_index(layer, 1, t_s // ROW_TILE)

    gidx_p, gidx_s = gidx(0)
    h_p = _prenorm(y_p, norm_pre[0], mods, gidx_p)
    h_s = _prenorm(y_s, norm_pre[0], mods, gidx_s)

    for layer in range(DEPTH):
        gidx_p, gidx_s = gidx(layer)
        if layer + 1 < DEPTH:
            next_gain = norm_pre[layer + 1]
            gnext_p, gnext_s = gidx(layer + 1)
        else:
            next_gain = gnext_p = gnext_s = None
        if layer % 2 == 0:
            e = layer // 2
            lam_init = 0.8 - 0.6 * math.exp(-0.3 * layer)
            w_out = w_out_even[e].astype(BF16)
            zeros16 = jnp.zeros((2 * H_A,), F32)
            alog_row = _pad_lanes(jnp.concatenate([zeros16, a_log_even[e].reshape(-1)])[None, :])
            dtb_row = _pad_lanes(jnp.concatenate([zeros16, dt_bias_even[e].reshape(-1)])[None, :])
            gnorm = gdn_norm_even[e][None, :]
            dgain = diff_norm_even[e][None, :]
            col_b = 4 * wa
            shift_b = 4 * H_A

            pa_p, pa_s = _inproj2(h_p, h_s, wt_even, e, 0, 4 * wa, IN_TILE, 1024)
            gp_p, gp_s = _inproj2(h_p, h_s, wt_even, e, col_b, LANES, IN_TILE, LANES)
            qb_p, qb_s = _inproj2(h_p, h_s, wt_even, e, col_b, wa, IN_TILE, wa, shift_b, out_dtype=BF16)
            zb_p, zb_s = _inproj2(h_p, h_s, wt_even, e, col_b + 3 * wa, wa, IN_TILE, wa, shift_b)
            kv_s = _inproj(h_s, wt_even, e, col_b + wa, 2 * wa, IN_TILE, wa, shift_b, out_dtype=BF16)
            kv_tile = IN_TILE if new_k is not None else IN_TILE // n_even
            new_k = _inproj(h_p, wt_even, e, col_b + wa, wa, kv_tile, wa, shift_b, (new_k, e, t_p))
            new_v = _inproj(h_p, wt_even, e, col_b + 2 * wa, wa, kv_tile, wa, shift_b, (new_v, e, t_p))
            oa_p, gdn_states = _gdn(pa_p, gp_p, conv_even[e], alog_row, dtb_row, gnorm, None,
                                    (gdn_states, n_even), e, n_p, t_p)
            ob_p = _attn((qb_p, 0), (new_k, e), (new_v, e), (zb_p, 0), lam_even[e], dgain, lam_init,
                         None, e, n_p, t_p)
            out_p = _outproj([oa_p, ob_p], w_out, y_p, norm_post[layer], mods, gidx_p, next_gain, gnext_p)

            oa_s = _gdn(pa_s, gp_s, conv_even[e], alog_row, dtb_row, gnorm, state_gdn, None, e, n_s, t_s)
            ob_s = _attn((qb_s, 0), (kv_s, 0), (kv_s, 1), (zb_s, 0), lam_even[e], dgain, lam_init,
                         (cos, sin, cache_k, cache_v), e, n_s, t_s)
            out_s = _outproj([oa_s, ob_s], w_out, y_s, norm_post[layer], mods, gidx_s, next_gain, gnext_s)
        else:
            od = layer // 2
            n_main = 2 * H_C * DK_C + 2 * H_C * DV_C
            w_out = w_out_odd[od].astype(BF16)
            wg_pad = jnp.zeros((2, LANES, H_C * DK_C), F32)
            for d in range(2):
                wg_pad = wg_pad.at[d, d * GATE_RANK:(d + 1) * GATE_RANK].set(w_gate_odd[od, d])
            bgate = b_gate_odd[od].reshape(2, 1, H_C * DK_C)
            ggain = gla_norm_odd[od][None, :]

            n_qkv = 2 * H_C * DK_C + H_C * DV_C
            qkv_p, qkv_s = _inproj2(h_p, h_s, wt_odd, od, 0, n_qkv, IN_TILE, 1024, out_dtype=BF16)
            z_p, z_s = _inproj2(h_p, h_s, wt_odd, od, n_qkv, n_main - n_qkv, IN_TILE, 1024)
            glr_p, glr_s = _inproj2(h_p, h_s, wt_odd, od, n_main, LANES, IN_TILE, LANES)
            o_p, gla_states = _gla(qkv_p, z_p, glr_p, wg_pad, bgate, ggain, None, (gla_states, n_odd),
                                   od, n_p, t_p)
            out_p = _outproj([o_p], w_out, y_p, norm_post[layer], mods, gidx_p, next_gain, gnext_p)

            o_s = _gla(qkv_s, z_s, glr_s, wg_pad, bgate, ggain, state_gla, None, od, n_s, t_s)
            out_s = _outproj([o_s], w_out, y_s, norm_post[layer], mods, gidx_s, next_gain, gnext_s)
        if next_gain is None:
            y_p, y_s = out_p, out_s
        else:
            (y_p, h_p), (y_s, h_s) = out_p, out_s

    return (y_p.reshape(n_p, t_p, D_MODEL),
            y_s.reshape(n_s, t_s, D_MODEL),
            gdn_states,
            new_k.reshape(n_p, n_even, t_p, H_B, 2 * DQK_B),
            new_v.reshape(n_p, n_even, t_p, H_B, DV_B),
            gla_states)
```
